```python
import jax
import jax.numpy as jnp
from jax import lax
import numpy as np

D_MODEL = 2048
BATCH = 2
SEQ = 4096
DEPTH = 2
DEC_BATCH = 32
DEC_SEQ = 16
PAST_LEN = 2048

CHUNK = 64
N_EVEN = (DEPTH + 1) // 2
N_ODD = DEPTH // 2
A_CHUNK = 128
A_HEADS = 8
A_DIM = 128
A_WIDTH = A_HEADS * A_DIM
B_HEADS = 8
B_DIM = 128
B_WIDTH = B_HEADS * B_DIM
B_PREV_CHUNKS = 8
B_WINDOW = B_PREV_CHUNKS * CHUNK
B_BAND = (B_PREV_CHUNKS + 1) * CHUNK
REL_CLIP = 128
C_HEADS = 16
C_DIM = 128
C_WIDTH = C_HEADS * C_DIM
SB_BLOCK = 128
AB_IN = 2 * A_WIDTH + 3 * B_WIDTH
AB_OUT = A_WIDTH + B_WIDTH
C_IN = 3 * C_WIDTH
N_EXPERTS = 64
N_GROUPS = 8
E_PER_GROUP = N_EXPERTS // N_GROUPS
TOPK_GROUPS = 4
TOP_K = 8
EXPERT_FF = 512
SHARED_FF = 512
ROUTED_SCALE = 2.5
EXPERT_BLOCK = 128
PLE_DIM = 256
LN_EPS = 1e-5
ALPHA = (2 * DEPTH) ** 0.25
BETA = (8 * DEPTH) ** -0.25
NEG = -1e9

kernel_name = "hybrid_streaming_gmlp_band_stickbreak_moe_step"


def layer_norm(x, g, b):
    xf = x.astype(jnp.float32)
    mu = jnp.mean(xf, axis=-1, keepdims=True)
    xc = xf - mu
    var = jnp.mean(xc * xc, axis=-1, keepdims=True)
    return (xc * lax.rsqrt(var + LN_EPS) * g.astype(jnp.float32) + b.astype(jnp.float32)).astype(x.dtype)


def swiglu(x, wg, wu, wd):
    return (jax.nn.silu(x @ wg) * (x @ wu)) @ wd


def sgu_prepare(h_u, h_v, ln_g, ln_b):
    lead = h_u.shape[:-1]
    u = jax.nn.gelu(h_u, approximate=False)
    v = jax.nn.gelu(h_v, approximate=False).reshape(*lead, A_HEADS, A_DIM)
    return u, layer_norm(v, ln_g, ln_b)


def sgu_mix(v, w_s, b_s):
    rows = v.shape[2]
    w = jnp.tril(w_s[:, :rows, :rows])
    return jnp.einsum('hrs,bcshd->bcrhd', w, v) + b_s[:, :rows].T[:, :, None]


def band_mask(qpos, kpos):
    dc = qpos // CHUNK - kpos // CHUNK
    return (kpos >= 0) & (dc >= 0) & (dc <= B_PREV_CHUNKS)


def rel_bias(table, qpos, kpos):
    idx = jnp.clip(qpos[:, None] - kpos[None, :], -REL_CLIP, REL_CLIP) + REL_CLIP
    return table[:, idx]


def band_attend(q, k, v, bias, mask):
    s = jnp.einsum('bgqhd,bgkhd->bghqk', q, k).astype(jnp.float32) * (B_DIM ** -0.5) + bias.astype(jnp.float32)
    s = jnp.where(mask[None, :, None], s, NEG)
    w = jax.nn.softmax(s, axis=-1).astype(v.dtype)
    return jnp.einsum('bghqk,bgkhd->bgqhd', w, v)


def split_ab(h):
    return jnp.split(h, [A_WIDTH, 2 * A_WIDTH, 2 * A_WIDTH + B_WIDTH, 2 * A_WIDTH + 2 * B_WIDTH], axis=-1)


def ab_mixer_prompt(x, w_in, w_out, sgu_w, sgu_b, sgu_g, sgu_bn, rel_tab):
    bsz, seq, _ = x.shape
    hu, hv, q, k, v = split_ab(x @ w_in)
    u, va = sgu_prepare(hu, hv, sgu_g, sgu_bn)
    n_ach = seq // A_CHUNK
    a_out = u * sgu_mix(va.reshape(bsz, n_ach, A_CHUNK, A_HEADS, A_DIM), sgu_w, sgu_b).reshape(bsz, seq, A_WIDTH)
    q = q.reshape(bsz, seq, B_HEADS, B_DIM)
    k = k.reshape(bsz, seq, B_HEADS, B_DIM)
    v = v.reshape(bsz, seq, B_HEADS, B_DIM)
    nc = seq // CHUNK
    pad = ((0, 0), (B_WINDOW, 0), (0, 0), (0, 0))
    band = jnp.arange(nc)[:, None] + jnp.arange(B_PREV_CHUNKS + 1)[None, :]
    kb = jnp.pad(k, pad).reshape(bsz, nc + B_PREV_CHUNKS, CHUNK, B_HEADS, B_DIM)[:, band]
    vb = jnp.pad(v, pad).reshape(bsz, nc + B_PREV_CHUNKS, CHUNK, B_HEADS, B_DIM)[:, band]
    kb = kb.reshape(bsz, nc, B_BAND, B_HEADS, B_DIM)
    vb = vb.reshape(bsz, nc, B_BAND, B_HEADS, B_DIM)
    qpos = jnp.arange(nc)[:, None] * CHUNK + jnp.arange(CHUNK)[None, :]
    kpos = qpos[:, :1] - B_WINDOW + jnp.arange(B_BAND)[None, :]
    mask = band_mask(qpos[:, :, None], kpos[:, None, :])
    bias = rel_bias(rel_tab, qpos[0], kpos[0])
    b_out = band_attend(q.reshape(bsz, nc, CHUNK, B_HEADS, B_DIM), kb, vb, bias, mask).reshape(bsz, seq, B_WIDTH)
    y = jnp.concatenate([a_out, b_out], axis=-1) @ w_out
    keep = min(B_WINDOW, seq)
    return y, k[:, seq - keep:], v[:, seq - keep:]


def ab_mixer_sample(x, cache_k, cache_v, w_in, w_out, sgu_w, sgu_b, sgu_g, sgu_bn, rel_tab):
    bsz, n, _ = x.shape
    hu, hv, q, k, v = split_ab(x @ w_in)
    u, va = sgu_prepare(hu, hv, sgu_g, sgu_bn)
    a_out = u * sgu_mix(va[:, None], sgu_w, sgu_b).reshape(bsz, n, A_WIDTH)
    q = q.reshape(bsz, n, B_HEADS, B_DIM)
    k = k.reshape(bsz, n, B_HEADS, B_DIM)
    v = v.reshape(bsz, n, B_HEADS, B_DIM)
    n_cache = cache_k.shape[1]
    kk = jnp.concatenate([cache_k, k], axis=1)
    vv = jnp.concatenate([cache_v, v], axis=1)
    qpos = PAST_LEN + jnp.arange(n)
    kpos = jnp.concatenate([PAST_LEN - n_cache + jnp.arange(n_cache), qpos])
    mask = band_mask(qpos[:, None], kpos[None, :])[None]
    bias = rel_bias(rel_tab, qpos, kpos)
    b_out = band_attend(q[:, None], kk[:, None], vv[:, None], bias, mask)[:, 0].reshape(bsz, n, B_WIDTH)
    y = jnp.concatenate([a_out, b_out], axis=-1) @ w_out
    return y, va, k, v


def stick_breaking(q, k, v, qpos, kpos):
    z = jnp.einsum('bqhd,bkhd->bhqk', q, k).astype(jnp.float32) * (C_DIM ** -0.5)
    z = jnp.where(kpos[None, :] < qpos[:, None], z, NEG)
    sp = jax.nn.softplus(z)
    later = lax.cumsum(sp, axis=3, reverse=True) - sp
    w = jnp.exp(jax.nn.log_sigmoid(z) - later)
    return jnp.einsum('bhqk,bkhd->bqhd', w.astype(v.dtype), v)


def split_c(h, bsz, n):
    q, k, v = jnp.split(h, 3, axis=-1)
    shp = (bsz, n, C_HEADS, C_DIM)
    return q.reshape(shp), k.reshape(shp), v.reshape(shp)


def c_mixer_prompt(x, w_in, w_out):
    bsz, seq, _ = x.shape
    q, k, v = split_c(x @ w_in, bsz, seq)
    nb = seq // SB_BLOCK
    kpos = jnp.arange(seq)
    q_blocks = q.reshape(bsz, nb, SB_BLOCK, C_HEADS, C_DIM).swapaxes(0, 1)

    def one_block(args):
        q_blk, start = args
        return stick_breaking(q_blk, k, v, start + jnp.arange(SB_BLOCK), kpos)

    o = lax.map(one_block, (q_blocks, jnp.arange(nb) * SB_BLOCK))
    o = o.swapaxes(0, 1).reshape(bsz, seq, C_WIDTH)
    return o @ w_out, k, v


def c_mixer_sample(x, cache_k, cache_v, w_in, w_out):
    bsz, n, _ = x.shape
    q, k, v = split_c(x @ w_in, bsz, n)
    kk = jnp.concatenate([cache_k, k], axis=1)
    vv = jnp.concatenate([cache_v, v], axis=1)
    o = stick_breaking(q, kk, vv, PAST_LEN + jnp.arange(n), jnp.arange(PAST_LEN + n))
    return o.reshape(bsz, n, C_WIDTH) @ w_out, k, v


def grouped_experts(xt, eidx, gate, w_gate, w_up, w_down):
    n_tok = xt.shape[0]
    n_pairs = n_tok * TOP_K
    flat_e = eidx.reshape(-1)
    order = jnp.argsort(flat_e)
    sorted_e = flat_e[order]
    sorted_tok = (order // TOP_K).astype(jnp.int32)
    sorted_gate = gate.reshape(-1)[order]
    counts = jnp.bincount(flat_e, length=N_EXPERTS)
    padded = (counts + EXPERT_BLOCK - 1) // EXPERT_BLOCK * EXPERT_BLOCK
    pad_end = jnp.cumsum(padded)
    pad_start = pad_end - padded
    start = jnp.cumsum(counts) - counts
    dest = pad_start[sorted_e] + jnp.arange(n_pairs) - start[sorted_e]
    n_blocks = -(-(n_pairs + N_EXPERTS * (EXPERT_BLOCK - 1)) // EXPERT_BLOCK)
    n_slots = n_blocks * EXPERT_BLOCK
    slot_tok = jnp.zeros((n_slots,), jnp.int32).at[dest].set(sorted_tok)
    slot_gate = jnp.zeros((n_slots,), jnp.float32).at[dest].set(sorted_gate)
    block_e = jnp.minimum(jnp.searchsorted(pad_end, jnp.arange(n_blocks) * EXPERT_BLOCK, side='right'), N_EXPERTS - 1)

    def run_block(args):
        tok, g, e = args
        xb = xt[tok]
        h = jax.nn.silu(xb @ w_gate[e]) * (xb @ w_up[e])
        return (h @ w_down[e]) * g[:, None].astype(xt.dtype)

    out = lax.map(run_block, (slot_tok.reshape(n_blocks, EXPERT_BLOCK),
                              slot_gate.reshape(n_blocks, EXPERT_BLOCK), block_e))
    return jax.ops.segment_sum(out.reshape(n_slots, D_MODEL), slot_tok, num_segments=n_tok)


def moe_ffn(x, w_router, b_router, w_gate, w_up, w_down, ws_gate, ws_up, ws_down):
    shape = x.shape
    xt = x.reshape(-1, D_MODEL)
    n_tok = xt.shape[0]
    scores = jax.nn.sigmoid((xt @ w_router).astype(jnp.float32))
    sel = scores + b_router.astype(jnp.float32)
    grp = lax.top_k(sel.reshape(n_tok, N_GROUPS, E_PER_GROUP), 2)[0].sum(-1)
    _, gidx = lax.top_k(grp, TOPK_GROUPS)
    gmask = jax.nn.one_hot(gidx, N_GROUPS, dtype=jnp.float32).sum(-2) > 0
    sel = jnp.where(jnp.repeat(gmask, E_PER_GROUP, axis=-1), sel, NEG)
    _, eidx = lax.top_k(sel, TOP_K)
    gate = jnp.take_along_axis(scores, eidx, axis=-1)
    gate = gate / jnp.sum(gate, axis=-1, keepdims=True) * ROUTED_SCALE
    routed = grouped_experts(xt, eidx, gate, w_gate, w_up, w_down)
    shared = swiglu(xt, ws_gate, ws_up, ws_down)
    return (routed + shared).reshape(shape)


def post_block(x, mix, p, g1, b1, g2, b2, w_router, b_router, w_gate, w_up, w_down,
               ws_gate, ws_up, ws_down, w_ple, w_ple_gate):
    x = layer_norm(ALPHA * x + mix, g1, b1)
    x = layer_norm(ALPHA * x + moe_ffn(x, w_router, b_router, w_gate, w_up, w_down, ws_gate, ws_up, ws_down), g2, b2)
    return x + (p @ w_ple) * jax.nn.sigmoid(x @ w_ple_gate)


def setup_inputs(seed: int = 0) -> dict:
    key = jax.random.key(seed)
    ks = iter(jax.random.split(key, 40))

    def nrm(shape, scale):
        return jax.random.normal(next(ks), shape, jnp.float32) * scale

    b_len = min(B_WINDOW, PAST_LEN)
    return {
        "x_prompt": nrm((BATCH, SEQ, D_MODEL), 1.0),
        "x_sample": nrm((DEC_BATCH, DEC_SEQ, D_MODEL), 1.0),
        "cache_b_k": nrm((N_EVEN, DEC_BATCH, b_len, B_HEADS, B_DIM), 1.0),
        "cache_b_v": nrm((N_EVEN, DEC_BATCH, b_len, B_HEADS, B_DIM), 1.0),
        "cache_c_k": nrm((N_ODD, DEC_BATCH, PAST_LEN, C_HEADS, C_DIM), 1.0),
        "cache_c_v": nrm((N_ODD, DEC_BATCH, PAST_LEN, C_HEADS, C_DIM), 1.0),
        "p_prompt": nrm((DEPTH, BATCH, SEQ, PLE_DIM), 1.0),
        "p_sample": nrm((DEPTH, DEC_BATCH, DEC_SEQ, PLE_DIM), 1.0),
        "w_in_ab": nrm((N_EVEN, D_MODEL, AB_IN), D_MODEL ** -0.5),
        "w_out_ab": nrm((N_EVEN, AB_OUT, D_MODEL), BETA * AB_OUT ** -0.5),
        "sgu_w": nrm((N_EVEN, A_HEADS, A_CHUNK, A_CHUNK), A_CHUNK ** -0.5),
        "sgu_b": 1.0 + nrm((N_EVEN, A_HEADS, A_CHUNK), 0.1),
        "sgu_ln_g": 1.0 + nrm((N_EVEN, A_HEADS, A_DIM), 0.01),
        "sgu_ln_b": nrm((N_EVEN, A_HEADS, A_DIM), 0.01),
        "rel_bias_tab": nrm((N_EVEN, B_HEADS, 2 * REL_CLIP + 1), 0.1),
        "w_in_c": nrm((N_ODD, D_MODEL, C_IN), D_MODEL ** -0.5),
        "w_out_c": nrm((N_ODD, C_WIDTH, D_MODEL), BETA * C_WIDTH ** -0.5),
        "ln_mix_g": 1.0 + nrm((DEPTH, D_MODEL), 0.01),
        "ln_mix_b": nrm((DEPTH, D_MODEL), 0.01),
        "ln_ffn_g": 1.0 + nrm((DEPTH, D_MODEL), 0.01),
        "ln_ffn_b": nrm((DEPTH, D_MODEL), 0.01),
        "w_router": nrm((DEPTH, D_MODEL, N_EXPERTS), D_MODEL ** -0.5),
        "b_router": nrm((DEPTH, N_EXPERTS), 0.01),
        "w_gate": nrm((DEPTH, N_EXPERTS, D_MODEL, EXPERT_FF), D_MODEL ** -0.5),
        "w_up": nrm((DEPTH, N_EXPERTS, D_MODEL, EXPERT_FF), D_MODEL ** -0.5),
        "w_down": nrm((DEPTH, N_EXPERTS, EXPERT_FF, D_MODEL), BETA * EXPERT_FF ** -0.5),
        "ws_gate": nrm((DEPTH, D_MODEL, SHARED_FF), D_MODEL ** -0.5),
        "ws_up": nrm((DEPTH, D_MODEL, SHARED_FF), D_MODEL ** -0.5),
        "ws_down": nrm((DEPTH, SHARED_FF, D_MODEL), BETA * SHARED_FF ** -0.5),
        "w_ple": nrm((DEPTH, PLE_DIM, D_MODEL), PLE_DIM ** -0.5),
        "w_ple_gate": nrm((DEPTH, D_MODEL, D_MODEL), D_MODEL ** -0.5),
    }


def reference(x_prompt, x_sample, cache_b_k, cache_b_v, cache_c_k, cache_c_v, p_prompt, p_sample,
              w_in_ab, w_out_ab, sgu_w, sgu_b, sgu_ln_g, sgu_ln_b, rel_bias_tab, w_in_c, w_out_c,
              ln_mix_g, ln_mix_b, ln_ffn_g, ln_ffn_b, w_router, b_router, w_gate, w_up, w_down,
              ws_gate, ws_up, ws_down, w_ple, w_ple_gate):
    xp, xs = x_prompt, x_sample
    bk_p, bv_p, bk_s, bv_s, av_s = [], [], [], [], []
    ck_p, cv_p, ck_s, cv_s = [], [], [], []
    for i in range(DEPTH):
        j = i // 2
        if i % 2 == 0:
            a_w = (w_in_ab[j], w_out_ab[j], sgu_w[j], sgu_b[j], sgu_ln_g[j], sgu_ln_b[j], rel_bias_tab[j])
            mp, k_p, v_p = ab_mixer_prompt(xp, *a_w)
            ms, va_s, k_s, v_s = ab_mixer_sample(xs, cache_b_k[j], cache_b_v[j], *a_w)
            bk_p.append(k_p)
            bv_p.append(v_p)
            bk_s.append(k_s)
            bv_s.append(v_s)
            av_s.append(va_s)
        else:
            mp, k_p, v_p = c_mixer_prompt(xp, w_in_c[j], w_out_c[j])
            ms, k_s, v_s = c_mixer_sample(xs, cache_c_k[j], cache_c_v[j], w_in_c[j], w_out_c[j])
            ck_p.append(k_p)
            cv_p.append(v_p)
            ck_s.append(k_s)
            cv_s.append(v_s)
        l_w = (ln_mix_g[i], ln_mix_b[i], ln_ffn_g[i], ln_ffn_b[i], w_router[i], b_router[i],
               w_gate[i], w_up[i], w_down[i], ws_gate[i], ws_up[i], ws_down[i], w_ple[i], w_ple_gate[i])
        xp = post_block(xp, mp, p_prompt[i], *l_w)
        xs = post_block(xs, ms, p_sample[i], *l_w)
    return (xp, xs, jnp.stack(bk_p), jnp.stack(bv_p), jnp.stack(bk_s), jnp.stack(bv_s), jnp.stack(av_s),
            jnp.stack(ck_p), jnp.stack(cv_p), jnp.stack(ck_s), jnp.stack(cv_s))
```

```python
import functools

import numpy as np
import jax
import jax.numpy as jnp
from jax import lax
from jax.experimental import pallas as pl
from jax.experimental.pallas import tpu as pltpu

D_MODEL = 2048
BATCH = 2
SEQ = 4096
DEPTH = 2
DEC_BATCH = 32
DEC_SEQ = 16
PAST_LEN = 2048

CHUNK = 64
A_CHUNK = 128
A_HEADS = 8
A_DIM = 128
A_WIDTH = A_HEADS * A_DIM
B_HEADS = 8
B_DIM = 128
B_WIDTH = B_HEADS * B_DIM
B_PREV_CHUNKS = 8
B_WINDOW = B_PREV_CHUNKS * CHUNK
REL_CLIP = 128
C_HEADS = 16
C_DIM = 128
C_WIDTH = C_HEADS * C_DIM
N_EXPERTS = 64
N_GROUPS = 8
E_PER_GROUP = N_EXPERTS // N_GROUPS
TOPK_GROUPS = 4
TOP_K = 8
EXPERT_FF = 512
SHARED_FF = 512
ROUTED_SCALE = 2.5
PLE_DIM = 256
LN_EPS = 1e-5
ALPHA = (2 * DEPTH) ** 0.25
NEG = -1e9

LANES = 128
VMEM_LIMIT_BYTES = 56 * 1024 * 1024

BF16 = jnp.bfloat16
F32 = jnp.float32


def _params(*sem):
    return pltpu.CompilerParams(dimension_semantics=sem, vmem_limit_bytes=VMEM_LIMIT_BYTES)


def _tile(n, pref):
    if n <= pref:
        return n
    for t in range(pref, 7, -1):
        if n % t == 0 and t % 8 == 0:
            return t
    return n


def _dot(a, b):
    return jnp.dot(a, b, preferred_element_type=F32)


def _dot_nt(a, b):
    return lax.dot_general(a, b, (((1,), (1,)), ((), ())), preferred_element_type=F32)


def _layer_norm(z, g, b):
    mu = jnp.mean(z, axis=-1, keepdims=True)
    zc = z - mu
    var = jnp.mean(zc * zc, axis=-1, keepdims=True)
    return zc * lax.rsqrt(var + LN_EPS) * g + b


def _gelu(x):
    return x * (lax.erf(x * np.float32(1.0 / np.sqrt(2.0))) + 1.0) * 0.5


def _softplus(z):
    return jnp.maximum(z, 0.0) + jnp.log1p(jnp.exp(-jnp.abs(z)))


def _mm_kernel(x_ref, w_ref, *o_refs):
    acc = _dot(x_ref[...].astype(BF16), w_ref[...])
    for o in o_refs:
        o[...] = acc.astype(o.dtype)


def _matmul(x, w, col0, ncols, out_dtypes, name):
    m, k = x.shape
    tm = _tile(m, 512)
    tn = _tile(ncols, 1024)
    assert col0 % tn == 0
    nb0 = col0 // tn
    return pl.pallas_call(
        _mm_kernel,
        grid=(ncols // tn, m // tm),
        in_specs=[pl.BlockSpec((tm, k), lambda n, i: (i, 0)),
                  pl.BlockSpec((k, tn), lambda n, i: (0, n + nb0))],
        out_specs=[pl.BlockSpec((tm, tn), lambda n, i: (i, n)) for _ in out_dtypes],
        out_shape=[jax.ShapeDtypeStruct((m, ncols), d) for d in out_dtypes],
        compiler_params=_params("arbitrary", "arbitrary"),
        name=name,
    )(x, w)


def _sgu_kernel(hu_ref, hv_ref, w_ref, bs_ref, g_ref, b_ref, *rest, emit_v):
    a_ref = rest[-2] if emit_v else rest[-1]
    rows = hu_ref.shape[0]
    r_i = lax.broadcasted_iota(jnp.int32, (rows, rows), 0)
    c_i = lax.broadcasted_iota(jnp.int32, (rows, rows), 1)
    causal = c_i <= r_i
    for h in range(A_HEADS):
        sl = slice(h * A_DIM, (h + 1) * A_DIM)
        u = _gelu(hu_ref[:, sl])
        v = _layer_norm(_gelu(hv_ref[:, sl]), g_ref[h:h + 1, :], b_ref[h:h + 1, :])
        w = jnp.where(causal, w_ref[h], 0.0).astype(BF16)
        mix = _dot(w, v.astype(BF16)) + bs_ref[:, h:h + 1]
        a_ref[:, sl] = (u * mix).astype(a_ref.dtype)
        if emit_v:
            rest[-1][:, sl] = v


def _sgu(hu, hv, w_s, bs_t, g, b, rows, row0, n_chunks, a_prev):
    t = hu.shape[0]
    emit_v = a_prev is not None
    off = row0 // rows
    row_spec = pl.BlockSpec((rows, A_WIDTH), lambda c: (c + off, 0))
    full = lambda a: pl.BlockSpec(a.shape, lambda c: (0,) * a.ndim)
    in_specs = [row_spec, row_spec, full(w_s), full(bs_t), full(g), full(b)]
    args = [hu, hv, w_s, bs_t, g, b]
    out_specs = [row_spec]
    out_shape = [jax.ShapeDtypeStruct((t, A_WIDTH), BF16)]
    aliases = {}
    if emit_v:
        in_specs.append(pl.BlockSpec(memory_space=pl.ANY))
        args.append(a_prev)
        aliases = {6: 0}
        out_specs.append(pl.BlockSpec((rows, A_WIDTH), lambda c: (c, 0)))
        out_shape.append(jax.ShapeDtypeStruct((n_chunks * rows, A_WIDTH), F32))
    return pl.pallas_call(
        functools.partial(_sgu_kernel, emit_v=emit_v),
        grid=(n_chunks,),
        in_specs=in_specs, out_specs=out_specs, out_shape=out_shape,
        input_output_aliases=aliases,
        compiler_params=_params("arbitrary"),
        name="sgu_sample" if emit_v else "sgu_prompt",
    )(*args)


BAND_TQ = 2 * CHUNK
BAND_NWB = B_WINDOW // BAND_TQ + 1


def _band_prompt_kernel(q_ref, k_ref, v_ref, bias_ref, o_ref):
    t = pl.program_id(2)
    tq = BAND_TQ
    q = q_ref[...]
    shift = int(np.log2(CHUNK))
    q_chunk = lax.shift_right_arithmetic(t * tq + lax.broadcasted_iota(jnp.int32, (tq, 1), 0), shift)
    lane = lax.broadcasted_iota(jnp.int32, (1, tq), 1)
    s_blocks = []
    for j in range(BAND_NWB):
        kb = t - (BAND_NWB - 1) + j
        start = pl.multiple_of(jnp.maximum(kb, 0) * tq, tq)
        s = _dot_nt(q, k_ref[pl.ds(start, tq), :]) * np.float32(B_DIM ** -0.5) + bias_ref[0, :, j * tq:(j + 1) * tq]
        kpos = kb * tq + lane
        dc = q_chunk - lax.shift_right_arithmetic(kpos, shift)
        valid = (kpos >= 0) & (dc >= 0) & (dc <= B_PREV_CHUNKS)
        s_blocks.append(jnp.where(valid, s, NEG))
    m = s_blocks[0].max(axis=-1, keepdims=True)
    for s in s_blocks[1:]:
        m = jnp.maximum(m, s.max(axis=-1, keepdims=True))
    acc = jnp.zeros((tq, B_DIM), F32)
    den = jnp.zeros((tq, 1), F32)
    for j, s in enumerate(s_blocks):
        kb = t - (BAND_NWB - 1) + j
        start = pl.multiple_of(jnp.maximum(kb, 0) * tq, tq)
        p = jnp.exp(s - m)
        den = den + p.sum(axis=-1, keepdims=True)
        acc = acc + _dot(p.astype(BF16), v_ref[pl.ds(start, tq), :])
    o_ref[...] = (acc / den).astype(o_ref.dtype)


def _band_prompt(q, k, v, bias):
    t_all = q.shape[0]
    n_t = SEQ // BAND_TQ
    q_spec = pl.BlockSpec((BAND_TQ, B_DIM), lambda b, h, t: (b * n_t + t, h))
    kv_spec = pl.BlockSpec((SEQ, B_DIM), lambda b, h, t: (b, h))
    return pl.pallas_call(
        _band_prompt_kernel,
        grid=(BATCH, B_HEADS, n_t),
        in_specs=[q_spec, kv_spec, kv_spec,
                  pl.BlockSpec((1, BAND_TQ, BAND_NWB * BAND_TQ), lambda b, h, t: (h, 0, 0))],
        out_specs=q_spec,
        out_shape=jax.ShapeDtypeStruct((t_all, B_WIDTH), BF16),
        compiler_params=_params("arbitrary", "arbitrary", "arbitrary"),
        name="band_prompt",
    )(q, k, v, bias)


def _band_sample_kernel(q_ref, kn_ref, vn_ref, kc_ref, vc_ref, bc_ref, bn_ref, mc_ref, mn_ref, prev_ref, o_ref):
    del prev_ref
    scale = np.float32(B_DIM ** -0.5)
    for h in range(B_HEADS):
        sl = slice(h * B_DIM, (h + 1) * B_DIM)
        q = q_ref[:, sl]
        s_c = _dot_nt(q, kc_ref[:, sl].astype(BF16)) * scale + bc_ref[h]
        s_n = _dot_nt(q, kn_ref[:, sl]) * scale + bn_ref[h]
        s_c = jnp.where(mc_ref[...] > 0, s_c, NEG)
        s_n = jnp.where(mn_ref[...] > 0, s_n, NEG)
        m = jnp.maximum(s_c.max(axis=-1, keepdims=True), s_n.max(axis=-1, keepdims=True))
        p_c = jnp.exp(s_c - m)
        p_n = jnp.exp(s_n - m)
        den = p_c.sum(axis=-1, keepdims=True) + p_n.sum(axis=-1, keepdims=True)
        acc = _dot(p_c.astype(BF16), vc_ref[:, sl].astype(BF16)) + _dot(p_n.astype(BF16), vn_ref[:, sl])
        o_ref[:, sl] = (acc / den).astype(o_ref.dtype)


def _band_sample(q, k, v, cache_k, cache_v, bias_c, bias_n, mask_c, mask_n, b_prev):
    n = DEC_SEQ
    off = BATCH * SEQ // n
    n_cache = cache_k.shape[1]
    row_spec = pl.BlockSpec((n, B_WIDTH), lambda b: (b + off, 0))
    cache_spec = pl.BlockSpec((None, n_cache, B_WIDTH), lambda b: (b, 0, 0))
    full = lambda a: pl.BlockSpec(a.shape, lambda b: (0,) * a.ndim)
    return pl.pallas_call(
        _band_sample_kernel,
        grid=(DEC_BATCH,),
        in_specs=[row_spec, row_spec, row_spec, cache_spec, cache_spec,
                  full(bias_c), full(bias_n), full(mask_c), full(mask_n),
                  pl.BlockSpec(memory_space=pl.ANY)],
        out_specs=row_spec,
        out_shape=jax.ShapeDtypeStruct(b_prev.shape, BF16),
        input_output_aliases={9: 0},
        compiler_params=_params("arbitrary"),
        name="band_sample",
    )(q, k, v, cache_k, cache_v, bias_c, bias_n, mask_c, mask_n, b_prev)


SB_TQ = 256
SB_TK = 256
SB_SAMPLE_TK = 512


def _sb_weights(z, carry, upper):
    sp = _softplus(z)
    hi = sp.astype(BF16)
    lo = (sp - hi.astype(F32)).astype(BF16)
    later = _dot(hi, upper) + _dot(lo, upper)
    w = jnp.exp((z - sp) - (carry + later))
    return w, carry + later[:, 0:1] + sp[:, 0:1]


def _upper(n):
    r_i = lax.broadcasted_iota(jnp.int32, (n, n), 0)
    c_i = lax.broadcasted_iota(jnp.int32, (n, n), 1)
    return (r_i > c_i).astype(BF16)


def _sb_prompt_kernel(q_ref, k_ref, v_ref, o_ref):
    i = pl.program_id(2)
    q = q_ref[...]
    upper = _upper(SB_TK)
    scale = np.float32(C_DIM ** -0.5)
    r_i = lax.broadcasted_iota(jnp.int32, (SB_TQ, SB_TK), 0)
    c_i = lax.broadcasted_iota(jnp.int32, (SB_TQ, SB_TK), 1)

    def block(j, carry, acc, diagonal):
        start = pl.multiple_of(j * SB_TK, SB_TK)
        z = _dot_nt(q, k_ref[pl.ds(start, SB_TK), :]) * scale
        if diagonal:
            z = jnp.where(c_i < r_i, z, NEG)
        w, carry = _sb_weights(z, carry, upper)
        return carry, acc + _dot(w.astype(BF16), v_ref[pl.ds(start, SB_TK), :])

    carry, acc = block(i, jnp.zeros((SB_TQ, 1), F32), jnp.zeros((SB_TQ, C_DIM), F32), True)

    def body(jj, state):
        return block(i - 1 - jj, state[0], state[1], False)

    carry, acc = lax.fori_loop(0, i, body, (carry, acc))
    o_ref[...] = acc.astype(o_ref.dtype)


def _sb_prompt(q, k, v):
    assert SB_TQ == SB_TK
    t_all = q.shape[0]
    n_q = SEQ // SB_TQ
    q_spec = pl.BlockSpec((SB_TQ, C_DIM), lambda b, h, i: (b * n_q + i, h))
    kv_spec = pl.BlockSpec((SEQ, C_DIM), lambda b, h, i: (b, h))
    return pl.pallas_call(
        _sb_prompt_kernel,
        grid=(BATCH, C_HEADS, n_q),
        in_specs=[q_spec, kv_spec, kv_spec],
        out_specs=q_spec,
        out_shape=jax.ShapeDtypeStruct((t_all, C_WIDTH), BF16),
        compiler_params=_params("arbitrary", "arbitrary", "arbitrary"),
        name="sb_prompt",
    )(q, k, v)


def _sb_sample_kernel(q_ref, kn_ref, vn_ref, kc_ref, vc_ref, prev_ref, o_ref, z_ref, w_ref, carry_ref, acc_ref):
    del prev_ref
    jj = pl.program_id(1)
    n = DEC_SEQ
    tk = kc_ref.shape[0]
    scale = np.float32(C_DIM ** -0.5)

    def all_heads(k_of, v_of, width, mask):
        for h in range(C_HEADS):
            sl = slice(h * C_DIM, (h + 1) * C_DIM)
            z_ref[h * n:(h + 1) * n, 0:width] = _dot_nt(q_ref[:, sl], k_of(sl)) * scale
        z = z_ref[:, 0:width]
        if mask is not None:
            z = jnp.where(mask, z, NEG)
        w, carry = _sb_weights(z, carry_ref[...], _upper(width))
        carry_ref[...] = carry
        w_ref[:, 0:width] = w.astype(BF16)
        for h in range(C_HEADS):
            sl = slice(h * C_DIM, (h + 1) * C_DIM)
            acc_ref[:, sl] += _dot(w_ref[h * n:(h + 1) * n, 0:width], v_of(sl))

    @pl.when(jj == 0)
    def _():
        carry_ref[...] = jnp.zeros_like(carry_ref)
        acc_ref[...] = jnp.zeros_like(acc_ref)
        r_i = lax.broadcasted_iota(jnp.int32, (C_HEADS * n, n), 0) % n
        c_i = lax.broadcasted_iota(jnp.int32, (C_HEADS * n, n), 1)
        all_heads(lambda sl: kn_ref[:, sl], lambda sl: vn_ref[:, sl], n, c_i < r_i)

    all_heads(lambda sl: kc_ref[:, sl].astype(BF16), lambda sl: vc_ref[:, sl].astype(BF16), tk, None)

    @pl.when(jj == pl.num_programs(1) - 1)
    def _():
        o_ref[...] = acc_ref[...].astype(o_ref.dtype)


def _sb_sample(q, k, v, cache_k, cache_v, o_prev):
    n = DEC_SEQ
    off = BATCH * SEQ // n
    past = cache_k.shape[1]
    tk = _tile(past, SB_SAMPLE_TK)
    n_kb = past // tk
    row_spec = pl.BlockSpec((n, C_WIDTH), lambda b, j: (b + off, 0))
    cache_spec = pl.BlockSpec((None, tk, C_WIDTH), lambda b, j: (b, n_kb - 1 - j, 0))
    return pl.pallas_call(
        _sb_sample_kernel,
        grid=(DEC_BATCH, n_kb),
        in_specs=[row_spec, row_spec, row_spec, cache_spec, cache_spec, pl.BlockSpec(memory_space=pl.ANY)],
        out_specs=row_spec,
        out_shape=jax.ShapeDtypeStruct(o_prev.shape, BF16),
        scratch_shapes=[pltpu.VMEM((C_HEADS * n, tk), F32), pltpu.VMEM((C_HEADS * n, tk), BF16),
                        pltpu.VMEM((C_HEADS * n, 1), F32), pltpu.VMEM((n, C_WIDTH), F32)],
        input_output_aliases={5: 0},
        compiler_params=_params("arbitrary", "arbitrary"),
        name="sb_sample",
    )(q, k, v, cache_k, cache_v, o_prev)


def _outproj_kernel(*refs, n_lhs):
    lhs = refs[:n_lhs]
    w_ref, x_ref, g_ref, b_ref, wr_ref, x1_ref, x1b_ref, lg_ref = refs[n_lhs:]
    k0 = 0
    y = None
    for a in lhs:
        kk = a.shape[1]
        part = _dot(a[...], w_ref[k0:k0 + kk, :])
        y = part if y is None else y + part
        k0 += kk
    x1 = _layer_norm(np.float32(ALPHA) * x_ref[...] + y, g_ref[...], b_ref[...])
    x1_ref[...] = x1
    x1b_ref[...] = x1.astype(BF16)
    lg_ref[...] = jnp.dot(x1, wr_ref[...], preferred_element_type=F32, precision=lax.Precision.HIGHEST)


def _outproj_ln(lhs, w, x, g, b, w_router, name):
    t = x.shape[0]
    tm = _tile(t, 256)
    row = lambda a: pl.BlockSpec((tm, a.shape[1]), lambda i: (i, 0))
    full = lambda a: pl.BlockSpec(a.shape, lambda i: (0,) * a.ndim)
    return pl.pallas_call(
        functools.partial(_outproj_kernel, n_lhs=len(lhs)),
        grid=(t // tm,),
        in_specs=[row(a) for a in lhs] + [full(w), row(x), full(g), full(b), full(w_router)],
        out_specs=[pl.BlockSpec((tm, D_MODEL), lambda i: (i, 0)), pl.BlockSpec((tm, D_MODEL), lambda i: (i, 0)),
                   pl.BlockSpec((tm, N_EXPERTS), lambda i: (i, 0))],
        out_shape=[jax.ShapeDtypeStruct((t, D_MODEL), F32), jax.ShapeDtypeStruct((t, D_MODEL), BF16),
                   jax.ShapeDtypeStruct((t, N_EXPERTS), F32)],
        compiler_params=_params("arbitrary"),
        name=name,
    )(*lhs, w, x, g, b, w_router)


EXPERT_BM = 256


def _expert_kernel(be_ref, nu_ref, x_ref, gate_ref, wg_ref, wu_ref, wd_ref, y_ref):
    i = pl.program_id(0)

    @pl.when(i < nu_ref[0])
    def _():
        x = x_ref[...]
        g = _dot(x, wg_ref[...])
        h = (g * jax.nn.sigmoid(g)) * _dot(x, wu_ref[...])
        y_ref[...] = _dot(h.astype(BF16), wd_ref[...]) * gate_ref[...]

    @pl.when(i >= nu_ref[0])
    def _():
        y_ref[...] = jnp.zeros_like(y_ref)


def _experts(block_e, n_used, xs, gate, wg, wu, wd):
    n_slots = xs.shape[0]
    n_blocks = n_slots // EXPERT_BM
    ff = wg.shape[2]
    grid_spec = pltpu.PrefetchScalarGridSpec(
        num_scalar_prefetch=2,
        grid=(n_blocks,),
        in_specs=[pl.BlockSpec((EXPERT_BM, D_MODEL), lambda i, be, nu: (i, 0)),
                  pl.BlockSpec((EXPERT_BM, 1), lambda i, be, nu: (i, 0)),
                  pl.BlockSpec((None, D_MODEL, ff), lambda i, be, nu: (be[i], 0, 0)),
                  pl.BlockSpec((None, D_MODEL, ff), lambda i, be, nu: (be[i], 0, 0)),
                  pl.BlockSpec((None, ff, D_MODEL), lambda i, be, nu: (be[i], 0, 0))],
        out_specs=pl.BlockSpec((EXPERT_BM, D_MODEL), lambda i, be, nu: (i, 0)),
    )
    return pl.pallas_call(
        _expert_kernel,
        grid_spec=grid_spec,
        out_shape=jax.ShapeDtypeStruct((n_slots, D_MODEL), F32),
        compiler_params=_params("arbitrary"),
        name="experts",
    )(block_e, n_used, xs, gate, wg, wu, wd)


def _route(logits, b_router):
    t = logits.shape[0]
    scores = jax.nn.sigmoid(logits)
    sel = scores + b_router.astype(F32)
    grp = lax.top_k(sel.reshape(t, N_GROUPS, E_PER_GROUP), 2)[0].sum(-1)
    _, gidx = lax.top_k(grp, TOPK_GROUPS)
    gmask = jax.nn.one_hot(gidx, N_GROUPS, dtype=F32).sum(-2) > 0
    sel = jnp.where(jnp.repeat(gmask, E_PER_GROUP, axis=-1), sel, NEG)
    _, eidx = lax.top_k(sel, TOP_K)
    gate = jnp.take_along_axis(scores, eidx, axis=-1)
    gate = gate / jnp.sum(gate, axis=-1, keepdims=True) * ROUTED_SCALE

    n_pairs = t * TOP_K
    flat_e = eidx.reshape(-1)
    order = jnp.argsort(flat_e)
    sorted_e = flat_e[order]
    counts = jnp.bincount(flat_e, length=N_EXPERTS)
    padded = (counts + EXPERT_BM - 1) // EXPERT_BM * EXPERT_BM
    pad_end = jnp.cumsum(padded)
    pad_start = pad_end - padded
    start = jnp.cumsum(counts) - counts
    dest = (pad_start[sorted_e] + jnp.arange(n_pairs) - start[sorted_e]).astype(jnp.int32)
    n_blocks = -(-(n_pairs + N_EXPERTS * (EXPERT_BM - 1)) // EXPERT_BM)
    n_slots = n_blocks * EXPERT_BM
    slot_tok = jnp.zeros((n_slots,), jnp.int32).at[dest].set((order // TOP_K).astype(jnp.int32))
    slot_gate = jnp.zeros((n_slots,), F32).at[dest].set(gate.reshape(-1)[order])
    block_e = jnp.minimum(jnp.searchsorted(pad_end, jnp.arange(n_blocks) * EXPERT_BM, side='right'),
                          N_EXPERTS - 1).astype(jnp.int32)
    pair_slot = jnp.zeros((n_pairs,), jnp.int32).at[order].set(dest).reshape(t, TOP_K)
    n_used = (pad_end[-1] // EXPERT_BM).astype(jnp.int32).reshape(1)
    return slot_tok, slot_gate, block_e, n_used, pair_slot


def _ffn_ln_kernel(xb_ref, x_ref, r_ref, wg_ref, wu_ref, wd_ref, g_ref, b_ref, o_ref):
    xb = xb_ref[...]
    gte = _dot(xb, wg_ref[...])
    h = (gte * jax.nn.sigmoid(gte)) * _dot(xb, wu_ref[...])
    shared = _dot(h.astype(BF16), wd_ref[...])
    z = np.float32(ALPHA) * x_ref[...] + (r_ref[...] + shared)
    o_ref[...] = _layer_norm(z, g_ref[...], b_ref[...])


def _ffn_ln(x1b, x1, routed, wg, wu, wd, g, b):
    t = x1.shape[0]
    tm = _tile(t, 256)
    row = pl.BlockSpec((tm, D_MODEL), lambda i: (i, 0))
    full = lambda a: pl.BlockSpec(a.shape, lambda i: (0,) * a.ndim)
    return pl.pallas_call(
        _ffn_ln_kernel,
        grid=(t // tm,),
        in_specs=[row, row, row, full(wg), full(wu), full(wd), full(g), full(b)],
        out_specs=row,
        out_shape=jax.ShapeDtypeStruct((t, D_MODEL), F32),
        compiler_params=_params("arbitrary"),
        name="ffn_ln",
    )(x1b, x1, routed, wg, wu, wd, g, b)


def _ple_kernel(xf_ref, xs_ref, p_ref, wp_ref, wg_ref, o_ref):
    gate = jax.nn.sigmoid(_dot(xf_ref[...].astype(BF16), wg_ref[...]))
    emb = _dot(p_ref[...].astype(BF16), wp_ref[...])
    o_ref[...] = xs_ref[...] + emb * gate


def _ple(x2, p, w_ple, w_gate):
    t = x2.shape[0]
    tm = _tile(t, 512)
    tn = _tile(D_MODEL, 1024)
    return pl.pallas_call(
        _ple_kernel,
        grid=(D_MODEL // tn, t // tm),
        in_specs=[pl.BlockSpec((tm, D_MODEL), lambda n, i: (i, 0)),
                  pl.BlockSpec((tm, tn), lambda n, i: (i, n)),
                  pl.BlockSpec((tm, p.shape[1]), lambda n, i: (i, 0)),
                  pl.BlockSpec((w_ple.shape[0], tn), lambda n, i: (0, n)),
                  pl.BlockSpec((D_MODEL, tn), lambda n, i: (0, n))],
        out_specs=pl.BlockSpec((tm, tn), lambda n, i: (i, n)),
        out_shape=jax.ShapeDtypeStruct((t, D_MODEL), F32),
        compiler_params=_params("arbitrary", "arbitrary"),
        name="ple",
    )(x2, x2, p, w_ple, w_gate)


def _rel_bias(table, qpos, kpos):
    idx = np.clip(qpos[:, None] - kpos[None, :], -REL_CLIP, REL_CLIP) + REL_CLIP
    return table[:, idx]


def _band_mask(qpos, kpos):
    dc = qpos // CHUNK - kpos // CHUNK
    return (kpos >= 0) & (dc >= 0) & (dc <= B_PREV_CHUNKS)


def _post_block(x, lhs, w_out, p, ln1_g, ln1_b, ln2_g, ln2_b, w_router, b_router,
                w_gate, w_up, w_down, ws_gate, ws_up, ws_down, w_ple, w_ple_gate, name):
    row = lambda a: a.reshape(1, -1)
    x1, x1b, logits = _outproj_ln(lhs, w_out, x, row(ln1_g), row(ln1_b), w_router, name)
    slot_tok, slot_gate, block_e, n_used, pair_slot = _route(logits, b_router)
    xs = jnp.take(x1b, slot_tok, axis=0)
    y = _experts(block_e, n_used, xs, slot_gate.reshape(-1, 1), w_gate, w_up, w_down)
    routed = jnp.take(y, pair_slot, axis=0).sum(axis=1)
    x2 = _ffn_ln(x1b, x1, routed, ws_gate, ws_up, ws_down, row(ln2_g), row(ln2_b))
    return _ple(x2, p, w_ple, w_ple_gate)


def kernel(x_prompt, x_sample, cache_b_k, cache_b_v, cache_c_k, cache_c_v, p_prompt, p_sample, w_in_ab, w_out_ab, sgu_w, sgu_b, sgu_ln_g, sgu_ln_b, rel_bias_tab, w_in_c, w_out_c, ln_mix_g, ln_mix_b, ln_ffn_g, ln_ffn_b, w_router, b_router, w_gate, w_up, w_down, ws_gate, ws_up, ws_down, w_ple, w_ple_gate):
    n_p = BATCH * SEQ
    n_s = DEC_BATCH * DEC_SEQ
    x = jnp.concatenate([x_prompt.reshape(n_p, D_MODEL), x_sample.reshape(n_s, D_MODEL)], axis=0)
    p_all = jnp.concatenate([p_prompt.reshape(DEPTH, n_p, PLE_DIM), p_sample.reshape(DEPTH, n_s, PLE_DIM)], axis=1)
    bf = lambda a: a.astype(BF16)
    outs = {k: [] for k in ("bk_p", "bv_p", "bk_s", "bv_s", "av_s", "ck_p", "cv_p", "ck_s", "cv_s")}

    for i in range(DEPTH):
        j = i // 2
        if i % 2 == 0:
            w_in = bf(w_in_ab[j])
            hu, = _matmul(x, w_in, 0, A_WIDTH, (F32,), "proj_u")
            hv, = _matmul(x, w_in, A_WIDTH, A_WIDTH, (F32,), "proj_v")
            q, = _matmul(x, w_in, 2 * A_WIDTH, B_WIDTH, (BF16,), "proj_q")
            k, kb16 = _matmul(x, w_in, 2 * A_WIDTH + B_WIDTH, B_WIDTH, (F32, BF16), "proj_k")
            v, vb16 = _matmul(x, w_in, 2 * A_WIDTH + 2 * B_WIDTH, B_WIDTH, (F32, BF16), "proj_vv")

            bs_t = sgu_b[j].T
            a_out, = _sgu(hu, hv, sgu_w[j], bs_t, sgu_ln_g[j], sgu_ln_b[j], A_CHUNK, 0, n_p // A_CHUNK, None)
            a_out, va_s = _sgu(hu, hv, sgu_w[j][:, :DEC_SEQ, :DEC_SEQ], bs_t[:DEC_SEQ], sgu_ln_g[j], sgu_ln_b[j],
                               DEC_SEQ, n_p, DEC_BATCH, a_out)

            qp = np.arange(BAND_TQ)
            kp = np.arange(BAND_NWB * BAND_TQ) - (BAND_NWB - 1) * BAND_TQ
            bias_p = _rel_bias(rel_bias_tab[j], qp, kp)
            b_out = _band_prompt(q, kb16, vb16, bias_p)
            n_cache = cache_b_k.shape[2]
            qs = PAST_LEN + np.arange(DEC_SEQ)
            kc = PAST_LEN - n_cache + np.arange(n_cache)
            b_out = _band_sample(
                q, kb16, vb16,
                cache_b_k[j].reshape(DEC_BATCH, n_cache, B_WIDTH), cache_b_v[j].reshape(DEC_BATCH, n_cache, B_WIDTH),
                _rel_bias(rel_bias_tab[j], qs, kc), _rel_bias(rel_bias_tab[j], qs, qs),
                jnp.asarray(_band_mask(qs[:, None], kc[None, :]), F32),
                jnp.asarray(_band_mask(qs[:, None], qs[None, :]), F32), b_out)

            keep = min(B_WINDOW, SEQ)
            kp4 = k[:n_p].reshape(BATCH, SEQ, B_HEADS, B_DIM)
            vp4 = v[:n_p].reshape(BATCH, SEQ, B_HEADS, B_DIM)
            outs["bk_p"].append(kp4[:, SEQ - keep:])
            outs["bv_p"].append(vp4[:, SEQ - keep:])
            outs["bk_s"].append(k[n_p:].reshape(DEC_BATCH, DEC_SEQ, B_HEADS, B_DIM))
            outs["bv_s"].append(v[n_p:].reshape(DEC_BATCH, DEC_SEQ, B_HEADS, B_DIM))
            outs["av_s"].append(va_s.reshape(DEC_BATCH, DEC_SEQ, A_HEADS, A_DIM))
            lhs, w_out, name = [a_out, b_out], bf(w_out_ab[j]), "outproj_ab"
        else:
            w_in = bf(w_in_c[j])
            q, = _matmul(x, w_in, 0, C_WIDTH, (BF16,), "proj_cq")
            k, kb16 = _matmul(x, w_in, C_WIDTH, C_WIDTH, (F32, BF16), "proj_ck")
            v, vb16 = _matmul(x, w_in, 2 * C_WIDTH, C_WIDTH, (F32, BF16), "proj_cv")
            o = _sb_prompt(q, kb16, vb16)
            o = _sb_sample(q, kb16, vb16, cache_c_k[j].reshape(DEC_BATCH, PAST_LEN, C_WIDTH),
                           cache_c_v[j].reshape(DEC_BATCH, PAST_LEN, C_WIDTH), o)
            outs["ck_p"].append(k[:n_p].reshape(BATCH, SEQ, C_HEADS, C_DIM))
            outs["cv_p"].append(v[:n_p].reshape(BATCH, SEQ, C_HEADS, C_DIM))
            outs["ck_s"].append(k[n_p:].reshape(DEC_BATCH, DEC_SEQ, C_HEADS, C_DIM))
            outs["cv_s"].append(v[n_p:].reshape(DEC_BATCH, DEC_SEQ, C_HEADS, C_DIM))
            lhs, w_out, name = [o], bf(w_out_c[j]), "outproj_c"

        x = _post_block(x, lhs, w_out, p_all[i], ln_mix_g[i], ln_mix_b[i], ln_ffn_g[i], ln_ffn_b[i],
                        w_router[i], b_router[i], bf(w_gate[i]), bf(w_up[i]), bf(w_down[i]),
                        bf(ws_gate[i]), bf(ws_up[i]), bf(ws_down[i]), bf(w_ple[i]), bf(w_ple_gate[i]), name)

    st = lambda key: jnp.stack(outs[key])
    return (x[:n_p].reshape(BATCH, SEQ, D_MODEL), x[n_p:].reshape(DEC_BATCH, DEC_SEQ, D_MODEL),
            st("bk_p"), st("bv_p"), st("bk_s"), st("bv_s"), st("av_s"),
            st("ck_p"), st("cv_p"), st("ck_s"), st("cv_s"))
```

```python
import functools

import numpy as np
import jax
import jax.numpy as jnp
from jax import lax
from jax.experimental import pallas as pl
from jax.experimental.pallas import tpu as pltpu

D_MODEL = 2048
BATCH = 2
SEQ = 4096
DEPTH = 2
DEC_BATCH = 32
DEC_SEQ = 16
PAST_LEN = 2048

CHUNK = 64
A_CHUNK = 128
A_HEADS = 8
A_DIM = 128
A_WIDTH = A_HEADS * A_DIM
B_HEADS = 8
B_DIM = 128
B_WIDTH = B_HEADS * B_DIM
B_PREV_CHUNKS = 8
B_WINDOW = B_PREV_CHUNKS * CHUNK
REL_CLIP = 128
C_HEADS = 16
C_DIM = 128
C_WIDTH = C_HEADS * C_DIM
N_EXPERTS = 64
N_GROUPS = 8
E_PER_GROUP = N_EXPERTS // N_GROUPS
TOPK_GROUPS = 4
TOP_K = 8
EXPERT_FF = 512
SHARED_FF = 512
ROUTED_SCALE = 2.5
PLE_DIM = 256
LN_EPS = 1e-5
ALPHA = (2 * DEPTH) ** 0.25
NEG = -1e9

LANES = 128
VMEM_LIMIT_BYTES = 56 * 1024 * 1024

BF16 = jnp.bfloat16
F32 = jnp.float32


def _params(*sem):
    return pltpu.CompilerParams(dimension_semantics=sem, vmem_limit_bytes=VMEM_LIMIT_BYTES)


def _tile(n, pref):
    if n <= pref:
        return n
    for t in range(pref, 7, -1):
        if n % t == 0 and t % 8 == 0:
            return t
    return n


def _dot(a, b):
    return jnp.dot(a, b, preferred_element_type=F32)


def _dot_nt(a, b):
    return lax.dot_general(a, b, (((1,), (1,)), ((), ())), preferred_element_type=F32)


def _layer_norm(z, g, b):
    mu = jnp.mean(z, axis=-1, keepdims=True)
    zc = z - mu
    var = jnp.mean(zc * zc, axis=-1, keepdims=True)
    return zc * lax.rsqrt(var + LN_EPS) * g + b


def _gelu(x):
    return x * (lax.erf(x * np.float32(1.0 / np.sqrt(2.0))) + 1.0) * 0.5


def _softplus(z):
    return jnp.maximum(z, 0.0) + jnp.log1p(jnp.exp(-jnp.abs(z)))


def _mm_kernel(x_ref, w_ref, *o_refs):
    acc = _dot(x_ref[...].astype(BF16), w_ref[...])
    for o in o_refs:
        o[...] = acc.astype(o.dtype)


def _matmul(x, w, col0, ncols, out_dtypes, name):
    m, k = x.shape
    tm = _tile(m, 512)
    tn = _tile(ncols, 1024)
    assert col0 % tn == 0
    nb0 = col0 // tn
    return pl.pallas_call(
        _mm_kernel,
        grid=(ncols // tn, m // tm),
        in_specs=[pl.BlockSpec((tm, k), lambda n, i: (i, 0)),
                  pl.BlockSpec((k, tn), lambda n, i: (0, n + nb0))],
        out_specs=[pl.BlockSpec((tm, tn), lambda n, i: (i, n)) for _ in out_dtypes],
        out_shape=[jax.ShapeDtypeStruct((m, ncols), d) for d in out_dtypes],
        compiler_params=_params("arbitrary", "arbitrary"),
        name=name,
    )(x, w)


def _sgu_kernel(hu_ref, hv_ref, w_ref, bs_ref, g_ref, b_ref, a_ref, *v_out, emit_v):
    rows = hu_ref.shape[0]
    r_i = lax.broadcasted_iota(jnp.int32, (rows, rows), 0)
    c_i = lax.broadcasted_iota(jnp.int32, (rows, rows), 1)
    causal = c_i <= r_i
    for h in range(A_HEADS):
        sl = slice(h * A_DIM, (h + 1) * A_DIM)
        u = _gelu(hu_ref[:, sl])
        v = _layer_norm(_gelu(hv_ref[:, sl]), g_ref[h:h + 1, :], b_ref[h:h + 1, :])
        w = jnp.where(causal, w_ref[h], 0.0).astype(BF16)
        mix = _dot(w, v.astype(BF16)) + bs_ref[:, h:h + 1]
        a_ref[:, sl] = (u * mix).astype(a_ref.dtype)
        if emit_v:
            v_out[0][:, sl] = v


def _sgu(hu, hv, w_s, bs_t, g, b, rows, row0, n_chunks, emit_v):
    off = row0 // rows
    in_spec = pl.BlockSpec((rows, A_WIDTH), lambda c: (c + off, 0))
    out_spec = pl.BlockSpec((rows, A_WIDTH), lambda c: (c, 0))
    full = lambda a: pl.BlockSpec(a.shape, lambda c: (0,) * a.ndim)
    out_specs = [out_spec]
    out_shape = [jax.ShapeDtypeStruct((n_chunks * rows, A_WIDTH), BF16)]
    if emit_v:
        out_specs.append(out_spec)
        out_shape.append(jax.ShapeDtypeStruct((n_chunks * rows, A_WIDTH), F32))
    return pl.pallas_call(
        functools.partial(_sgu_kernel, emit_v=emit_v),
        grid=(n_chunks,),
        in_specs=[in_spec, in_spec, full(w_s), full(bs_t), full(g), full(b)],
        out_specs=out_specs, out_shape=out_shape,
        compiler_params=_params("arbitrary"),
        name="sgu_sample" if emit_v else "sgu_prompt",
    )(hu, hv, w_s, bs_t, g, b)


BAND_TQ = 2 * CHUNK
BAND_NWB = B_WINDOW // BAND_TQ + 1


def _band_prompt_kernel(q_ref, k_ref, v_ref, bias_ref, o_ref):
    t = pl.program_id(2)
    tq = BAND_TQ
    q = q_ref[...]
    shift = int(np.log2(CHUNK))
    q_chunk = lax.shift_right_arithmetic(t * tq + lax.broadcasted_iota(jnp.int32, (tq, 1), 0), shift)
    lane = lax.broadcasted_iota(jnp.int32, (1, tq), 1)
    s_blocks = []
    for j in range(BAND_NWB):
        kb = t - (BAND_NWB - 1) + j
        start = pl.multiple_of(jnp.maximum(kb, 0) * tq, tq)
        s = _dot_nt(q, k_ref[pl.ds(start, tq), :]) * np.float32(B_DIM ** -0.5) + bias_ref[0, :, j * tq:(j + 1) * tq]
        kpos = kb * tq + lane
        dc = q_chunk - lax.shift_right_arithmetic(kpos, shift)
        valid = (kpos >= 0) & (dc >= 0) & (dc <= B_PREV_CHUNKS)
        s_blocks.append(jnp.where(valid, s, NEG))
    m = s_blocks[0].max(axis=-1, keepdims=True)
    for s in s_blocks[1:]:
        m = jnp.maximum(m, s.max(axis=-1, keepdims=True))
    acc = jnp.zeros((tq, B_DIM), F32)
    den = jnp.zeros((tq, 1), F32)
    for j, s in enumerate(s_blocks):
        kb = t - (BAND_NWB - 1) + j
        start = pl.multiple_of(jnp.maximum(kb, 0) * tq, tq)
        p = jnp.exp(s - m)
        den = den + p.sum(axis=-1, keepdims=True)
        acc = acc + _dot(p.astype(BF16), v_ref[pl.ds(start, tq), :])
    o_ref[...] = (acc / den).astype(o_ref.dtype)


def _band_prompt(q, k, v, bias):
    t_all = BATCH * SEQ
    n_t = SEQ // BAND_TQ
    q_spec = pl.BlockSpec((BAND_TQ, B_DIM), lambda b, h, t: (b * n_t + t, h))
    kv_spec = pl.BlockSpec((SEQ, B_DIM), lambda b, h, t: (b, h))
    return pl.pallas_call(
        _band_prompt_kernel,
        grid=(BATCH, B_HEADS, n_t),
        in_specs=[q_spec, kv_spec, kv_spec,
                  pl.BlockSpec((1, BAND_TQ, BAND_NWB * BAND_TQ), lambda b, h, t: (h, 0, 0))],
        out_specs=q_spec,
        out_shape=jax.ShapeDtypeStruct((t_all, B_WIDTH), BF16),
        compiler_params=_params("arbitrary", "arbitrary", "arbitrary"),
        name="band_prompt",
    )(q, k, v, bias)


def _band_sample_kernel(q_ref, kn_ref, vn_ref, kc_ref, vc_ref, bc_ref, bn_ref, mc_ref, mn_ref, o_ref):
    scale = np.float32(B_DIM ** -0.5)
    for h in range(B_HEADS):
        sl = slice(h * B_DIM, (h + 1) * B_DIM)
        q = q_ref[:, sl]
        s_c = _dot_nt(q, kc_ref[:, sl].astype(BF16)) * scale + bc_ref[h]
        s_n = _dot_nt(q, kn_ref[:, sl]) * scale + bn_ref[h]
        s_c = jnp.where(mc_ref[...] > 0, s_c, NEG)
        s_n = jnp.where(mn_ref[...] > 0, s_n, NEG)
        m = jnp.maximum(s_c.max(axis=-1, keepdims=True), s_n.max(axis=-1, keepdims=True))
        p_c = jnp.exp(s_c - m)
        p_n = jnp.exp(s_n - m)
        den = p_c.sum(axis=-1, keepdims=True) + p_n.sum(axis=-1, keepdims=True)
        acc = _dot(p_c.astype(BF16), vc_ref[:, sl].astype(BF16)) + _dot(p_n.astype(BF16), vn_ref[:, sl])
        o_ref[:, sl] = (acc / den).astype(o_ref.dtype)


def _band_sample(q, k, v, cache_k, cache_v, bias_c, bias_n, mask_c, mask_n):
    n = DEC_SEQ
    off = BATCH * SEQ // n
    n_cache = cache_k.shape[1]
    row_spec = pl.BlockSpec((n, B_WIDTH), lambda b: (b + off, 0))
    cache_spec = pl.BlockSpec((None, n_cache, B_WIDTH), lambda b: (b, 0, 0))
    full = lambda a: pl.BlockSpec(a.shape, lambda b: (0,) * a.ndim)
    return pl.pallas_call(
        _band_sample_kernel,
        grid=(DEC_BATCH,),
        in_specs=[row_spec, row_spec, row_spec, cache_spec, cache_spec,
                  full(bias_c), full(bias_n), full(mask_c), full(mask_n)],
        out_specs=pl.BlockSpec((n, B_WIDTH), lambda b: (b, 0)),
        out_shape=jax.ShapeDtypeStruct((DEC_BATCH * n, B_WIDTH), BF16),
        compiler_params=_params("arbitrary"),
        name="band_sample",
    )(q, k, v, cache_k, cache_v, bias_c, bias_n, mask_c, mask_n)


SB_TQ = 256
SB_TK = 256
SB_SAMPLE_TK = 512


def _sb_weights(z, carry, upper):
    sp = _softplus(z)
    hi = sp.astype(BF16)
    lo = (sp - hi.astype(F32)).astype(BF16)
    later = _dot(hi, upper) + _dot(lo, upper)
    w = jnp.exp((z - sp) - (carry + later))
    return w, carry + later[:, 0:1] + sp[:, 0:1]


def _upper(n):
    r_i = lax.broadcasted_iota(jnp.int32, (n, n), 0)
    c_i = lax.broadcasted_iota(jnp.int32, (n, n), 1)
    return (r_i > c_i).astype(BF16)


def _sb_prompt_kernel(q_ref, k_ref, v_ref, o_ref):
    i = pl.program_id(2)
    q = q_ref[...]
    upper = _upper(SB_TK)
    scale = np.float32(C_DIM ** -0.5)
    r_i = lax.broadcasted_iota(jnp.int32, (SB_TQ, SB_TK), 0)
    c_i = lax.broadcasted_iota(jnp.int32, (SB_TQ, SB_TK), 1)

    def block(j, carry, acc, diagonal):
        start = pl.multiple_of(j * SB_TK, SB_TK)
        z = _dot_nt(q, k_ref[pl.ds(start, SB_TK), :]) * scale
        if diagonal:
            z = jnp.where(c_i < r_i, z, NEG)
        w, carry = _sb_weights(z, carry, upper)
        return carry, acc + _dot(w.astype(BF16), v_ref[pl.ds(start, SB_TK), :])

    carry, acc = block(i, jnp.zeros((SB_TQ, 1), F32), jnp.zeros((SB_TQ, C_DIM), F32), True)

    def body(jj, state):
        return block(i - 1 - jj, state[0], state[1], False)

    carry, acc = lax.fori_loop(0, i, body, (carry, acc))
    o_ref[...] = acc.astype(o_ref.dtype)


def _sb_prompt(q, k, v):
    assert SB_TQ == SB_TK
    t_all = BATCH * SEQ
    n_q = SEQ // SB_TQ
    q_spec = pl.BlockSpec((SB_TQ, C_DIM), lambda b, h, i: (b * n_q + i, h))
    kv_spec = pl.BlockSpec((SEQ, C_DIM), lambda b, h, i: (b, h))
    return pl.pallas_call(
        _sb_prompt_kernel,
        grid=(BATCH, C_HEADS, n_q),
        in_specs=[q_spec, kv_spec, kv_spec],
        out_specs=q_spec,
        out_shape=jax.ShapeDtypeStruct((t_all, C_WIDTH), BF16),
        compiler_params=_params("arbitrary", "arbitrary", "arbitrary"),
        name="sb_prompt",
    )(q, k, v)


def _sb_sample_kernel(q_ref, kn_ref, vn_ref, kc_ref, vc_ref, o_ref, z_ref, w_ref, carry_ref, acc_ref):
    jj = pl.program_id(1)
    n = DEC_SEQ
    tk = kc_ref.shape[0]
    scale = np.float32(C_DIM ** -0.5)

    def all_heads(k_of, v_of, width, mask):
        for h in range(C_HEADS):
            sl = slice(h * C_DIM, (h + 1) * C_DIM)
            z_ref[h * n:(h + 1) * n, 0:width] = _dot_nt(q_ref[:, sl], k_of(sl)) * scale
        z = z_ref[:, 0:width]
        if mask is not None:
            z = jnp.where(mask, z, NEG)
        w, carry = _sb_weights(z, carry_ref[...], _upper(width))
        carry_ref[...] = carry
        w_ref[:, 0:width] = w.astype(BF16)
        for h in range(C_HEADS):
            sl = slice(h * C_DIM, (h + 1) * C_DIM)
            acc_ref[:, sl] += _dot(w_ref[h * n:(h + 1) * n, 0:width], v_of(sl))

    @pl.when(jj == 0)
    def _():
        carry_ref[...] = jnp.zeros_like(carry_ref)
        acc_ref[...] = jnp.zeros_like(acc_ref)
        r_i = lax.broadcasted_iota(jnp.int32, (C_HEADS * n, n), 0) % n
        c_i = lax.broadcasted_iota(jnp.int32, (C_HEADS * n, n), 1)
        all_heads(lambda sl: kn_ref[:, sl], lambda sl: vn_ref[:, sl], n, c_i < r_i)

    all_heads(lambda sl: kc_ref[:, sl].astype(BF16), lambda sl: vc_ref[:, sl].astype(BF16), tk, None)

    @pl.when(jj == pl.num_programs(1) - 1)
    def _():
        o_ref[...] = acc_ref[...].astype(o_ref.dtype)


def _sb_sample(q, k, v, cache_k, cache_v):
    n = DEC_SEQ
    off = BATCH * SEQ // n
    past = cache_k.shape[1]
    tk = _tile(past, SB_SAMPLE_TK)
    n_kb = past // tk
    row_spec = pl.BlockSpec((n, C_WIDTH), lambda b, j: (b + off, 0))
    cache_spec = pl.BlockSpec((None, tk, C_WIDTH), lambda b, j: (b, n_kb - 1 - j, 0))
    return pl.pallas_call(
        _sb_sample_kernel,
        grid=(DEC_BATCH, n_kb),
        in_specs=[row_spec, row_spec, row_spec, cache_spec, cache_spec],
        out_specs=pl.BlockSpec((n, C_WIDTH), lambda b, j: (b, 0)),
        out_shape=jax.ShapeDtypeStruct((DEC_BATCH * n, C_WIDTH), BF16),
        scratch_shapes=[pltpu.VMEM((C_HEADS * n, tk), F32), pltpu.VMEM((C_HEADS * n, tk), BF16),
                        pltpu.VMEM((C_HEADS * n, 1), F32), pltpu.VMEM((n, C_WIDTH), F32)],
        compiler_params=_params("arbitrary", "arbitrary"),
        name="sb_sample",
    )(q, k, v, cache_k, cache_v)


def _pack_bf16_pairs(x):
    half = x.shape[1] // 2
    bits = lax.bitcast_convert_type(x.astype(BF16).astype(F32), jnp.uint32)
    return (bits[:, :half] & jnp.uint32(0xFFFF0000)) | (bits[:, half:] >> 16)


def _unpack_bf16_pairs(w):
    hi = lax.bitcast_convert_type(w & jnp.uint32(0xFFFF0000), F32)
    lo = lax.bitcast_convert_type(w << 16, F32)
    return jnp.concatenate([hi, lo], axis=1).astype(BF16)


def _outproj_kernel(*refs, n_lhs, n_prompt_tiles):
    lhs = refs[:2 * n_lhs]
    w_ref, x_ref, g_ref, b_ref, wr_ref, x1_ref, x1p_ref, lg_ref = refs[2 * n_lhs:]
    is_prompt = pl.program_id(0) < n_prompt_tiles
    k0 = 0
    y = None
    for a_p, a_s in zip(lhs[0::2], lhs[1::2]):
        kk = a_p.shape[1]
        part = _dot(jnp.where(is_prompt, a_p[...], a_s[...]), w_ref[k0:k0 + kk, :])
        y = part if y is None else y + part
        k0 += kk
    x1 = _layer_norm(np.float32(ALPHA) * x_ref[...] + y, g_ref[...], b_ref[...])
    x1_ref[...] = x1
    x1p_ref[...] = _pack_bf16_pairs(x1)
    lg_ref[...] = lax.dot_general(wr_ref[...], x1, (((1,), (1,)), ((), ())), preferred_element_type=F32,
                                  precision=lax.Precision.HIGHEST)


def _outproj_ln(lhs, w, x, g, b, w_router_t, name):
    t = x.shape[0]
    tm = MOE_TM
    n_pt = lhs[0][0].shape[0] // tm
    assert all(a_p.shape[0] == n_pt * tm and a_s.shape[0] == t - n_pt * tm for a_p, a_s in lhs)
    row = lambda a: pl.BlockSpec((tm, a.shape[1]), lambda i: (i, 0))
    full = lambda a: pl.BlockSpec(a.shape, lambda i: (0,) * a.ndim)
    lhs_specs, lhs_args = [], []
    for a_p, a_s in lhs:
        lhs_specs += [pl.BlockSpec((tm, a_p.shape[1]), lambda i: (jnp.minimum(i, n_pt - 1), 0)),
                      pl.BlockSpec((tm, a_s.shape[1]), lambda i: (jnp.maximum(i - n_pt, 0), 0))]
        lhs_args += [a_p, a_s]
    return pl.pallas_call(
        functools.partial(_outproj_kernel, n_lhs=len(lhs), n_prompt_tiles=n_pt),
        grid=(t // tm,),
        in_specs=lhs_specs + [full(w), row(x), full(g), full(b), full(w_router_t)],
        out_specs=[pl.BlockSpec((tm, D_MODEL), lambda i: (i, 0)), pl.BlockSpec((tm, D_MODEL // 2), lambda i: (i, 0)),
                   pl.BlockSpec((N_EXPERTS, tm), lambda i: (0, i))],
        out_shape=[jax.ShapeDtypeStruct((t, D_MODEL), F32), jax.ShapeDtypeStruct((t, D_MODEL // 2), jnp.uint32),
                   jax.ShapeDtypeStruct((N_EXPERTS, t), F32)],
        compiler_params=_params("arbitrary"),
        name=name,
    )(*lhs_args, w, x, g, b, w_router_t)


MOE_TM = 256
EXPERT_BM = 256


def _route_select(lg, bias):
    e, tm = lg.shape
    ninf = np.float32(-np.inf)
    shift = int(np.log2(E_PER_GROUP))
    scores = jax.nn.sigmoid(lg)
    sel = scores + bias
    e_id = lax.broadcasted_iota(jnp.int32, (e, tm), 0)
    g_id = lax.shift_right_logical(e_id, shift)
    g3 = sel.reshape(N_GROUPS, E_PER_GROUP, tm)
    i3 = lax.broadcasted_iota(jnp.int32, g3.shape, 1)
    m1 = g3.max(axis=1, keepdims=True)
    first = jnp.where(g3 == m1, i3, E_PER_GROUP).min(axis=1, keepdims=True)
    m2 = jnp.where(i3 == first, ninf, g3).max(axis=1, keepdims=True)
    grp = jnp.broadcast_to(m1 + m2, g3.shape).reshape(e, tm)
    gsel = jnp.zeros((e, tm), jnp.int32)
    for _ in range(TOPK_GROUPS):
        m = grp.max(axis=0, keepdims=True)
        first = jnp.where(grp == m, e_id, e).min(axis=0, keepdims=True)
        chosen = g_id == lax.shift_right_logical(first, shift)
        gsel = jnp.where(chosen, 1, gsel)
        grp = jnp.where(chosen, ninf, grp)
    sel = jnp.where(gsel > 0, sel, NEG)
    picks = []
    for _ in range(TOP_K):
        m = sel.max(axis=0, keepdims=True)
        first = jnp.where(sel == m, e_id, e).min(axis=0, keepdims=True)
        oh = e_id == first
        picks.append(oh)
        sel = jnp.where(oh, ninf, sel)
    return scores, picks


def _route_kernel(lg_ref, bias_ref, dest_ref, gate_ref, cnt_ref, pad_ref, counts_ref, pads_ref, run_ref):
    phase = pl.program_id(0)
    i = pl.program_id(1)
    e, tm = lg_ref.shape
    scores, picks = _route_select(lg_ref[...], bias_ref[...])
    mask = jnp.zeros((e, tm), F32)
    for oh in picks:
        mask = jnp.where(oh, 1.0, mask)

    @pl.when((phase == 0) & (i == 0))
    def _():
        counts_ref[...] = jnp.zeros_like(counts_ref)

    @pl.when(phase == 0)
    def _():
        counts_ref[...] += mask.sum(axis=1, keepdims=True)

    @pl.when((phase == 1) & (i == 0))
    def _():
        counts = counts_ref[...]
        padded = jnp.ceil(counts * np.float32(1.0 / EXPERT_BM)) * np.float32(EXPERT_BM)
        r_i = lax.broadcasted_iota(jnp.int32, (e, e), 0)
        c_i = lax.broadcasted_iota(jnp.int32, (e, e), 1)
        before = (c_i < r_i).astype(F32)
        starts = jnp.dot(before, jnp.broadcast_to(padded, (e, LANES)), preferred_element_type=F32,
                         precision=lax.Precision.HIGHEST)
        pads_ref[...] = starts[:, 0:1]
        run_ref[...] = jnp.zeros_like(run_ref)
        cnt_ref[...] = jnp.broadcast_to(counts, cnt_ref.shape)
        pad_ref[...] = starts

    @pl.when(phase == 1)
    def _():
        r_i = lax.broadcasted_iota(jnp.int32, (tm, tm), 0)
        c_i = lax.broadcasted_iota(jnp.int32, (tm, tm), 1)
        earlier = (r_i < c_i).astype(BF16)
        slot = pads_ref[...] + run_ref[...] + _dot(mask.astype(BF16), earlier)
        run_ref[...] += mask.sum(axis=1, keepdims=True)
        k_i = lax.broadcasted_iota(jnp.int32, (TOP_K, tm), 0)
        dest = jnp.zeros((TOP_K, tm), F32)
        gate = jnp.zeros((TOP_K, tm), F32)
        total = jnp.zeros((1, tm), F32)
        for k, oh in enumerate(picks):
            d_k = jnp.where(oh, slot, 0.0).sum(axis=0, keepdims=True)
            g_k = jnp.where(oh, scores, 0.0).sum(axis=0, keepdims=True)
            total = total + g_k
            dest = jnp.where(k_i == k, d_k, dest)
            gate = jnp.where(k_i == k, g_k, gate)
        dest_ref[...] = dest.astype(jnp.int32)
        gate_ref[...] = gate / total * np.float32(ROUTED_SCALE)


def _route(lg_t, bias):
    e, t = lg_t.shape
    tm = MOE_TM
    n_t = t // tm
    tile = pl.BlockSpec((TOP_K, tm), lambda p, i: (0, i * p))
    meta = pl.BlockSpec((e, LANES), lambda p, i: (0, 0))
    return pl.pallas_call(
        _route_kernel,
        grid=(2, n_t),
        in_specs=[pl.BlockSpec((e, tm), lambda p, i: (0, i)), pl.BlockSpec((e, 1), lambda p, i: (0, 0))],
        out_specs=[tile, tile, meta, meta],
        out_shape=[jax.ShapeDtypeStruct((TOP_K, t), jnp.int32), jax.ShapeDtypeStruct((TOP_K, t), F32),
                   jax.ShapeDtypeStruct((e, LANES), F32), jax.ShapeDtypeStruct((e, LANES), F32)],
        scratch_shapes=[pltpu.VMEM((e, 1), F32), pltpu.VMEM((e, 1), F32), pltpu.VMEM((e, 1), F32)],
        compiler_params=_params("arbitrary", "arbitrary"),
        name="route",
    )(lg_t, bias)


def _row_copy(src, src_row, dst, dst_row, sem):
    return pltpu.make_async_copy(src.at[pl.ds(src_row, 1)], dst.at[pl.ds(dst_row, 1)], sem)


def _scatter_kernel(dest_ref, x_ref, xs_in, xs_out, sem):
    del xs_in
    tm = x_ref.shape[0]

    def issue(t, c):
        for k in range(TOP_K):
            _row_copy(x_ref, t, xs_out, dest_ref[k * tm + t], sem).start()
        return c

    def drain(t, c):
        for k in range(TOP_K):
            _row_copy(x_ref, t, xs_out, dest_ref[k * tm + t], sem).wait()
        return c

    lax.fori_loop(0, tm, issue, 0)
    lax.fori_loop(0, tm, drain, 0)


def _scatter(dest_flat, x1p, xs_init):
    t, w = x1p.shape
    tm = MOE_TM
    return pl.pallas_call(
        _scatter_kernel,
        grid=(t // tm,),
        in_specs=[pl.BlockSpec((TOP_K * tm,), lambda i: (i,), memory_space=pltpu.SMEM),
                  pl.BlockSpec((tm, w), lambda i: (i, 0)),
                  pl.BlockSpec(memory_space=pl.ANY)],
        out_specs=pl.BlockSpec(memory_space=pl.ANY),
        out_shape=jax.ShapeDtypeStruct(xs_init.shape, xs_init.dtype),
        scratch_shapes=[pltpu.SemaphoreType.DMA(())],
        input_output_aliases={2: 0},
        compiler_params=_params("arbitrary"),
        name="scatter",
    )(dest_flat, x1p, xs_init)


def _expert_kernel(be_ref, nu_ref, x_ref, wg_ref, wu_ref, wd_ref, y_ref):
    i = pl.program_id(0)

    @pl.when(i < nu_ref[0])
    def _():
        x = _unpack_bf16_pairs(x_ref[...])
        g = _dot(x, wg_ref[...])
        h = (g * jax.nn.sigmoid(g)) * _dot(x, wu_ref[...])
        y_ref[...] = _dot(h.astype(BF16), wd_ref[...])

    @pl.when(i >= nu_ref[0])
    def _():
        y_ref[...] = jnp.zeros_like(y_ref)


def _experts(block_e, n_used, xs, wg, wu, wd):
    n_slots = xs.shape[0]
    n_blocks = n_slots // EXPERT_BM
    ff = wg.shape[2]
    grid_spec = pltpu.PrefetchScalarGridSpec(
        num_scalar_prefetch=2,
        grid=(n_blocks,),
        in_specs=[pl.BlockSpec((EXPERT_BM, D_MODEL // 2), lambda i, be, nu: (i, 0)),
                  pl.BlockSpec((None, D_MODEL, ff), lambda i, be, nu: (be[i], 0, 0)),
                  pl.BlockSpec((None, D_MODEL, ff), lambda i, be, nu: (be[i], 0, 0)),
                  pl.BlockSpec((None, ff, D_MODEL), lambda i, be, nu: (be[i], 0, 0))],
        out_specs=pl.BlockSpec((EXPERT_BM, D_MODEL), lambda i, be, nu: (i, 0)),
    )
    return pl.pallas_call(
        _expert_kernel,
        grid_spec=grid_spec,
        out_shape=jax.ShapeDtypeStruct((n_slots, D_MODEL), F32),
        compiler_params=_params("arbitrary"),
        name="experts",
    )(block_e, n_used, xs, wg, wu, wd)


def _ffn_ln_kernel(dest_ref, xp_ref, x_ref, gate_ref, y_hbm, wg_ref, wu_ref, wd_ref, g_ref, b_ref, o_ref, buf, sem):
    tm = x_ref.shape[0]

    def issue(t, c):
        for k in range(TOP_K):
            _row_copy(y_hbm, dest_ref[k * tm + t], buf.at[k], t, sem).start()
        return c

    def drain(t, c):
        for k in range(TOP_K):
            _row_copy(y_hbm, dest_ref[k * tm + t], buf.at[k], t, sem).wait()
        return c

    lax.fori_loop(0, tm, issue, 0)
    xb = _unpack_bf16_pairs(xp_ref[...])
    gte = _dot(xb, wg_ref[...])
    h = (gte * jax.nn.sigmoid(gte)) * _dot(xb, wu_ref[...])
    shared = _dot(h.astype(BF16), wd_ref[...])
    lax.fori_loop(0, tm, drain, 0)
    routed = None
    for k in range(TOP_K):
        part = buf[k] * gate_ref[:, k:k + 1]
        routed = part if routed is None else routed + part
    z = np.float32(ALPHA) * x_ref[...] + (routed + shared)
    o_ref[...] = _layer_norm(z, g_ref[...], b_ref[...])


def _ffn_ln(dest_flat, x1p, x1, gate_t, y, wg, wu, wd, g, b):
    t = x1.shape[0]
    tm = MOE_TM
    row = lambda a: pl.BlockSpec((tm, a.shape[1]), lambda i: (i, 0))
    full = lambda a: pl.BlockSpec(a.shape, lambda i: (0,) * a.ndim)
    return pl.pallas_call(
        _ffn_ln_kernel,
        grid=(t // tm,),
        in_specs=[pl.BlockSpec((TOP_K * tm,), lambda i: (i,), memory_space=pltpu.SMEM),
                  row(x1p), row(x1), row(gate_t), pl.BlockSpec(memory_space=pl.ANY),
                  full(wg), full(wu), full(wd), full(g), full(b)],
        out_specs=row(x1),
        out_shape=jax.ShapeDtypeStruct((t, D_MODEL), F32),
        scratch_shapes=[pltpu.VMEM((TOP_K, tm, D_MODEL), F32), pltpu.SemaphoreType.DMA(())],
        compiler_params=_params("arbitrary"),
        name="ffn_ln",
    )(dest_flat, x1p, x1, gate_t, y, wg, wu, wd, g, b)


def _ple_kernel(xf_ref, xs_ref, p_ref, wp_ref, wg_ref, o_ref):
    gate = jax.nn.sigmoid(_dot(xf_ref[...].astype(BF16), wg_ref[...]))
    emb = _dot(p_ref[...].astype(BF16), wp_ref[...])
    o_ref[...] = xs_ref[...] + emb * gate


def _ple(x2, p, w_ple, w_gate):
    t = x2.shape[0]
    tm = _tile(t, 512)
    tn = _tile(D_MODEL, 1024)
    return pl.pallas_call(
        _ple_kernel,
        grid=(D_MODEL // tn, t // tm),
        in_specs=[pl.BlockSpec((tm, D_MODEL), lambda n, i: (i, 0)),
                  pl.BlockSpec((tm, tn), lambda n, i: (i, n)),
                  pl.BlockSpec((tm, p.shape[1]), lambda n, i: (i, 0)),
                  pl.BlockSpec((w_ple.shape[0], tn), lambda n, i: (0, n)),
                  pl.BlockSpec((D_MODEL, tn), lambda n, i: (0, n))],
        out_specs=pl.BlockSpec((tm, tn), lambda n, i: (i, n)),
        out_shape=jax.ShapeDtypeStruct((t, D_MODEL), F32),
        compiler_params=_params("arbitrary", "arbitrary"),
        name="ple",
    )(x2, x2, p, w_ple, w_gate)


def _rel_bias(table, qpos, kpos):
    idx = np.clip(qpos[:, None] - kpos[None, :], -REL_CLIP, REL_CLIP) + REL_CLIP
    return table[:, idx]


def _band_mask(qpos, kpos):
    dc = qpos // CHUNK - kpos // CHUNK
    return (kpos >= 0) & (dc >= 0) & (dc <= B_PREV_CHUNKS)


def _post_block(x, lhs, w_out, p, ln1_g, ln1_b, ln2_g, ln2_b, w_router, b_router,
                w_gate, w_up, w_down, ws_gate, ws_up, ws_down, w_ple, w_ple_gate, name):
    row = lambda a: a.reshape(1, -1)
    t = x.shape[0]
    x1, x1p, lg_t = _outproj_ln(lhs, w_out, x, row(ln1_g), row(ln1_b), w_router.T, name)
    dest, gate, cnt, pad = _route(lg_t, b_router.astype(F32).reshape(N_EXPERTS, 1))

    n_blocks = -(-(t * TOP_K + N_EXPERTS * (EXPERT_BM - 1)) // EXPERT_BM)
    counts = cnt[:, 0].astype(jnp.int32)
    pad_end = pad[:, 0].astype(jnp.int32) + (counts + EXPERT_BM - 1) // EXPERT_BM * EXPERT_BM
    block_e = jnp.minimum((pad_end[:, None] <= jnp.arange(n_blocks)[None, :] * EXPERT_BM).sum(axis=0),
                          N_EXPERTS - 1).astype(jnp.int32)
    n_used = (pad_end[-1:] // EXPERT_BM).astype(jnp.int32)
    dest_flat = dest.reshape(TOP_K, t // MOE_TM, MOE_TM).transpose(1, 0, 2).reshape(-1)

    xs = _scatter(dest_flat, x1p, jnp.zeros((n_blocks * EXPERT_BM, D_MODEL // 2), jnp.uint32))
    y = _experts(block_e, n_used, xs, w_gate, w_up, w_down)
    x2 = _ffn_ln(dest_flat, x1p, x1, gate.T, y, ws_gate, ws_up, ws_down, row(ln2_g), row(ln2_b))
    return _ple(x2, p, w_ple, w_ple_gate)


def kernel(x_prompt, x_sample, cache_b_k, cache_b_v, cache_c_k, cache_c_v, p_prompt, p_sample, w_in_ab, w_out_ab, sgu_w, sgu_b, sgu_ln_g, sgu_ln_b, rel_bias_tab, w_in_c, w_out_c, ln_mix_g, ln_mix_b, ln_ffn_g, ln_ffn_b, w_router, b_router, w_gate, w_up, w_down, ws_gate, ws_up, ws_down, w_ple, w_ple_gate):
    n_p = BATCH * SEQ
    n_s = DEC_BATCH * DEC_SEQ
    x = jnp.concatenate([x_prompt.reshape(n_p, D_MODEL), x_sample.reshape(n_s, D_MODEL)], axis=0)
    p_all = jnp.concatenate([p_prompt.reshape(DEPTH, n_p, PLE_DIM), p_sample.reshape(DEPTH, n_s, PLE_DIM)], axis=1)
    bf = lambda a: a.astype(BF16)
    outs = {k: [] for k in ("bk_p", "bv_p", "bk_s", "bv_s", "av_s", "ck_p", "cv_p", "ck_s", "cv_s")}

    for i in range(DEPTH):
        j = i // 2
        if i % 2 == 0:
            w_in = bf(w_in_ab[j])
            hu, = _matmul(x, w_in, 0, A_WIDTH, (F32,), "proj_u")
            hv, = _matmul(x, w_in, A_WIDTH, A_WIDTH, (F32,), "proj_v")
            q, = _matmul(x, w_in, 2 * A_WIDTH, B_WIDTH, (BF16,), "proj_q")
            k, kb16 = _matmul(x, w_in, 2 * A_WIDTH + B_WIDTH, B_WIDTH, (F32, BF16), "proj_k")
            v, vb16 = _matmul(x, w_in, 2 * A_WIDTH + 2 * B_WIDTH, B_WIDTH, (F32, BF16), "proj_vv")

            bs_t = sgu_b[j].T
            a_p, = _sgu(hu, hv, sgu_w[j], bs_t, sgu_ln_g[j], sgu_ln_b[j], A_CHUNK, 0, n_p // A_CHUNK, False)
            a_s, va_s = _sgu(hu, hv, sgu_w[j][:, :DEC_SEQ, :DEC_SEQ], bs_t[:DEC_SEQ], sgu_ln_g[j], sgu_ln_b[j],
                             DEC_SEQ, n_p, DEC_BATCH, True)

            qp = np.arange(BAND_TQ)
            kp = np.arange(BAND_NWB * BAND_TQ) - (BAND_NWB - 1) * BAND_TQ
            bias_p = _rel_bias(rel_bias_tab[j], qp, kp)
            b_p = _band_prompt(q, kb16, vb16, bias_p)
            n_cache = cache_b_k.shape[2]
            qs = PAST_LEN + np.arange(DEC_SEQ)
            kc = PAST_LEN - n_cache + np.arange(n_cache)
            b_s = _band_sample(
                q, kb16, vb16,
                cache_b_k[j].reshape(DEC_BATCH, n_cache, B_WIDTH), cache_b_v[j].reshape(DEC_BATCH, n_cache, B_WIDTH),
                _rel_bias(rel_bias_tab[j], qs, kc), _rel_bias(rel_bias_tab[j], qs, qs),
                jnp.asarray(_band_mask(qs[:, None], kc[None, :]), F32),
                jnp.asarray(_band_mask(qs[:, None], qs[None, :]), F32))

            keep = min(B_WINDOW, SEQ)
            kp4 = k[:n_p].reshape(BATCH, SEQ, B_HEADS, B_DIM)
            vp4 = v[:n_p].reshape(BATCH, SEQ, B_HEADS, B_DIM)
            outs["bk_p"].append(kp4[:, SEQ - keep:])
            outs["bv_p"].append(vp4[:, SEQ - keep:])
            outs["bk_s"].append(k[n_p:].reshape(DEC_BATCH, DEC_SEQ, B_HEADS, B_DIM))
            outs["bv_s"].append(v[n_p:].reshape(DEC_BATCH, DEC_SEQ, B_HEADS, B_DIM))
            outs["av_s"].append(va_s.reshape(DEC_BATCH, DEC_SEQ, A_HEADS, A_DIM))
            lhs, w_out, name = [(a_p, a_s), (b_p, b_s)], bf(w_out_ab[j]), "outproj_ab"
        else:
            w_in = bf(w_in_c[j])
            q, = _matmul(x, w_in, 0, C_WIDTH, (BF16,), "proj_cq")
            k, kb16 = _matmul(x, w_in, C_WIDTH, C_WIDTH, (F32, BF16), "proj_ck")
            v, vb16 = _matmul(x, w_in, 2 * C_WIDTH, C_WIDTH, (F32, BF16), "proj_cv")
            o_p = _sb_prompt(q, kb16, vb16)
            o_s = _sb_sample(q, kb16, vb16, cache_c_k[j].reshape(DEC_BATCH, PAST_LEN, C_WIDTH),
                             cache_c_v[j].reshape(DEC_BATCH, PAST_LEN, C_WIDTH))
            outs["ck_p"].append(k[:n_p].reshape(BATCH, SEQ, C_HEADS, C_DIM))
            outs["cv_p"].append(v[:n_p].reshape(BATCH, SEQ, C_HEADS, C_DIM))
            outs["ck_s"].append(k[n_p:].reshape(DEC_BATCH, DEC_SEQ, C_HEADS, C_DIM))
            outs["cv_s"].append(v[n_p:].reshape(DEC_BATCH, DEC_SEQ, C_HEADS, C_DIM))
            lhs, w_out, name = [(o_p, o_s)], bf(w_out_c[j]), "outproj_c"

        x = _post_block(x, lhs, w_out, p_all[i], ln_mix_g[i], ln_mix_b[i], ln_ffn_g[i], ln_ffn_b[i],
                        w_router[i], b_router[i], bf(w_gate[i]), bf(w_up[i]), bf(w_down[i]),
                        bf(ws_gate[i]), bf(ws_up[i]), bf(ws_down[i]), bf(w_ple[i]), bf(w_ple_gate[i]), name)

    st = lambda key: jnp.stack(outs[key])
    return (x[:n_p].reshape(BATCH, SEQ, D_MODEL), x[n_p:].reshape(DEC_BATCH, DEC_SEQ, D_MODEL),
            st("bk_p"), st("bv_p"), st("bk_s"), st("bv_s"), st("av_s"),
            st("ck_p"), st("cv_p"), st("ck_s"), st("cv_s"))
```

```python
import functools

import numpy as np
import jax
import jax.numpy as jnp
from jax import lax
from jax.experimental import pallas as pl
from jax.experimental.pallas import tpu as pltpu

D_MODEL = 2048
BATCH = 2
SEQ = 4096
DEPTH = 2
DEC_BATCH = 32
DEC_SEQ = 16
PAST_LEN = 2048

CHUNK = 64
A_CHUNK = 128
A_HEADS = 8
A_DIM = 128
A_WIDTH = A_HEADS * A_DIM
B_HEADS = 8
B_DIM = 128
B_WIDTH = B_HEADS * B_DIM
B_PREV_CHUNKS = 8
B_WINDOW = B_PREV_CHUNKS * CHUNK
REL_CLIP = 128
C_HEADS = 16
C_DIM = 128
C_WIDTH = C_HEADS * C_DIM
N_EXPERTS = 64
N_GROUPS = 8
E_PER_GROUP = N_EXPERTS // N_GROUPS
TOPK_GROUPS = 4
TOP_K = 8
EXPERT_FF = 512
SHARED_FF = 512
ROUTED_SCALE = 2.5
PLE_DIM = 256
LN_EPS = 1e-5
ALPHA = (2 * DEPTH) ** 0.25
NEG = -1e9

LANES = 128
VMEM_LIMIT_BYTES = 56 * 1024 * 1024

BF16 = jnp.bfloat16
F32 = jnp.float32


def _params(*sem):
    return pltpu.CompilerParams(dimension_semantics=sem, vmem_limit_bytes=VMEM_LIMIT_BYTES)


def _tile(n, pref):
    if n <= pref:
        return n
    for t in range(pref, 7, -1):
        if n % t == 0 and t % 8 == 0:
            return t
    return n


def _dot(a, b):
    return jnp.dot(a, b, preferred_element_type=F32)


def _dot_nt(a, b):
    return lax.dot_general(a, b, (((1,), (1,)), ((), ())), preferred_element_type=F32)


def _layer_norm(z, g, b):
    mu = jnp.mean(z, axis=-1, keepdims=True)
    zc = z - mu
    var = jnp.mean(zc * zc, axis=-1, keepdims=True)
    return zc * lax.rsqrt(var + LN_EPS) * g + b


def _gelu(x):
    return x * (lax.erf(x * np.float32(1.0 / np.sqrt(2.0))) + 1.0) * 0.5


def _softplus(z):
    return jnp.maximum(z, 0.0) + jnp.log(1.0 + jnp.exp(-jnp.abs(z)))


def _mm_kernel(x_ref, w_ref, *rest, split, n_prompt_tiles):
    o_refs, wb_ref = rest[:-1], rest[-1]
    i = pl.program_id(1)

    @pl.when(i == 0)
    def _():
        wb_ref[...] = w_ref[...].astype(BF16)

    acc = _dot(x_ref[...].astype(BF16), wb_ref[...])
    refs = iter(o_refs)
    for is_split in split:
        if is_split:
            o_p, o_s = next(refs), next(refs)

            @pl.when(i < n_prompt_tiles)
            def _():
                o_p[...] = acc.astype(o_p.dtype)

            @pl.when(i >= n_prompt_tiles)
            def _():
                o_s[...] = acc.astype(o_s.dtype)
        else:
            o = next(refs)
            o[...] = acc.astype(o.dtype)


def _matmul(x, w, col0, ncols, outs, name, n_prompt=None):
    m, k = x.shape
    n_prompt = m if n_prompt is None else n_prompt
    tm = _tile(np.gcd(n_prompt, m - n_prompt) if n_prompt < m else m, 512)
    tn = _tile(ncols, 1024)
    assert col0 % tn == 0 and m % tm == 0 and n_prompt % tm == 0
    nb0 = col0 // tn
    n_pt = n_prompt // tm
    out_specs, out_shape = [], []
    for dtype, is_split in outs:
        if is_split:
            out_specs += [pl.BlockSpec((tm, tn), lambda n, i: (jnp.minimum(i, n_pt - 1), n)),
                          pl.BlockSpec((tm, tn), lambda n, i: (jnp.maximum(i - n_pt, 0), n))]
            out_shape += [jax.ShapeDtypeStruct((n_prompt, ncols), dtype),
                          jax.ShapeDtypeStruct((m - n_prompt, ncols), dtype)]
        else:
            out_specs.append(pl.BlockSpec((tm, tn), lambda n, i: (i, n)))
            out_shape.append(jax.ShapeDtypeStruct((m, ncols), dtype))
    return pl.pallas_call(
        functools.partial(_mm_kernel, split=tuple(s for _, s in outs), n_prompt_tiles=n_pt),
        grid=(ncols // tn, m // tm),
        in_specs=[pl.BlockSpec((tm, k), lambda n, i: (i, 0)),
                  pl.BlockSpec((k, tn), lambda n, i: (0, n + nb0))],
        out_specs=out_specs, out_shape=out_shape,
        scratch_shapes=[pltpu.VMEM((k, tn), BF16)],
        compiler_params=_params("arbitrary", "arbitrary"),
        name=name,
    )(x, w)


def _sgu_kernel(hu_ref, hv_ref, w_ref, bs_ref, g_ref, b_ref, a_ref, *v_out, emit_v):
    rows = hu_ref.shape[0]
    r_i = lax.broadcasted_iota(jnp.int32, (rows, rows), 0)
    c_i = lax.broadcasted_iota(jnp.int32, (rows, rows), 1)
    causal = c_i <= r_i
    for h in range(A_HEADS):
        sl = slice(h * A_DIM, (h + 1) * A_DIM)
        u = _gelu(hu_ref[:, sl])
        v = _layer_norm(_gelu(hv_ref[:, sl]), g_ref[h:h + 1, :], b_ref[h:h + 1, :])
        w = jnp.where(causal, w_ref[h], 0.0).astype(BF16)
        mix = _dot(w, v.astype(BF16)) + bs_ref[:, h:h + 1]
        a_ref[:, sl] = (u * mix).astype(a_ref.dtype)
        if emit_v:
            v_out[0][:, sl] = v


def _sgu(hu, hv, w_s, bs_t, g, b, rows, row0, n_chunks, emit_v):
    off = row0 // rows
    in_spec = pl.BlockSpec((rows, A_WIDTH), lambda c: (c + off, 0))
    out_spec = pl.BlockSpec((rows, A_WIDTH), lambda c: (c, 0))
    full = lambda a: pl.BlockSpec(a.shape, lambda c: (0,) * a.ndim)
    out_specs = [out_spec]
    out_shape = [jax.ShapeDtypeStruct((n_chunks * rows, A_WIDTH), BF16)]
    if emit_v:
        out_specs.append(out_spec)
        out_shape.append(jax.ShapeDtypeStruct((n_chunks * rows, A_WIDTH), F32))
    return pl.pallas_call(
        functools.partial(_sgu_kernel, emit_v=emit_v),
        grid=(n_chunks,),
        in_specs=[in_spec, in_spec, full(w_s), full(bs_t), full(g), full(b)],
        out_specs=out_specs, out_shape=out_shape,
        compiler_params=_params("arbitrary"),
        name="sgu_sample" if emit_v else "sgu_prompt",
    )(hu, hv, w_s, bs_t, g, b)


BAND_TQ = 2 * CHUNK
BAND_NWB = B_WINDOW // BAND_TQ + 1


def _band_prompt_kernel(q_ref, k_ref, v_ref, bias_ref, o_ref):
    t = pl.program_id(2)
    tq = BAND_TQ
    q = q_ref[...]
    shift = int(np.log2(CHUNK))
    q_chunk = lax.shift_right_arithmetic(t * tq + lax.broadcasted_iota(jnp.int32, (tq, 1), 0), shift)
    lane = lax.broadcasted_iota(jnp.int32, (1, tq), 1)
    s_blocks = []
    for j in range(BAND_NWB):
        kb = t - (BAND_NWB - 1) + j
        start = pl.multiple_of(jnp.maximum(kb, 0) * tq, tq)
        s = _dot_nt(q, k_ref[pl.ds(start, tq), :]) * np.float32(B_DIM ** -0.5) + bias_ref[0, :, j * tq:(j + 1) * tq]
        kpos = kb * tq + lane
        dc = q_chunk - lax.shift_right_arithmetic(kpos, shift)
        valid = (kpos >= 0) & (dc >= 0) & (dc <= B_PREV_CHUNKS)
        s_blocks.append(jnp.where(valid, s, NEG))
    m = s_blocks[0].max(axis=-1, keepdims=True)
    for s in s_blocks[1:]:
        m = jnp.maximum(m, s.max(axis=-1, keepdims=True))
    acc = jnp.zeros((tq, B_DIM), F32)
    den = jnp.zeros((tq, 1), F32)
    for j, s in enumerate(s_blocks):
        kb = t - (BAND_NWB - 1) + j
        start = pl.multiple_of(jnp.maximum(kb, 0) * tq, tq)
        p = jnp.exp(s - m)
        den = den + p.sum(axis=-1, keepdims=True)
        acc = acc + _dot(p.astype(BF16), v_ref[pl.ds(start, tq), :])
    o_ref[...] = (acc / den).astype(o_ref.dtype)


def _band_prompt(q, k, v, bias):
    t_all = BATCH * SEQ
    n_t = SEQ // BAND_TQ
    q_spec = pl.BlockSpec((BAND_TQ, B_DIM), lambda b, h, t: (b * n_t + t, h))
    kv_spec = pl.BlockSpec((SEQ, B_DIM), lambda b, h, t: (b, h))
    return pl.pallas_call(
        _band_prompt_kernel,
        grid=(BATCH, B_HEADS, n_t),
        in_specs=[q_spec, kv_spec, kv_spec,
                  pl.BlockSpec((1, BAND_TQ, BAND_NWB * BAND_TQ), lambda b, h, t: (h, 0, 0))],
        out_specs=q_spec,
        out_shape=jax.ShapeDtypeStruct((t_all, B_WIDTH), BF16),
        compiler_params=_params("arbitrary", "arbitrary", "arbitrary"),
        name="band_prompt",
    )(q, k, v, bias)


def _band_sample_kernel(q_ref, kn_ref, vn_ref, kc_ref, vc_ref, bc_ref, bn_ref, mc_ref, mn_ref, o_ref):
    scale = np.float32(B_DIM ** -0.5)
    for h in range(B_HEADS):
        sl = slice(h * B_DIM, (h + 1) * B_DIM)
        q = q_ref[:, sl]
        s_c = _dot_nt(q, kc_ref[:, sl].astype(BF16)) * scale + bc_ref[h]
        s_n = _dot_nt(q, kn_ref[:, sl]) * scale + bn_ref[h]
        s_c = jnp.where(mc_ref[...] > 0, s_c, NEG)
        s_n = jnp.where(mn_ref[...] > 0, s_n, NEG)
        m = jnp.maximum(s_c.max(axis=-1, keepdims=True), s_n.max(axis=-1, keepdims=True))
        p_c = jnp.exp(s_c - m)
        p_n = jnp.exp(s_n - m)
        den = p_c.sum(axis=-1, keepdims=True) + p_n.sum(axis=-1, keepdims=True)
        acc = _dot(p_c.astype(BF16), vc_ref[:, sl].astype(BF16)) + _dot(p_n.astype(BF16), vn_ref[:, sl])
        o_ref[:, sl] = (acc / den).astype(o_ref.dtype)


def _band_sample(q, k, v, cache_k, cache_v, bias_c, bias_n, mask_c, mask_n):
    n = DEC_SEQ
    off = BATCH * SEQ // n
    n_cache = cache_k.shape[1]
    row_spec = pl.BlockSpec((n, B_WIDTH), lambda b: (b + off, 0))
    cache_spec = pl.BlockSpec((None, n_cache, B_WIDTH), lambda b: (b, 0, 0))
    full = lambda a: pl.BlockSpec(a.shape, lambda b: (0,) * a.ndim)
    return pl.pallas_call(
        _band_sample_kernel,
        grid=(DEC_BATCH,),
        in_specs=[row_spec, row_spec, row_spec, cache_spec, cache_spec,
                  full(bias_c), full(bias_n), full(mask_c), full(mask_n)],
        out_specs=pl.BlockSpec((n, B_WIDTH), lambda b: (b, 0)),
        out_shape=jax.ShapeDtypeStruct((DEC_BATCH * n, B_WIDTH), BF16),
        compiler_params=_params("arbitrary"),
        name="band_sample",
    )(q, k, v, cache_k, cache_v, bias_c, bias_n, mask_c, mask_n)


SB_TQ = 256
SB_TK = 256
SB_SAMPLE_TK = 512
SB_HP = 2


def _sb_weights(z, carry, upper):
    sp = _softplus(z)
    hi = sp.astype(BF16)
    lo = (sp - hi.astype(F32)).astype(BF16)
    later = _dot(hi, upper) + _dot(lo, upper)
    w = jnp.exp((z - sp) - (carry + later))
    return w, carry + later[:, 0:1] + sp[:, 0:1]


def _upper(n):
    r_i = lax.broadcasted_iota(jnp.int32, (n, n), 0)
    c_i = lax.broadcasted_iota(jnp.int32, (n, n), 1)
    return (r_i > c_i).astype(BF16)


def _sb_prompt_kernel(q_ref, k_ref, v_ref, o_ref):
    i = pl.program_id(2)
    upper = _upper(SB_TK)
    scale = np.float32(C_DIM ** -0.5)
    r_i = lax.broadcasted_iota(jnp.int32, (SB_TQ, SB_TK), 0)
    c_i = lax.broadcasted_iota(jnp.int32, (SB_TQ, SB_TK), 1)
    heads = [slice(h * C_DIM, (h + 1) * C_DIM) for h in range(SB_HP)]
    qs = [q_ref[:, sl] for sl in heads]

    def block(j, state, diagonal):
        start = pl.multiple_of(j * SB_TK, SB_TK)
        out = []
        for sl, q, (carry, acc) in zip(heads, qs, state):
            z = _dot_nt(q, k_ref[pl.ds(start, SB_TK), sl]) * scale
            if diagonal:
                z = jnp.where(c_i < r_i, z, NEG)
            w, carry = _sb_weights(z, carry, upper)
            out.append((carry, acc + _dot(w.astype(BF16), v_ref[pl.ds(start, SB_TK), sl])))
        return tuple(out)

    zero = (jnp.zeros((SB_TQ, 1), F32), jnp.zeros((SB_TQ, C_DIM), F32))
    state = block(i, (zero,) * SB_HP, True)
    state = lax.fori_loop(0, i, lambda jj, st: block(i - 1 - jj, st, False), state)
    for sl, (_, acc) in zip(heads, state):
        o_ref[:, sl] = acc.astype(o_ref.dtype)


def _sb_prompt(q, k, v):
    assert SB_TQ == SB_TK and C_HEADS % SB_HP == 0
    t_all = BATCH * SEQ
    n_q = SEQ // SB_TQ
    q_spec = pl.BlockSpec((SB_TQ, SB_HP * C_DIM), lambda b, h, i: (b * n_q + i, h))
    kv_spec = pl.BlockSpec((SEQ, SB_HP * C_DIM), lambda b, h, i: (b, h))
    return pl.pallas_call(
        _sb_prompt_kernel,
        grid=(BATCH, C_HEADS // SB_HP, n_q),
        in_specs=[q_spec, kv_spec, kv_spec],
        out_specs=q_spec,
        out_shape=jax.ShapeDtypeStruct((t_all, C_WIDTH), BF16),
        compiler_params=_params("arbitrary", "arbitrary", "arbitrary"),
        name="sb_prompt",
    )(q, k, v)


def _sb_sample_kernel(q_ref, kn_ref, vn_ref, kc_ref, vc_ref, o_ref, z_ref, w_ref, carry_ref, acc_ref):
    jj = pl.program_id(1)
    n = DEC_SEQ
    tk = kc_ref.shape[0]
    scale = np.float32(C_DIM ** -0.5)

    def all_heads(k_of, v_of, width, mask):
        for h in range(C_HEADS):
            sl = slice(h * C_DIM, (h + 1) * C_DIM)
            z_ref[h * n:(h + 1) * n, 0:width] = _dot_nt(q_ref[:, sl], k_of(sl)) * scale
        z = z_ref[:, 0:width]
        if mask is not None:
            z = jnp.where(mask, z, NEG)
        w, carry = _sb_weights(z, carry_ref[...], _upper(width))
        carry_ref[...] = carry
        w_ref[:, 0:width] = w.astype(BF16)
        for h in range(C_HEADS):
            sl = slice(h * C_DIM, (h + 1) * C_DIM)
            acc_ref[:, sl] += _dot(w_ref[h * n:(h + 1) * n, 0:width], v_of(sl))

    @pl.when(jj == 0)
    def _():
        carry_ref[...] = jnp.zeros_like(carry_ref)
        acc_ref[...] = jnp.zeros_like(acc_ref)
        r_i = lax.broadcasted_iota(jnp.int32, (C_HEADS * n, n), 0) % n
        c_i = lax.broadcasted_iota(jnp.int32, (C_HEADS * n, n), 1)
        all_heads(lambda sl: kn_ref[:, sl], lambda sl: vn_ref[:, sl], n, c_i < r_i)

    all_heads(lambda sl: kc_ref[:, sl].astype(BF16), lambda sl: vc_ref[:, sl].astype(BF16), tk, None)

    @pl.when(jj == pl.num_programs(1) - 1)
    def _():
        o_ref[...] = acc_ref[...].astype(o_ref.dtype)


def _sb_sample(q, k, v, cache_k, cache_v):
    n = DEC_SEQ
    off = BATCH * SEQ // n
    past = cache_k.shape[1]
    tk = _tile(past, SB_SAMPLE_TK)
    n_kb = past // tk
    row_spec = pl.BlockSpec((n, C_WIDTH), lambda b, j: (b + off, 0))
    cache_spec = pl.BlockSpec((None, tk, C_WIDTH), lambda b, j: (b, n_kb - 1 - j, 0))
    return pl.pallas_call(
        _sb_sample_kernel,
        grid=(DEC_BATCH, n_kb),
        in_specs=[row_spec, row_spec, row_spec, cache_spec, cache_spec],
        out_specs=pl.BlockSpec((n, C_WIDTH), lambda b, j: (b, 0)),
        out_shape=jax.ShapeDtypeStruct((DEC_BATCH * n, C_WIDTH), BF16),
        scratch_shapes=[pltpu.VMEM((C_HEADS * n, tk), F32), pltpu.VMEM((C_HEADS * n, tk), BF16),
                        pltpu.VMEM((C_HEADS * n, 1), F32), pltpu.VMEM((n, C_WIDTH), F32)],
        compiler_params=_params("arbitrary", "arbitrary"),
        name="sb_sample",
    )(q, k, v, cache_k, cache_v)


def _pack_bf16_pairs(x):
    half = x.shape[1] // 2
    bits = lax.bitcast_convert_type(x.astype(BF16).astype(F32), jnp.uint32)
    return (bits[:, :half] & jnp.uint32(0xFFFF0000)) | (bits[:, half:] >> 16)


def _unpack_bf16_pairs(w):
    hi = lax.bitcast_convert_type(w & jnp.uint32(0xFFFF0000), F32)
    lo = lax.bitcast_convert_type(w << 16, F32)
    return jnp.concatenate([hi, lo], axis=1).astype(BF16)


def _outproj_kernel(*refs, n_lhs, n_prompt_tiles):
    lhs = refs[:2 * n_lhs]
    w_ref, x_ref, g_ref, b_ref, wr_ref, x1_ref, x1p_ref, lg_ref = refs[2 * n_lhs:]
    is_prompt = pl.program_id(0) < n_prompt_tiles
    k0 = 0
    y = None
    for a_p, a_s in zip(lhs[0::2], lhs[1::2]):
        kk = a_p.shape[1]
        part = _dot(jnp.where(is_prompt, a_p[...], a_s[...]), w_ref[k0:k0 + kk, :])
        y = part if y is None else y + part
        k0 += kk
    x1 = _layer_norm(np.float32(ALPHA) * x_ref[...] + y, g_ref[...], b_ref[...])
    x1_ref[...] = x1
    x1p_ref[...] = _pack_bf16_pairs(x1)
    lg_ref[...] = lax.dot_general(wr_ref[...], x1, (((1,), (1,)), ((), ())), preferred_element_type=F32,
                                  precision=lax.Precision.HIGHEST)


def _outproj_ln(lhs, w, x, g, b, w_router_t, name):
    t = x.shape[0]
    tm = MOE_TM
    n_pt = lhs[0][0].shape[0] // tm
    assert all(a_p.shape[0] == n_pt * tm and a_s.shape[0] == t - n_pt * tm for a_p, a_s in lhs)
    row = lambda a: pl.BlockSpec((tm, a.shape[1]), lambda i: (i, 0))
    full = lambda a: pl.BlockSpec(a.shape, lambda i: (0,) * a.ndim)
    lhs_specs, lhs_args = [], []
    for a_p, a_s in lhs:
        lhs_specs += [pl.BlockSpec((tm, a_p.shape[1]), lambda i: (jnp.minimum(i, n_pt - 1), 0)),
                      pl.BlockSpec((tm, a_s.shape[1]), lambda i: (jnp.maximum(i - n_pt, 0), 0))]
        lhs_args += [a_p, a_s]
    return pl.pallas_call(
        functools.partial(_outproj_kernel, n_lhs=len(lhs), n_prompt_tiles=n_pt),
        grid=(t // tm,),
        in_specs=lhs_specs + [full(w), row(x), full(g), full(b), full(w_router_t)],
        out_specs=[pl.BlockSpec((tm, D_MODEL), lambda i: (i, 0)), pl.BlockSpec((tm, D_MODEL // 2), lambda i: (i, 0)),
                   pl.BlockSpec((N_EXPERTS, tm), lambda i: (0, i))],
        out_shape=[jax.ShapeDtypeStruct((t, D_MODEL), F32), jax.ShapeDtypeStruct((t, D_MODEL // 2), jnp.uint32),
                   jax.ShapeDtypeStruct((N_EXPERTS, t), F32)],
        compiler_params=_params("arbitrary"),
        name=name,
    )(*lhs_args, w, x, g, b, w_router_t)


MOE_TM = 256
EXPERT_BM = 256


def _route_select(lg, bias):
    e, tm = lg.shape
    ninf = np.float32(-np.inf)
    shift = int(np.log2(E_PER_GROUP))
    scores = jax.nn.sigmoid(lg)
    sel = scores + bias
    e_id = lax.broadcasted_iota(jnp.int32, (e, tm), 0)
    g_id = lax.shift_right_logical(e_id, shift)
    g3 = sel.reshape(N_GROUPS, E_PER_GROUP, tm)
    i3 = lax.broadcasted_iota(jnp.int32, g3.shape, 1)
    m1 = g3.max(axis=1, keepdims=True)
    first = jnp.where(g3 == m1, i3, E_PER_GROUP).min(axis=1, keepdims=True)
    m2 = jnp.where(i3 == first, ninf, g3).max(axis=1, keepdims=True)
    grp = jnp.broadcast_to(m1 + m2, g3.shape).reshape(e, tm)
    gsel = jnp.zeros((e, tm), jnp.int32)
    for _ in range(TOPK_GROUPS):
        m = grp.max(axis=0, keepdims=True)
        first = jnp.where(grp == m, e_id, e).min(axis=0, keepdims=True)
        chosen = g_id == lax.shift_right_logical(first, shift)
        gsel = jnp.where(chosen, 1, gsel)
        grp = jnp.where(chosen, ninf, grp)
    sel = jnp.where(gsel > 0, sel, NEG)
    picks = []
    for _ in range(TOP_K):
        m = sel.max(axis=0, keepdims=True)
        first = jnp.where(sel == m, e_id, e).min(axis=0, keepdims=True)
        oh = e_id == first
        picks.append(oh)
        sel = jnp.where(oh, ninf, sel)
    return scores, picks


def _route_kernel(lg_ref, bias_ref, dest_ref, gate_ref, cnt_ref, pad_ref, counts_ref, pads_ref, run_ref):
    phase = pl.program_id(0)
    i = pl.program_id(1)
    e, tm = lg_ref.shape
    scores, picks = _route_select(lg_ref[...], bias_ref[...])
    mask = jnp.zeros((e, tm), F32)
    for oh in picks:
        mask = jnp.where(oh, 1.0, mask)

    @pl.when((phase == 0) & (i == 0))
    def _():
        counts_ref[...] = jnp.zeros_like(counts_ref)

    @pl.when(phase == 0)
    def _():
        counts_ref[...] += mask.sum(axis=1, keepdims=True)

    @pl.when((phase == 1) & (i == 0))
    def _():
        counts = counts_ref[...]
        padded = jnp.ceil(counts * np.float32(1.0 / EXPERT_BM)) * np.float32(EXPERT_BM)
        r_i = lax.broadcasted_iota(jnp.int32, (e, e), 0)
        c_i = lax.broadcasted_iota(jnp.int32, (e, e), 1)
        before = (c_i < r_i).astype(F32)
        starts = jnp.dot(before, jnp.broadcast_to(padded, (e, LANES)), preferred_element_type=F32,
                         precision=lax.Precision.HIGHEST)
        pads_ref[...] = starts[:, 0:1]
        run_ref[...] = jnp.zeros_like(run_ref)
        cnt_ref[...] = jnp.broadcast_to(counts, cnt_ref.shape)
        pad_ref[...] = starts

    @pl.when(phase == 1)
    def _():
        r_i = lax.broadcasted_iota(jnp.int32, (tm, tm), 0)
        c_i = lax.broadcasted_iota(jnp.int32, (tm, tm), 1)
        earlier = (r_i < c_i).astype(BF16)
        slot = pads_ref[...] + run_ref[...] + _dot(mask.astype(BF16), earlier)
        run_ref[...] += mask.sum(axis=1, keepdims=True)
        k_i = lax.broadcasted_iota(jnp.int32, (TOP_K, tm), 0)
        dest = jnp.zeros((TOP_K, tm), F32)
        gate = jnp.zeros((TOP_K, tm), F32)
        total = jnp.zeros((1, tm), F32)
        for k, oh in enumerate(picks):
            d_k = jnp.where(oh, slot, 0.0).sum(axis=0, keepdims=True)
            g_k = jnp.where(oh, scores, 0.0).sum(axis=0, keepdims=True)
            total = total + g_k
            dest = jnp.where(k_i == k, d_k, dest)
            gate = jnp.where(k_i == k, g_k, gate)
        dest_ref[...] = dest.astype(jnp.int32)
        gate_ref[...] = gate / total * np.float32(ROUTED_SCALE)


def _route(lg_t, bias):
    e, t = lg_t.shape
    tm = MOE_TM
    n_t = t // tm
    tile = pl.BlockSpec((TOP_K, tm), lambda p, i: (0, i * p))
    meta = pl.BlockSpec((e, LANES), lambda p, i: (0, 0))
    return pl.pallas_call(
        _route_kernel,
        grid=(2, n_t),
        in_specs=[pl.BlockSpec((e, tm), lambda p, i: (0, i)), pl.BlockSpec((e, 1), lambda p, i: (0, 0))],
        out_specs=[tile, tile, meta, meta],
        out_shape=[jax.ShapeDtypeStruct((TOP_K, t), jnp.int32), jax.ShapeDtypeStruct((TOP_K, t), F32),
                   jax.ShapeDtypeStruct((e, LANES), F32), jax.ShapeDtypeStruct((e, LANES), F32)],
        scratch_shapes=[pltpu.VMEM((e, 1), F32), pltpu.VMEM((e, 1), F32), pltpu.VMEM((e, 1), F32)],
        compiler_params=_params("arbitrary", "arbitrary"),
        name="route",
    )(lg_t, bias)


def _row_copy(src, src_row, dst, dst_row, sem):
    return pltpu.make_async_copy(src.at[pl.ds(src_row, 1)], dst.at[pl.ds(dst_row, 1)], sem)


def _scatter_kernel(dest_ref, x_ref, xs_in, xs_out, sem):
    del xs_in
    tm = x_ref.shape[0]

    def issue(t, c):
        for k in range(TOP_K):
            _row_copy(x_ref, t, xs_out, dest_ref[k * tm + t], sem).start()
        return c

    def drain(t, c):
        for k in range(TOP_K):
            _row_copy(x_ref, t, xs_out, dest_ref[k * tm + t], sem).wait()
        return c

    lax.fori_loop(0, tm, issue, 0)
    lax.fori_loop(0, tm, drain, 0)


def _scatter(dest_flat, x1p, xs_init):
    t, w = x1p.shape
    tm = MOE_TM
    return pl.pallas_call(
        _scatter_kernel,
        grid=(t // tm,),
        in_specs=[pl.BlockSpec((TOP_K * tm,), lambda i: (i,), memory_space=pltpu.SMEM),
                  pl.BlockSpec((tm, w), lambda i: (i, 0)),
                  pl.BlockSpec(memory_space=pl.ANY)],
        out_specs=pl.BlockSpec(memory_space=pl.ANY),
        out_shape=jax.ShapeDtypeStruct(xs_init.shape, xs_init.dtype),
        scratch_shapes=[pltpu.SemaphoreType.DMA(())],
        input_output_aliases={2: 0},
        compiler_params=_params("arbitrary"),
        name="scatter",
    )(dest_flat, x1p, xs_init)


def _expert_kernel(be_ref, nu_ref, x_ref, wg_ref, wu_ref, wd_ref, y_ref, wgb, wub, wdb):
    i = pl.program_id(0)
    new_expert = (i == 0) | (be_ref[i] != be_ref[jnp.maximum(i - 1, 0)])

    @pl.when((i < nu_ref[0]) & new_expert)
    def _():
        wgb[...] = wg_ref[...].astype(BF16)
        wub[...] = wu_ref[...].astype(BF16)
        wdb[...] = wd_ref[...].astype(BF16)

    @pl.when(i < nu_ref[0])
    def _():
        x = _unpack_bf16_pairs(x_ref[...])
        g = _dot(x, wgb[...])
        h = (g * jax.nn.sigmoid(g)) * _dot(x, wub[...])
        y_ref[...] = _dot(h.astype(BF16), wdb[...])

    @pl.when(i >= nu_ref[0])
    def _():
        y_ref[...] = jnp.zeros_like(y_ref)


def _experts(block_e, n_used, xs, layer, wg, wu, wd):
    n_slots = xs.shape[0]
    n_blocks = n_slots // EXPERT_BM
    ff = wg.shape[3]
    grid_spec = pltpu.PrefetchScalarGridSpec(
        num_scalar_prefetch=2,
        grid=(n_blocks,),
        in_specs=[pl.BlockSpec((EXPERT_BM, D_MODEL // 2), lambda i, be, nu: (i, 0)),
                  pl.BlockSpec((None, None, D_MODEL, ff), lambda i, be, nu: (layer, be[i], 0, 0)),
                  pl.BlockSpec((None, None, D_MODEL, ff), lambda i, be, nu: (layer, be[i], 0, 0)),
                  pl.BlockSpec((None, None, ff, D_MODEL), lambda i, be, nu: (layer, be[i], 0, 0))],
        out_specs=pl.BlockSpec((EXPERT_BM, D_MODEL), lambda i, be, nu: (i, 0)),
        scratch_shapes=[pltpu.VMEM((D_MODEL, ff), BF16), pltpu.VMEM((D_MODEL, ff), BF16),
                        pltpu.VMEM((ff, D_MODEL), BF16)],
    )
    return pl.pallas_call(
        _expert_kernel,
        grid_spec=grid_spec,
        out_shape=jax.ShapeDtypeStruct((n_slots, D_MODEL), F32),
        compiler_params=_params("arbitrary"),
        name="experts",
    )(block_e, n_used, xs, wg, wu, wd)


def _ffn_ln_kernel(dest_ref, xp_ref, x_ref, gate_ref, y_hbm, wg_ref, wu_ref, wd_ref, g_ref, b_ref, o_ref, buf, sem):
    tm = x_ref.shape[0]

    def issue(t, c):
        for k in range(TOP_K):
            _row_copy(y_hbm, dest_ref[k * tm + t], buf.at[k], t, sem).start()
        return c

    def drain(t, c):
        for k in range(TOP_K):
            _row_copy(y_hbm, dest_ref[k * tm + t], buf.at[k], t, sem).wait()
        return c

    lax.fori_loop(0, tm, issue, 0)
    xb = _unpack_bf16_pairs(xp_ref[...])
    gte = _dot(xb, wg_ref[...])
    h = (gte * jax.nn.sigmoid(gte)) * _dot(xb, wu_ref[...])
    shared = _dot(h.astype(BF16), wd_ref[...])
    lax.fori_loop(0, tm, drain, 0)
    routed = None
    for k in range(TOP_K):
        part = buf[k] * gate_ref[:, k:k + 1]
        routed = part if routed is None else routed + part
    z = np.float32(ALPHA) * x_ref[...] + (routed + shared)
    o_ref[...] = _layer_norm(z, g_ref[...], b_ref[...])


def _ffn_ln(dest_flat, x1p, x1, gate_t, y, wg, wu, wd, g, b):
    t = x1.shape[0]
    tm = MOE_TM
    row = lambda a: pl.BlockSpec((tm, a.shape[1]), lambda i: (i, 0))
    full = lambda a: pl.BlockSpec(a.shape, lambda i: (0,) * a.ndim)
    return pl.pallas_call(
        _ffn_ln_kernel,
        grid=(t // tm,),
        in_specs=[pl.BlockSpec((TOP_K * tm,), lambda i: (i,), memory_space=pltpu.SMEM),
                  row(x1p), row(x1), row(gate_t), pl.BlockSpec(memory_space=pl.ANY),
                  full(wg), full(wu), full(wd), full(g), full(b)],
        out_specs=row(x1),
        out_shape=jax.ShapeDtypeStruct((t, D_MODEL), F32),
        scratch_shapes=[pltpu.VMEM((TOP_K, tm, D_MODEL), F32), pltpu.SemaphoreType.DMA(())],
        compiler_params=_params("arbitrary"),
        name="ffn_ln",
    )(dest_flat, x1p, x1, gate_t, y, wg, wu, wd, g, b)


def _ple_kernel(xf_ref, xs_ref, p_ref, wp_ref, wg_ref, *o_refs, n_prompt_tiles):
    gate = jax.nn.sigmoid(_dot(xf_ref[...].astype(BF16), wg_ref[...]))
    emb = _dot(p_ref[...].astype(BF16), wp_ref[...])
    out = xs_ref[...] + emb * gate
    if len(o_refs) == 1:
        o_refs[0][...] = out
    else:
        i = pl.program_id(1)

        @pl.when(i < n_prompt_tiles)
        def _():
            o_refs[0][...] = out

        @pl.when(i >= n_prompt_tiles)
        def _():
            o_refs[1][...] = out


def _ple(x2, p, w_ple, w_gate, n_prompt=None):
    t = x2.shape[0]
    tm = _tile(t if n_prompt is None else np.gcd(n_prompt, t - n_prompt), 512)
    tn = _tile(D_MODEL, 1024)
    n_pt = None if n_prompt is None else n_prompt // tm
    if n_prompt is None:
        out_specs = [pl.BlockSpec((tm, tn), lambda n, i: (i, n))]
        out_shape = [jax.ShapeDtypeStruct((t, D_MODEL), F32)]
    else:
        out_specs = [pl.BlockSpec((tm, tn), lambda n, i: (jnp.minimum(i, n_pt - 1), n)),
                     pl.BlockSpec((tm, tn), lambda n, i: (jnp.maximum(i - n_pt, 0), n))]
        out_shape = [jax.ShapeDtypeStruct((n_prompt, D_MODEL), F32),
                     jax.ShapeDtypeStruct((t - n_prompt, D_MODEL), F32)]
    return pl.pallas_call(
        functools.partial(_ple_kernel, n_prompt_tiles=n_pt),
        grid=(D_MODEL // tn, t // tm),
        in_specs=[pl.BlockSpec((tm, D_MODEL), lambda n, i: (i, 0)),
                  pl.BlockSpec((tm, tn), lambda n, i: (i, n)),
                  pl.BlockSpec((tm, p.shape[1]), lambda n, i: (i, 0)),
                  pl.BlockSpec((w_ple.shape[0], tn), lambda n, i: (0, n)),
                  pl.BlockSpec((D_MODEL, tn), lambda n, i: (0, n))],
        out_specs=out_specs, out_shape=out_shape,
        compiler_params=_params("arbitrary", "arbitrary"),
        name="ple",
    )(x2, x2, p, w_ple, w_gate)


def _rel_bias(table, qpos, kpos):
    idx = np.clip(qpos[:, None] - kpos[None, :], -REL_CLIP, REL_CLIP) + REL_CLIP
    return table[:, idx]


def _band_prompt_bias(table):
    tq, width = BAND_TQ, BAND_NWB * BAND_TQ
    span = width + tq - 1
    rel = (BAND_NWB - 1) * tq + tq - 1 - np.arange(span)
    strip = table[:, np.clip(rel, -REL_CLIP, REL_CLIP) + REL_CLIP]
    u = jnp.concatenate([strip[:, tq - 1:], strip[:, :1], strip[:, :tq - 1]], axis=1)
    flat = jnp.tile(u, (1, tq))[:, :tq * span]
    return flat.reshape(table.shape[0], tq, span)[:, :, :width]


def _band_mask(qpos, kpos):
    dc = qpos // CHUNK - kpos // CHUNK
    return (kpos >= 0) & (dc >= 0) & (dc <= B_PREV_CHUNKS)


def _post_block(x, lhs, w_out, p, ln1_g, ln1_b, ln2_g, ln2_b, w_router, b_router,
                layer, w_gate, w_up, w_down, ws_gate, ws_up, ws_down, w_ple, w_ple_gate, name, n_prompt):
    row = lambda a: a.reshape(1, -1)
    t = x.shape[0]
    x1, x1p, lg_t = _outproj_ln(lhs, w_out, x, row(ln1_g), row(ln1_b), w_router.T, name)
    dest, gate, cnt, pad = _route(lg_t, b_router.astype(F32).reshape(N_EXPERTS, 1))

    n_blocks = -(-(t * TOP_K + N_EXPERTS * (EXPERT_BM - 1)) // EXPERT_BM)
    counts = cnt[:, 0].astype(jnp.int32)
    pad_end = pad[:, 0].astype(jnp.int32) + (counts + EXPERT_BM - 1) // EXPERT_BM * EXPERT_BM
    block_e = jnp.minimum((pad_end[:, None] <= jnp.arange(n_blocks)[None, :] * EXPERT_BM).sum(axis=0),
                          N_EXPERTS - 1).astype(jnp.int32)
    n_used = (pad_end[-1:] // EXPERT_BM).astype(jnp.int32)
    dest_flat = dest.reshape(TOP_K, t // MOE_TM, MOE_TM).transpose(1, 0, 2).reshape(-1)

    xs = _scatter(dest_flat, x1p, jnp.zeros((n_blocks * EXPERT_BM, D_MODEL // 2), jnp.uint32))
    y = _experts(block_e, n_used, xs, layer, w_gate, w_up, w_down)
    x2 = _ffn_ln(dest_flat, x1p, x1, gate.T, y, ws_gate, ws_up, ws_down, row(ln2_g), row(ln2_b))
    return _ple(x2, p, w_ple, w_ple_gate, n_prompt)


def kernel(x_prompt, x_sample, cache_b_k, cache_b_v, cache_c_k, cache_c_v, p_prompt, p_sample, w_in_ab, w_out_ab, sgu_w, sgu_b, sgu_ln_g, sgu_ln_b, rel_bias_tab, w_in_c, w_out_c, ln_mix_g, ln_mix_b, ln_ffn_g, ln_ffn_b, w_router, b_router, w_gate, w_up, w_down, ws_gate, ws_up, ws_down, w_ple, w_ple_gate):
    n_p = BATCH * SEQ
    n_s = DEC_BATCH * DEC_SEQ
    x = jnp.concatenate([x_prompt.reshape(n_p, D_MODEL), x_sample.reshape(n_s, D_MODEL)], axis=0)
    p_all = jnp.concatenate([p_prompt.reshape(DEPTH, n_p, PLE_DIM), p_sample.reshape(DEPTH, n_s, PLE_DIM)], axis=1)
    bf = lambda a: a.astype(BF16)
    outs = {k: [] for k in ("bk_p", "bv_p", "bk_s", "bv_s", "av_s", "ck_p", "cv_p", "ck_s", "cv_s")}
    kv_outs = ((F32, True), (BF16, False))

    for i in range(DEPTH):
        j = i // 2
        if i % 2 == 0:
            w_in = w_in_ab[j]
            hu, = _matmul(x, w_in, 0, A_WIDTH, ((F32, False),), "proj_u")
            hv, = _matmul(x, w_in, A_WIDTH, A_WIDTH, ((F32, False),), "proj_v")
            q, = _matmul(x, w_in, 2 * A_WIDTH, B_WIDTH, ((BF16, False),), "proj_q")
            k_p, k_s, kb16 = _matmul(x, w_in, 2 * A_WIDTH + B_WIDTH, B_WIDTH, kv_outs, "proj_k", n_p)
            v_p, v_s, vb16 = _matmul(x, w_in, 2 * A_WIDTH + 2 * B_WIDTH, B_WIDTH, kv_outs, "proj_vv", n_p)

            bs_t = sgu_b[j].T
            a_p, = _sgu(hu, hv, sgu_w[j], bs_t, sgu_ln_g[j], sgu_ln_b[j], A_CHUNK, 0, n_p // A_CHUNK, False)
            a_s, va_s = _sgu(hu, hv, sgu_w[j][:, :DEC_SEQ, :DEC_SEQ], bs_t[:DEC_SEQ], sgu_ln_g[j], sgu_ln_b[j],
                             DEC_SEQ, n_p, DEC_BATCH, True)

            b_p = _band_prompt(q, kb16, vb16, _band_prompt_bias(rel_bias_tab[j]))
            n_cache = cache_b_k.shape[2]
            qs = PAST_LEN + np.arange(DEC_SEQ)
            kc = PAST_LEN - n_cache + np.arange(n_cache)
            b_s = _band_sample(
                q, kb16, vb16,
                cache_b_k[j].reshape(DEC_BATCH, n_cache, B_WIDTH), cache_b_v[j].reshape(DEC_BATCH, n_cache, B_WIDTH),
                _rel_bias(rel_bias_tab[j], qs, kc), _rel_bias(rel_bias_tab[j], qs, qs),
                jnp.asarray(_band_mask(qs[:, None], kc[None, :]), F32),
                jnp.asarray(_band_mask(qs[:, None], qs[None, :]), F32))

            keep = min(B_WINDOW, SEQ)
            outs["bk_p"].append(k_p.reshape(BATCH, SEQ, B_HEADS, B_DIM)[:, SEQ - keep:])
            outs["bv_p"].append(v_p.reshape(BATCH, SEQ, B_HEADS, B_DIM)[:, SEQ - keep:])
            outs["bk_s"].append(k_s.reshape(DEC_BATCH, DEC_SEQ, B_HEADS, B_DIM))
            outs["bv_s"].append(v_s.reshape(DEC_BATCH, DEC_SEQ, B_HEADS, B_DIM))
            outs["av_s"].append(va_s.reshape(DEC_BATCH, DEC_SEQ, A_HEADS, A_DIM))
            lhs, w_out, name = [(a_p, a_s), (b_p, b_s)], bf(w_out_ab[j]), "outproj_ab"
        else:
            w_in = w_in_c[j]
            q, = _matmul(x, w_in, 0, C_WIDTH, ((BF16, False),), "proj_cq")
            k_p, k_s, kb16 = _matmul(x, w_in, C_WIDTH, C_WIDTH, kv_outs, "proj_ck", n_p)
            v_p, v_s, vb16 = _matmul(x, w_in, 2 * C_WIDTH, C_WIDTH, kv_outs, "proj_cv", n_p)
            o_p = _sb_prompt(q, kb16, vb16)
            o_s = _sb_sample(q, kb16, vb16, cache_c_k[j].reshape(DEC_BATCH, PAST_LEN, C_WIDTH),
                             cache_c_v[j].reshape(DEC_BATCH, PAST_LEN, C_WIDTH))
            outs["ck_p"].append(k_p.reshape(BATCH, SEQ, C_HEADS, C_DIM))
            outs["cv_p"].append(v_p.reshape(BATCH, SEQ, C_HEADS, C_DIM))
            outs["ck_s"].append(k_s.reshape(DEC_BATCH, DEC_SEQ, C_HEADS, C_DIM))
            outs["cv_s"].append(v_s.reshape(DEC_BATCH, DEC_SEQ, C_HEADS, C_DIM))
            lhs, w_out, name = [(o_p, o_s)], bf(w_out_c[j]), "outproj_c"

        x = _post_block(x, lhs, w_out, p_all[i], ln_mix_g[i], ln_mix_b[i], ln_ffn_g[i], ln_ffn_b[i],
                        w_router[i], b_router[i], i, w_gate, w_up, w_down,
                        bf(ws_gate[i]), bf(ws_up[i]), bf(ws_down[i]), bf(w_ple[i]), bf(w_ple_gate[i]), name,
                        n_p if i == DEPTH - 1 else None)
        if i < DEPTH - 1:
            x, = x

    y_p, y_s = x
    st = lambda key: jnp.stack(outs[key])
    return (y_p.reshape(BATCH, SEQ, D_MODEL), y_s.reshape(DEC_BATCH, DEC_SEQ, D_MODEL),
            st("bk_p"), st("bv_p"), st("bk_s"), st("bv_s"), st("av_s"),
            st("ck_p"), st("cv_p"), st("ck_s"), st("cv_s"))
```

```python
import functools

import numpy as np
import jax
import jax.numpy as jnp
from jax import lax
from jax.experimental import pallas as pl
from jax.experimental.pallas import tpu as pltpu

D_MODEL = 2048
BATCH = 2
SEQ = 4096
DEPTH = 2
DEC_BATCH = 32
DEC_SEQ = 16
PAST_LEN = 2048

CHUNK = 64
N_EVEN = (DEPTH + 1) // 2
N_ODD = DEPTH // 2
A_CHUNK = 128
A_HEADS = 8
A_DIM = 128
A_WIDTH = A_HEADS * A_DIM
B_HEADS = 8
B_DIM = 128
B_WIDTH = B_HEADS * B_DIM
B_PREV_CHUNKS = 8
B_WINDOW = B_PREV_CHUNKS * CHUNK
REL_CLIP = 128
C_HEADS = 16
C_DIM = 128
C_WIDTH = C_HEADS * C_DIM
N_EXPERTS = 64
N_GROUPS = 8
E_PER_GROUP = N_EXPERTS // N_GROUPS
TOPK_GROUPS = 4
TOP_K = 8
EXPERT_FF = 512
SHARED_FF = 512
ROUTED_SCALE = 2.5
PLE_DIM = 256
LN_EPS = 1e-5
ALPHA = (2 * DEPTH) ** 0.25
NEG = -1e9

LANES = 128
VMEM_LIMIT_BYTES = 56 * 1024 * 1024

BF16 = jnp.bfloat16
F32 = jnp.float32


def _params(*sem):
    return pltpu.CompilerParams(dimension_semantics=sem, vmem_limit_bytes=VMEM_LIMIT_BYTES)


def _tile(n, pref):
    if n <= pref:
        return n
    for t in range(pref, 7, -1):
        if n % t == 0 and t % 8 == 0:
            return t
    return n


def _dot(a, b):
    return jnp.dot(a, b, preferred_element_type=F32)


def _dot_nt(a, b):
    return lax.dot_general(a, b, (((1,), (1,)), ((), ())), preferred_element_type=F32)


def _layer_norm(z, g, b):
    mu = jnp.mean(z, axis=-1, keepdims=True)
    zc = z - mu
    var = jnp.mean(zc * zc, axis=-1, keepdims=True)
    return zc * lax.rsqrt(var + LN_EPS) * g + b


def _gelu(x):
    return x * (lax.erf(x * np.float32(1.0 / np.sqrt(2.0))) + 1.0) * 0.5


def _softplus(z):
    return jnp.maximum(z, 0.0) + jnp.log(1.0 + jnp.exp(-jnp.abs(z)))


def _mm_kernel(x_ref, w_ref, *rest, split, n_prompt_tiles):
    o_refs, wb_ref = rest[:-1], rest[-1]
    i = pl.program_id(1)

    @pl.when(i == 0)
    def _():
        wb_ref[...] = w_ref[...].astype(BF16)

    acc = _dot(x_ref[...].astype(BF16), wb_ref[...])
    refs = iter(o_refs)
    for is_split in split:
        if is_split:
            o_p, o_s = next(refs), next(refs)

            @pl.when(i < n_prompt_tiles)
            def _():
                o_p[...] = acc.astype(o_p.dtype)

            @pl.when(i >= n_prompt_tiles)
            def _():
                o_s[...] = acc.astype(o_s.dtype)
        else:
            o = next(refs)
            o[...] = acc.astype(o.dtype)


def _matmul(x, w, col0, ncols, outs, name, n_prompt=None):
    m, k = x.shape
    n_prompt = m if n_prompt is None else n_prompt
    tm = _tile(np.gcd(n_prompt, m - n_prompt) if n_prompt < m else m, 512)
    tn = _tile(ncols, 1024)
    assert col0 % tn == 0 and m % tm == 0 and n_prompt % tm == 0
    nb0 = col0 // tn
    n_pt = n_prompt // tm
    out_specs, out_shape = [], []
    for dtype, is_split in outs:
        if is_split:
            out_specs += [pl.BlockSpec((tm, tn), lambda n, i: (jnp.minimum(i, n_pt - 1), n)),
                          pl.BlockSpec((tm, tn), lambda n, i: (jnp.maximum(i - n_pt, 0), n))]
            out_shape += [jax.ShapeDtypeStruct((n_prompt, ncols), dtype),
                          jax.ShapeDtypeStruct((m - n_prompt, ncols), dtype)]
        else:
            out_specs.append(pl.BlockSpec((tm, tn), lambda n, i: (i, n)))
            out_shape.append(jax.ShapeDtypeStruct((m, ncols), dtype))
    return pl.pallas_call(
        functools.partial(_mm_kernel, split=tuple(s for _, s in outs), n_prompt_tiles=n_pt),
        grid=(ncols // tn, m // tm),
        in_specs=[pl.BlockSpec((tm, k), lambda n, i: (i, 0)),
                  pl.BlockSpec((k, tn), lambda n, i: (0, n + nb0))],
        out_specs=out_specs, out_shape=out_shape,
        scratch_shapes=[pltpu.VMEM((k, tn), BF16)],
        compiler_params=_params("arbitrary", "arbitrary"),
        name=name,
    )(x, w)


def _sgu_kernel(hu_ref, hv_ref, w_ref, bs_ref, g_ref, b_ref, a_ref, *v_out, emit_v):
    rows = hu_ref.shape[0]
    r_i = lax.broadcasted_iota(jnp.int32, (rows, rows), 0)
    c_i = lax.broadcasted_iota(jnp.int32, (rows, rows), 1)
    causal = c_i <= r_i
    for h in range(A_HEADS):
        sl = slice(h * A_DIM, (h + 1) * A_DIM)
        u = _gelu(hu_ref[:, sl])
        v = _layer_norm(_gelu(hv_ref[:, sl]), g_ref[h:h + 1, :], b_ref[h:h + 1, :])
        w = jnp.where(causal, w_ref[h], 0.0).astype(BF16)
        mix = _dot(w, v.astype(BF16)) + bs_ref[:, h:h + 1]
        a_ref[:, sl] = (u * mix).astype(a_ref.dtype)
        if emit_v:
            v_out[0][:, sl] = v


def _sgu(hu, hv, w_s, bs_t, g, b, rows, row0, n_chunks, emit_v):
    off = row0 // rows
    in_spec = pl.BlockSpec((rows, A_WIDTH), lambda c: (c + off, 0))
    out_spec = pl.BlockSpec((rows, A_WIDTH), lambda c: (c, 0))
    full = lambda a: pl.BlockSpec(a.shape, lambda c: (0,) * a.ndim)
    out_specs = [out_spec]
    out_shape = [jax.ShapeDtypeStruct((n_chunks * rows, A_WIDTH), BF16)]
    if emit_v:
        out_specs.append(out_spec)
        out_shape.append(jax.ShapeDtypeStruct((n_chunks * rows, A_WIDTH), F32))
    return pl.pallas_call(
        functools.partial(_sgu_kernel, emit_v=emit_v),
        grid=(n_chunks,),
        in_specs=[in_spec, in_spec, full(w_s), full(bs_t), full(g), full(b)],
        out_specs=out_specs, out_shape=out_shape,
        compiler_params=_params("arbitrary"),
        name="sgu_sample" if emit_v else "sgu_prompt",
    )(hu, hv, w_s, bs_t, g, b)


BAND_TQ = 2 * CHUNK
BAND_NWB = B_WINDOW // BAND_TQ + 1


def _band_prompt_kernel(q_ref, k_ref, v_ref, bias_ref, o_ref):
    t = pl.program_id(2)
    tq = BAND_TQ
    q = q_ref[...]
    shift = int(np.log2(CHUNK))
    q_chunk = lax.shift_right_arithmetic(t * tq + lax.broadcasted_iota(jnp.int32, (tq, 1), 0), shift)
    lane = lax.broadcasted_iota(jnp.int32, (1, tq), 1)
    s_blocks = []
    for j in range(BAND_NWB):
        kb = t - (BAND_NWB - 1) + j
        start = pl.multiple_of(jnp.maximum(kb, 0) * tq, tq)
        s = _dot_nt(q, k_ref[pl.ds(start, tq), :]) * np.float32(B_DIM ** -0.5) + bias_ref[0, :, j * tq:(j + 1) * tq]
        kpos = kb * tq + lane
        dc = q_chunk - lax.shift_right_arithmetic(kpos, shift)
        valid = (kpos >= 0) & (dc >= 0) & (dc <= B_PREV_CHUNKS)
        s_blocks.append(jnp.where(valid, s, NEG))
    m = s_blocks[0].max(axis=-1, keepdims=True)
    for s in s_blocks[1:]:
        m = jnp.maximum(m, s.max(axis=-1, keepdims=True))
    acc = jnp.zeros((tq, B_DIM), F32)
    den = jnp.zeros((tq, 1), F32)
    for j, s in enumerate(s_blocks):
        kb = t - (BAND_NWB - 1) + j
        start = pl.multiple_of(jnp.maximum(kb, 0) * tq, tq)
        p = jnp.exp(s - m)
        den = den + p.sum(axis=-1, keepdims=True)
        acc = acc + _dot(p.astype(BF16), v_ref[pl.ds(start, tq), :])
    o_ref[...] = (acc / den).astype(o_ref.dtype)


def _band_prompt(q, k, v, bias):
    t_all = BATCH * SEQ
    n_t = SEQ // BAND_TQ
    q_spec = pl.BlockSpec((BAND_TQ, B_DIM), lambda b, h, t: (b * n_t + t, h))
    kv_spec = pl.BlockSpec((SEQ, B_DIM), lambda b, h, t: (b, h))
    return pl.pallas_call(
        _band_prompt_kernel,
        grid=(BATCH, B_HEADS, n_t),
        in_specs=[q_spec, kv_spec, kv_spec,
                  pl.BlockSpec((1, BAND_TQ, BAND_NWB * BAND_TQ), lambda b, h, t: (h, 0, 0))],
        out_specs=q_spec,
        out_shape=jax.ShapeDtypeStruct((t_all, B_WIDTH), BF16),
        compiler_params=_params("arbitrary", "arbitrary", "arbitrary"),
        name="band_prompt",
    )(q, k, v, bias)


def _band_sample_kernel(q_ref, kn_ref, vn_ref, kc_ref, vc_ref, bc_ref, bn_ref, mc_ref, mn_ref, o_ref):
    scale = np.float32(B_DIM ** -0.5)
    n_cache = kc_ref.shape[0] // B_HEADS
    for h in range(B_HEADS):
        sl = slice(h * B_DIM, (h + 1) * B_DIM)
        q = q_ref[:, sl]
        s_c = _dot_nt(q, kc_ref[pl.ds(h, n_cache, stride=B_HEADS), :].astype(BF16)) * scale + bc_ref[h]
        s_n = _dot_nt(q, kn_ref[:, sl]) * scale + bn_ref[h]
        s_c = jnp.where(mc_ref[...] > 0, s_c, NEG)
        s_n = jnp.where(mn_ref[...] > 0, s_n, NEG)
        m = jnp.maximum(s_c.max(axis=-1, keepdims=True), s_n.max(axis=-1, keepdims=True))
        p_c = jnp.exp(s_c - m)
        p_n = jnp.exp(s_n - m)
        den = p_c.sum(axis=-1, keepdims=True) + p_n.sum(axis=-1, keepdims=True)
        acc = (_dot(p_c.astype(BF16), vc_ref[pl.ds(h, n_cache, stride=B_HEADS), :].astype(BF16))
               + _dot(p_n.astype(BF16), vn_ref[:, sl]))
        o_ref[:, sl] = (acc / den).astype(o_ref.dtype)


def _band_sample(q, k, v, cache_k, cache_v, layer, bias_c, bias_n, mask_c, mask_n):
    n = DEC_SEQ
    off = BATCH * SEQ // n
    n_cache = cache_k.shape[2] // B_HEADS
    row_spec = pl.BlockSpec((n, B_WIDTH), lambda b: (b + off, 0))
    cache_spec = pl.BlockSpec((None, None, n_cache * B_HEADS, B_DIM), lambda b: (layer, b, 0, 0))
    full = lambda a: pl.BlockSpec(a.shape, lambda b: (0,) * a.ndim)
    return pl.pallas_call(
        _band_sample_kernel,
        grid=(DEC_BATCH,),
        in_specs=[row_spec, row_spec, row_spec, cache_spec, cache_spec,
                  full(bias_c), full(bias_n), full(mask_c), full(mask_n)],
        out_specs=pl.BlockSpec((n, B_WIDTH), lambda b: (b, 0)),
        out_shape=jax.ShapeDtypeStruct((DEC_BATCH * n, B_WIDTH), BF16),
        compiler_params=_params("arbitrary"),
        name="band_sample",
    )(q, k, v, cache_k, cache_v, bias_c, bias_n, mask_c, mask_n)


SB_TQ = 256
SB_TK = 256
SB_SAMPLE_TK = 512
SB_HP = 2


def _sb_weights(z, carry, upper):
    sp = _softplus(z)
    hi = sp.astype(BF16)
    lo = (sp - hi.astype(F32)).astype(BF16)
    later = _dot(hi, upper) + _dot(lo, upper)
    w = jnp.exp((z - sp) - (carry + later))
    return w, carry + later[:, 0:1] + sp[:, 0:1]


def _upper(n):
    r_i = lax.broadcasted_iota(jnp.int32, (n, n), 0)
    c_i = lax.broadcasted_iota(jnp.int32, (n, n), 1)
    return (r_i > c_i).astype(BF16)


def _sb_prompt_kernel(q_ref, k_ref, v_ref, o_ref):
    i = pl.program_id(2)
    upper = _upper(SB_TK)
    scale = np.float32(C_DIM ** -0.5)
    r_i = lax.broadcasted_iota(jnp.int32, (SB_TQ, SB_TK), 0)
    c_i = lax.broadcasted_iota(jnp.int32, (SB_TQ, SB_TK), 1)
    heads = [slice(h * C_DIM, (h + 1) * C_DIM) for h in range(SB_HP)]
    qs = [q_ref[:, sl] for sl in heads]

    def block(j, state, diagonal):
        start = pl.multiple_of(j * SB_TK, SB_TK)
        out = []
        for sl, q, (carry, acc) in zip(heads, qs, state):
            z = _dot_nt(q, k_ref[pl.ds(start, SB_TK), sl]) * scale
            if diagonal:
                z = jnp.where(c_i < r_i, z, NEG)
            w, carry = _sb_weights(z, carry, upper)
            out.append((carry, acc + _dot(w.astype(BF16), v_ref[pl.ds(start, SB_TK), sl])))
        return tuple(out)

    zero = (jnp.zeros((SB_TQ, 1), F32), jnp.zeros((SB_TQ, C_DIM), F32))
    state = block(i, (zero,) * SB_HP, True)
    state = lax.fori_loop(0, i, lambda jj, st: block(i - 1 - jj, st, False), state)
    for sl, (_, acc) in zip(heads, state):
        o_ref[:, sl] = acc.astype(o_ref.dtype)


def _sb_prompt(q, k, v):
    assert SB_TQ == SB_TK and C_HEADS % SB_HP == 0
    t_all = BATCH * SEQ
    n_q = SEQ // SB_TQ
    q_spec = pl.BlockSpec((SB_TQ, SB_HP * C_DIM), lambda b, h, i: (b * n_q + i, h))
    kv_spec = pl.BlockSpec((SEQ, SB_HP * C_DIM), lambda b, h, i: (b, h))
    return pl.pallas_call(
        _sb_prompt_kernel,
        grid=(BATCH, C_HEADS // SB_HP, n_q),
        in_specs=[q_spec, kv_spec, kv_spec],
        out_specs=q_spec,
        out_shape=jax.ShapeDtypeStruct((t_all, C_WIDTH), BF16),
        compiler_params=_params("arbitrary", "arbitrary", "arbitrary"),
        name="sb_prompt",
    )(q, k, v)


def _sb_sample_kernel(q_ref, kn_ref, vn_ref, kc_ref, vc_ref, o_ref, z_ref, w_ref, carry_ref, acc_ref):
    jj = pl.program_id(1)
    n = DEC_SEQ
    tk = kc_ref.shape[0] // C_HEADS
    scale = np.float32(C_DIM ** -0.5)
    col = lambda h: slice(h * C_DIM, (h + 1) * C_DIM)

    def all_heads(k_of, v_of, width, mask):
        for h in range(C_HEADS):
            z_ref[h * n:(h + 1) * n, 0:width] = _dot_nt(q_ref[:, col(h)], k_of(h)) * scale
        z = z_ref[:, 0:width]
        if mask is not None:
            z = jnp.where(mask, z, NEG)
        w, carry = _sb_weights(z, carry_ref[...], _upper(width))
        carry_ref[...] = carry
        w_ref[:, 0:width] = w.astype(BF16)
        for h in range(C_HEADS):
            acc_ref[:, col(h)] += _dot(w_ref[h * n:(h + 1) * n, 0:width], v_of(h))

    @pl.when(jj == 0)
    def _():
        carry_ref[...] = jnp.zeros_like(carry_ref)
        acc_ref[...] = jnp.zeros_like(acc_ref)
        r_i = lax.broadcasted_iota(jnp.int32, (C_HEADS * n, n), 0) % n
        c_i = lax.broadcasted_iota(jnp.int32, (C_HEADS * n, n), 1)
        all_heads(lambda h: kn_ref[:, col(h)], lambda h: vn_ref[:, col(h)], n, c_i < r_i)

    all_heads(lambda h: kc_ref[pl.ds(h, tk, stride=C_HEADS), :].astype(BF16),
              lambda h: vc_ref[pl.ds(h, tk, stride=C_HEADS), :].astype(BF16), tk, None)

    @pl.when(jj == pl.num_programs(1) - 1)
    def _():
        o_ref[...] = acc_ref[...].astype(o_ref.dtype)


def _sb_sample(q, k, v, cache_k, cache_v, layer):
    n = DEC_SEQ
    off = BATCH * SEQ // n
    past = cache_k.shape[2] // C_HEADS
    tk = _tile(past, SB_SAMPLE_TK)
    n_kb = past // tk
    row_spec = pl.BlockSpec((n, C_WIDTH), lambda b, j: (b + off, 0))
    cache_spec = pl.BlockSpec((None, None, tk * C_HEADS, C_DIM), lambda b, j: (layer, b, n_kb - 1 - j, 0))
    return pl.pallas_call(
        _sb_sample_kernel,
        grid=(DEC_BATCH, n_kb),
        in_specs=[row_spec, row_spec, row_spec, cache_spec, cache_spec],
        out_specs=pl.BlockSpec((n, C_WIDTH), lambda b, j: (b, 0)),
        out_shape=jax.ShapeDtypeStruct((DEC_BATCH * n, C_WIDTH), BF16),
        scratch_shapes=[pltpu.VMEM((C_HEADS * n, tk), F32), pltpu.VMEM((C_HEADS * n, tk), BF16),
                        pltpu.VMEM((C_HEADS * n, 1), F32), pltpu.VMEM((n, C_WIDTH), F32)],
        compiler_params=_params("arbitrary", "arbitrary"),
        name="sb_sample",
    )(q, k, v, cache_k, cache_v)


def _pack_bf16_pairs(x):
    half = x.shape[1] // 2
    bits = lax.bitcast_convert_type(x.astype(BF16).astype(F32), jnp.uint32)
    return (bits[:, :half] & jnp.uint32(0xFFFF0000)) | (bits[:, half:] >> 16)


def _unpack_bf16_pairs(w):
    hi = lax.bitcast_convert_type(w & jnp.uint32(0xFFFF0000), F32)
    lo = lax.bitcast_convert_type(w << 16, F32)
    return jnp.concatenate([hi, lo], axis=1).astype(BF16)


def _outproj_kernel(*refs, n_lhs, n_prompt_tiles):
    lhs = refs[:2 * n_lhs]
    w_ref, x_ref, g_ref, b_ref, wr_ref, x1_ref, x1p_ref, lg_ref = refs[2 * n_lhs:]
    is_prompt = pl.program_id(0) < n_prompt_tiles
    k0 = 0
    y = None
    for a_p, a_s in zip(lhs[0::2], lhs[1::2]):
        kk = a_p.shape[1]
        part = _dot(jnp.where(is_prompt, a_p[...], a_s[...]), w_ref[k0:k0 + kk, :])
        y = part if y is None else y + part
        k0 += kk
    x1 = _layer_norm(np.float32(ALPHA) * x_ref[...] + y, g_ref[...], b_ref[...])
    x1_ref[...] = x1
    x1p_ref[...] = _pack_bf16_pairs(x1)
    lg_ref[...] = lax.dot_general(wr_ref[...], x1, (((1,), (1,)), ((), ())), preferred_element_type=F32,
                                  precision=lax.Precision.HIGHEST)


def _outproj_ln(lhs, w, x, g, b, w_router_t, name):
    t = x.shape[0]
    tm = MOE_TM
    n_pt = lhs[0][0].shape[0] // tm
    assert all(a_p.shape[0] == n_pt * tm and a_s.shape[0] == t - n_pt * tm for a_p, a_s in lhs)
    row = lambda a: pl.BlockSpec((tm, a.shape[1]), lambda i: (i, 0))
    full = lambda a: pl.BlockSpec(a.shape, lambda i: (0,) * a.ndim)
    lhs_specs, lhs_args = [], []
    for a_p, a_s in lhs:
        lhs_specs += [pl.BlockSpec((tm, a_p.shape[1]), lambda i: (jnp.minimum(i, n_pt - 1), 0)),
                      pl.BlockSpec((tm, a_s.shape[1]), lambda i: (jnp.maximum(i - n_pt, 0), 0))]
        lhs_args += [a_p, a_s]
    return pl.pallas_call(
        functools.partial(_outproj_kernel, n_lhs=len(lhs), n_prompt_tiles=n_pt),
        grid=(t // tm,),
        in_specs=lhs_specs + [full(w), row(x), full(g), full(b), full(w_router_t)],
        out_specs=[pl.BlockSpec((tm, D_MODEL), lambda i: (i, 0)), pl.BlockSpec((tm, D_MODEL // 2), lambda i: (i, 0)),
                   pl.BlockSpec((N_EXPERTS, tm), lambda i: (0, i))],
        out_shape=[jax.ShapeDtypeStruct((t, D_MODEL), F32), jax.ShapeDtypeStruct((t, D_MODEL // 2), jnp.uint32),
                   jax.ShapeDtypeStruct((N_EXPERTS, t), F32)],
        compiler_params=_params("arbitrary"),
        name=name,
    )(*lhs_args, w, x, g, b, w_router_t)


MOE_TM = 256
EXPERT_BM = 256


def _route_select(lg, bias):
    e, tm = lg.shape
    ninf = np.float32(-np.inf)
    shift = int(np.log2(E_PER_GROUP))
    scores = jax.nn.sigmoid(lg)
    sel = scores + bias
    e_id = lax.broadcasted_iota(jnp.int32, (e, tm), 0)
    g_id = lax.shift_right_logical(e_id, shift)
    g3 = sel.reshape(N_GROUPS, E_PER_GROUP, tm)
    i3 = lax.broadcasted_iota(jnp.int32, g3.shape, 1)
    m1 = g3.max(axis=1, keepdims=True)
    first = jnp.where(g3 == m1, i3, E_PER_GROUP).min(axis=1, keepdims=True)
    m2 = jnp.where(i3 == first, ninf, g3).max(axis=1, keepdims=True)
    grp = jnp.broadcast_to(m1 + m2, g3.shape).reshape(e, tm)
    gsel = jnp.zeros((e, tm), jnp.int32)
    for _ in range(TOPK_GROUPS):
        m = grp.max(axis=0, keepdims=True)
        first = jnp.where(grp == m, e_id, e).min(axis=0, keepdims=True)
        chosen = g_id == lax.shift_right_logical(first, shift)
        gsel = jnp.where(chosen, 1, gsel)
        grp = jnp.where(chosen, ninf, grp)
    sel = jnp.where(gsel > 0, sel, NEG)
    picks = []
    for _ in range(TOP_K):
        m = sel.max(axis=0, keepdims=True)
        first = jnp.where(sel == m, e_id, e).min(axis=0, keepdims=True)
        oh = e_id == first
        picks.append(oh)
        sel = jnp.where(oh, ninf, sel)
    return scores, picks


def _route_kernel(lg_ref, bias_ref, dest_ref, gate_ref, cnt_ref, pad_ref, counts_ref, pads_ref, run_ref):
    phase = pl.program_id(0)
    i = pl.program_id(1)
    e, tm = lg_ref.shape
    scores, picks = _route_select(lg_ref[...], bias_ref[...])
    mask = jnp.zeros((e, tm), F32)
    for oh in picks:
        mask = jnp.where(oh, 1.0, mask)

    @pl.when((phase == 0) & (i == 0))
    def _():
        counts_ref[...] = jnp.zeros_like(counts_ref)

    @pl.when(phase == 0)
    def _():
        counts_ref[...] += mask.sum(axis=1, keepdims=True)

    @pl.when((phase == 1) & (i == 0))
    def _():
        counts = counts_ref[...]
        padded = jnp.ceil(counts * np.float32(1.0 / EXPERT_BM)) * np.float32(EXPERT_BM)
        r_i = lax.broadcasted_iota(jnp.int32, (e, e), 0)
        c_i = lax.broadcasted_iota(jnp.int32, (e, e), 1)
        before = (c_i < r_i).astype(F32)
        starts = jnp.dot(before, jnp.broadcast_to(padded, (e, LANES)), preferred_element_type=F32,
                         precision=lax.Precision.HIGHEST)
        pads_ref[...] = starts[:, 0:1]
        run_ref[...] = jnp.zeros_like(run_ref)
        cnt_ref[...] = jnp.broadcast_to(counts, cnt_ref.shape)
        pad_ref[...] = starts

    @pl.when(phase == 1)
    def _():
        r_i = lax.broadcasted_iota(jnp.int32, (tm, tm), 0)
        c_i = lax.broadcasted_iota(jnp.int32, (tm, tm), 1)
        earlier = (r_i < c_i).astype(BF16)
        slot = pads_ref[...] + run_ref[...] + _dot(mask.astype(BF16), earlier)
        run_ref[...] += mask.sum(axis=1, keepdims=True)
        k_i = lax.broadcasted_iota(jnp.int32, (TOP_K, tm), 0)
        dest = jnp.zeros((TOP_K, tm), F32)
        gate = jnp.zeros((TOP_K, tm), F32)
        total = jnp.zeros((1, tm), F32)
        for k, oh in enumerate(picks):
            d_k = jnp.where(oh, slot, 0.0).sum(axis=0, keepdims=True)
            g_k = jnp.where(oh, scores, 0.0).sum(axis=0, keepdims=True)
            total = total + g_k
            dest = jnp.where(k_i == k, d_k, dest)
            gate = jnp.where(k_i == k, g_k, gate)
        dest_ref[...] = dest.astype(jnp.int32)
        gate_ref[...] = gate / total * np.float32(ROUTED_SCALE)


def _route(lg_t, bias):
    e, t = lg_t.shape
    tm = MOE_TM
    n_t = t // tm
    tile = pl.BlockSpec((TOP_K, tm), lambda p, i: (0, i * p))
    meta = pl.BlockSpec((e, LANES), lambda p, i: (0, 0))
    return pl.pallas_call(
        _route_kernel,
        grid=(2, n_t),
        in_specs=[pl.BlockSpec((e, tm), lambda p, i: (0, i)), pl.BlockSpec((e, 1), lambda p, i: (0, 0))],
        out_specs=[tile, tile, meta, meta],
        out_shape=[jax.ShapeDtypeStruct((TOP_K, t), jnp.int32), jax.ShapeDtypeStruct((TOP_K, t), F32),
                   jax.ShapeDtypeStruct((e, LANES), F32), jax.ShapeDtypeStruct((e, LANES), F32)],
        scratch_shapes=[pltpu.VMEM((e, 1), F32), pltpu.VMEM((e, 1), F32), pltpu.VMEM((e, 1), F32)],
        compiler_params=_params("arbitrary", "arbitrary"),
        name="route",
    )(lg_t, bias)


def _row_copy(src, src_row, dst, dst_row, sem):
    return pltpu.make_async_copy(src.at[pl.ds(src_row, 1)], dst.at[pl.ds(dst_row, 1)], sem)


def _scatter_kernel(dest_ref, x_ref, xs_in, xs_out, sem):
    del xs_in
    tm = x_ref.shape[0]

    def issue(t, c):
        for k in range(TOP_K):
            _row_copy(x_ref, t, xs_out, dest_ref[k * tm + t], sem).start(priority=k % 2)
        return c

    def drain(t, c):
        for k in range(TOP_K):
            _row_copy(x_ref, t, xs_out, dest_ref[k * tm + t], sem).wait()
        return c

    lax.fori_loop(0, tm, issue, 0)
    lax.fori_loop(0, tm, drain, 0)


def _scatter(dest_flat, x1p, xs_init):
    t, w = x1p.shape
    tm = MOE_TM
    return pl.pallas_call(
        _scatter_kernel,
        grid=(t // tm,),
        in_specs=[pl.BlockSpec((TOP_K * tm,), lambda i: (i,), memory_space=pltpu.SMEM),
                  pl.BlockSpec((tm, w), lambda i: (i, 0)),
                  pl.BlockSpec(memory_space=pl.ANY)],
        out_specs=pl.BlockSpec(memory_space=pl.ANY),
        out_shape=jax.ShapeDtypeStruct(xs_init.shape, xs_init.dtype),
        scratch_shapes=[pltpu.SemaphoreType.DMA(())],
        input_output_aliases={2: 0},
        compiler_params=_params("arbitrary"),
        name="scatter",
    )(dest_flat, x1p, xs_init)


def _expert_kernel(be_ref, nu_ref, x_ref, wg_ref, wu_ref, wd_ref, y_ref, wgb, wub, wdb):
    i = pl.program_id(0)
    new_expert = (i == 0) | (be_ref[i] != be_ref[jnp.maximum(i - 1, 0)])

    @pl.when((i < nu_ref[0]) & new_expert)
    def _():
        wgb[...] = wg_ref[...].astype(BF16)
        wub[...] = wu_ref[...].astype(BF16)
        wdb[...] = wd_ref[...].astype(BF16)

    @pl.when(i < nu_ref[0])
    def _():
        x = _unpack_bf16_pairs(x_ref[...])
        g = _dot(x, wgb[...])
        h = (g * jax.nn.sigmoid(g)) * _dot(x, wub[...])
        y_ref[...] = _dot(h.astype(BF16), wdb[...])

    @pl.when(i >= nu_ref[0])
    def _():
        y_ref[...] = jnp.zeros_like(y_ref)


def _experts(block_e, n_used, xs, layer, wg, wu, wd):
    n_slots = xs.shape[0]
    n_blocks = n_slots // EXPERT_BM
    ff = wg.shape[3]
    grid_spec = pltpu.PrefetchScalarGridSpec(
        num_scalar_prefetch=2,
        grid=(n_blocks,),
        in_specs=[pl.BlockSpec((EXPERT_BM, D_MODEL // 2), lambda i, be, nu: (i, 0)),
                  pl.BlockSpec((None, None, D_MODEL, ff), lambda i, be, nu: (layer, be[i], 0, 0)),
                  pl.BlockSpec((None, None, D_MODEL, ff), lambda i, be, nu: (layer, be[i], 0, 0)),
                  pl.BlockSpec((None, None, ff, D_MODEL), lambda i, be, nu: (layer, be[i], 0, 0))],
        out_specs=pl.BlockSpec((EXPERT_BM, D_MODEL), lambda i, be, nu: (i, 0)),
        scratch_shapes=[pltpu.VMEM((D_MODEL, ff), BF16), pltpu.VMEM((D_MODEL, ff), BF16),
                        pltpu.VMEM((ff, D_MODEL), BF16)],
    )
    return pl.pallas_call(
        _expert_kernel,
        grid_spec=grid_spec,
        out_shape=jax.ShapeDtypeStruct((n_slots, D_MODEL), F32),
        compiler_params=_params("arbitrary"),
        name="experts",
    )(block_e, n_used, xs, wg, wu, wd)


def _ffn_ln_kernel(dest_ref, xp_ref, x_ref, gate_ref, y_hbm, wg_ref, wu_ref, wd_ref, g_ref, b_ref, o_ref, buf, sem):
    tm = x_ref.shape[0]

    def issue(t, c):
        for k in range(TOP_K):
            _row_copy(y_hbm, dest_ref[k * tm + t], buf.at[k], t, sem).start(priority=k % 2)
        return c

    def drain(t, c):
        for k in range(TOP_K):
            _row_copy(y_hbm, dest_ref[k * tm + t], buf.at[k], t, sem).wait()
        return c

    lax.fori_loop(0, tm, issue, 0)
    xb = _unpack_bf16_pairs(xp_ref[...])
    gte = _dot(xb, wg_ref[...])
    h = (gte * jax.nn.sigmoid(gte)) * _dot(xb, wu_ref[...])
    shared = _dot(h.astype(BF16), wd_ref[...])
    lax.fori_loop(0, tm, drain, 0)
    routed = None
    for k in range(TOP_K):
        part = buf[k] * gate_ref[:, k:k + 1]
        routed = part if routed is None else routed + part
    z = np.float32(ALPHA) * x_ref[...] + (routed + shared)
    o_ref[...] = _layer_norm(z, g_ref[...], b_ref[...])


def _ffn_ln(dest_flat, x1p, x1, gate_t, y, wg, wu, wd, g, b):
    t = x1.shape[0]
    tm = MOE_TM
    row = lambda a: pl.BlockSpec((tm, a.shape[1]), lambda i: (i, 0))
    full = lambda a: pl.BlockSpec(a.shape, lambda i: (0,) * a.ndim)
    return pl.pallas_call(
        _ffn_ln_kernel,
        grid=(t // tm,),
        in_specs=[pl.BlockSpec((TOP_K * tm,), lambda i: (i,), memory_space=pltpu.SMEM),
                  row(x1p), row(x1), row(gate_t), pl.BlockSpec(memory_space=pl.ANY),
                  full(wg), full(wu), full(wd), full(g), full(b)],
        out_specs=row(x1),
        out_shape=jax.ShapeDtypeStruct((t, D_MODEL), F32),
        scratch_shapes=[pltpu.VMEM((TOP_K, tm, D_MODEL), F32), pltpu.SemaphoreType.DMA(())],
        compiler_params=_params("arbitrary"),
        name="ffn_ln",
    )(dest_flat, x1p, x1, gate_t, y, wg, wu, wd, g, b)


def _ple_kernel(xf_ref, xs_ref, p_ref, wp_ref, wg_ref, *o_refs, n_prompt_tiles):
    gate = jax.nn.sigmoid(_dot(xf_ref[...].astype(BF16), wg_ref[...]))
    emb = _dot(p_ref[...].astype(BF16), wp_ref[...])
    out = xs_ref[...] + emb * gate
    if len(o_refs) == 1:
        o_refs[0][...] = out
    else:
        i = pl.program_id(1)

        @pl.when(i < n_prompt_tiles)
        def _():
            o_refs[0][...] = out

        @pl.when(i >= n_prompt_tiles)
        def _():
            o_refs[1][...] = out


def _ple(x2, p, w_ple, w_gate, n_prompt=None):
    t = x2.shape[0]
    tm = _tile(t if n_prompt is None else np.gcd(n_prompt, t - n_prompt), 512)
    tn = _tile(D_MODEL, 1024)
    n_pt = None if n_prompt is None else n_prompt // tm
    if n_prompt is None:
        out_specs = [pl.BlockSpec((tm, tn), lambda n, i: (i, n))]
        out_shape = [jax.ShapeDtypeStruct((t, D_MODEL), F32)]
    else:
        out_specs = [pl.BlockSpec((tm, tn), lambda n, i: (jnp.minimum(i, n_pt - 1), n)),
                     pl.BlockSpec((tm, tn), lambda n, i: (jnp.maximum(i - n_pt, 0), n))]
        out_shape = [jax.ShapeDtypeStruct((n_prompt, D_MODEL), F32),
                     jax.ShapeDtypeStruct((t - n_prompt, D_MODEL), F32)]
    return pl.pallas_call(
        functools.partial(_ple_kernel, n_prompt_tiles=n_pt),
        grid=(D_MODEL // tn, t // tm),
        in_specs=[pl.BlockSpec((tm, D_MODEL), lambda n, i: (i, 0)),
                  pl.BlockSpec((tm, tn), lambda n, i: (i, n)),
                  pl.BlockSpec((tm, p.shape[1]), lambda n, i: (i, 0)),
                  pl.BlockSpec((w_ple.shape[0], tn), lambda n, i: (0, n)),
                  pl.BlockSpec((D_MODEL, tn), lambda n, i: (0, n))],
        out_specs=out_specs, out_shape=out_shape,
        compiler_params=_params("arbitrary", "arbitrary"),
        name="ple",
    )(x2, x2, p, w_ple, w_gate)


def _rel_bias(table, qpos, kpos):
    idx = np.clip(qpos[:, None] - kpos[None, :], -REL_CLIP, REL_CLIP) + REL_CLIP
    return table[:, idx]


def _band_prompt_bias(table):
    tq, width = BAND_TQ, BAND_NWB * BAND_TQ
    span = width + tq - 1
    rel = (BAND_NWB - 1) * tq + tq - 1 - np.arange(span)
    strip = table[:, np.clip(rel, -REL_CLIP, REL_CLIP) + REL_CLIP]
    u = jnp.concatenate([strip[:, tq - 1:], strip[:, :1], strip[:, :tq - 1]], axis=1)
    flat = jnp.tile(u, (1, tq))[:, :tq * span]
    return flat.reshape(table.shape[0], tq, span)[:, :, :width]


def _band_mask(qpos, kpos):
    dc = qpos // CHUNK - kpos // CHUNK
    return (kpos >= 0) & (dc >= 0) & (dc <= B_PREV_CHUNKS)


def _post_block(x, lhs, w_out, p, ln1_g, ln1_b, ln2_g, ln2_b, w_router, b_router,
                layer, w_gate, w_up, w_down, ws_gate, ws_up, ws_down, w_ple, w_ple_gate, name, n_prompt):
    row = lambda a: a.reshape(1, -1)
    t = x.shape[0]
    x1, x1p, lg_t = _outproj_ln(lhs, w_out, x, row(ln1_g), row(ln1_b), w_router.T, name)
    dest, gate, cnt, pad = _route(lg_t, b_router.astype(F32).reshape(N_EXPERTS, 1))

    n_blocks = -(-(t * TOP_K + N_EXPERTS * (EXPERT_BM - 1)) // EXPERT_BM)
    counts = cnt[:, 0].astype(jnp.int32)
    pad_end = pad[:, 0].astype(jnp.int32) + (counts + EXPERT_BM - 1) // EXPERT_BM * EXPERT_BM
    block_e = jnp.minimum((pad_end[:, None] <= jnp.arange(n_blocks)[None, :] * EXPERT_BM).sum(axis=0),
                          N_EXPERTS - 1).astype(jnp.int32)
    n_used = (pad_end[-1:] // EXPERT_BM).astype(jnp.int32)
    dest_flat = dest.reshape(TOP_K, t // MOE_TM, MOE_TM).transpose(1, 0, 2).reshape(-1)

    xs = _scatter(dest_flat, x1p, jnp.zeros((n_blocks * EXPERT_BM, D_MODEL // 2), jnp.uint32))
    y = _experts(block_e, n_used, xs, layer, w_gate, w_up, w_down)
    x2 = _ffn_ln(dest_flat, x1p, x1, gate.T, y, ws_gate, ws_up, ws_down, row(ln2_g), row(ln2_b))
    return _ple(x2, p, w_ple, w_ple_gate, n_prompt)


def kernel(x_prompt, x_sample, cache_b_k, cache_b_v, cache_c_k, cache_c_v, p_prompt, p_sample, w_in_ab, w_out_ab, sgu_w, sgu_b, sgu_ln_g, sgu_ln_b, rel_bias_tab, w_in_c, w_out_c, ln_mix_g, ln_mix_b, ln_ffn_g, ln_ffn_b, w_router, b_router, w_gate, w_up, w_down, ws_gate, ws_up, ws_down, w_ple, w_ple_gate):
    n_p = BATCH * SEQ
    n_s = DEC_BATCH * DEC_SEQ
    x = jnp.concatenate([x_prompt.reshape(n_p, D_MODEL), x_sample.reshape(n_s, D_MODEL)], axis=0)
    p_all = jnp.concatenate([p_prompt.reshape(DEPTH, n_p, PLE_DIM), p_sample.reshape(DEPTH, n_s, PLE_DIM)], axis=1)
    bf = lambda a: a.astype(BF16)
    outs = {k: [] for k in ("bk_p", "bv_p", "bk_s", "bv_s", "av_s", "ck_p", "cv_p", "ck_s", "cv_s")}
    kv_outs = ((F32, True), (BF16, False))

    for i in range(DEPTH):
        j = i // 2
        if i % 2 == 0:
            w_in = w_in_ab[j]
            hu, = _matmul(x, w_in, 0, A_WIDTH, ((F32, False),), "proj_u")
            hv, = _matmul(x, w_in, A_WIDTH, A_WIDTH, ((F32, False),), "proj_v")
            q, = _matmul(x, w_in, 2 * A_WIDTH, B_WIDTH, ((BF16, False),), "proj_q")
            k_p, k_s, kb16 = _matmul(x, w_in, 2 * A_WIDTH + B_WIDTH, B_WIDTH, kv_outs, "proj_k", n_p)
            v_p, v_s, vb16 = _matmul(x, w_in, 2 * A_WIDTH + 2 * B_WIDTH, B_WIDTH, kv_outs, "proj_vv", n_p)

            bs_t = sgu_b[j].T
            a_p, = _sgu(hu, hv, sgu_w[j], bs_t, sgu_ln_g[j], sgu_ln_b[j], A_CHUNK, 0, n_p // A_CHUNK, False)
            a_s, va_s = _sgu(hu, hv, sgu_w[j][:, :DEC_SEQ, :DEC_SEQ], bs_t[:DEC_SEQ], sgu_ln_g[j], sgu_ln_b[j],
                             DEC_SEQ, n_p, DEC_BATCH, True)

            b_p = _band_prompt(q, kb16, vb16, _band_prompt_bias(rel_bias_tab[j]))
            n_cache = cache_b_k.shape[2]
            qs = PAST_LEN + np.arange(DEC_SEQ)
            kc = PAST_LEN - n_cache + np.arange(n_cache)
            b_s = _band_sample(
                q, kb16, vb16, cache_b_k.reshape(N_EVEN, DEC_BATCH, n_cache * B_HEADS, B_DIM),
                cache_b_v.reshape(N_EVEN, DEC_BATCH, n_cache * B_HEADS, B_DIM), j,
                _rel_bias(rel_bias_tab[j], qs, kc), _rel_bias(rel_bias_tab[j], qs, qs),
                jnp.asarray(_band_mask(qs[:, None], kc[None, :]), F32),
                jnp.asarray(_band_mask(qs[:, None], qs[None, :]), F32))

            keep = min(B_WINDOW, SEQ)
            outs["bk_p"].append(k_p.reshape(BATCH, SEQ, B_HEADS, B_DIM)[:, SEQ - keep:])
            outs["bv_p"].append(v_p.reshape(BATCH, SEQ, B_HEADS, B_DIM)[:, SEQ - keep:])
            outs["bk_s"].append(k_s.reshape(DEC_BATCH, DEC_SEQ, B_HEADS, B_DIM))
            outs["bv_s"].append(v_s.reshape(DEC_BATCH, DEC_SEQ, B_HEADS, B_DIM))
            outs["av_s"].append(va_s.reshape(DEC_BATCH, DEC_SEQ, A_HEADS, A_DIM))
            lhs, w_out, name = [(a_p, a_s), (b_p, b_s)], bf(w_out_ab[j]), "outproj_ab"
        else:
            w_in = w_in_c[j]
            q, = _matmul(x, w_in, 0, C_WIDTH, ((BF16, False),), "proj_cq")
            k_p, k_s, kb16 = _matmul(x, w_in, C_WIDTH, C_WIDTH, kv_outs, "proj_ck", n_p)
            v_p, v_s, vb16 = _matmul(x, w_in, 2 * C_WIDTH, C_WIDTH, kv_outs, "proj_cv", n_p)
            o_p = _sb_prompt(q, kb16, vb16)
            rows = lambda c: c.reshape(N_ODD, DEC_BATCH, PAST_LEN * C_HEADS, C_DIM)
            o_s = _sb_sample(q, kb16, vb16, rows(cache_c_k), rows(cache_c_v), j)
            outs["ck_p"].append(k_p.reshape(BATCH, SEQ, C_HEADS, C_DIM))
            outs["cv_p"].append(v_p.reshape(BATCH, SEQ, C_HEADS, C_DIM))
            outs["ck_s"].append(k_s.reshape(DEC_BATCH, DEC_SEQ, C_HEADS, C_DIM))
            outs["cv_s"].append(v_s.reshape(DEC_BATCH, DEC_SEQ, C_HEADS, C_DIM))
            lhs, w_out, name = [(o_p, o_s)], bf(w_out_c[j]), "outproj_c"

        x = _post_block(x, lhs, w_out, p_all[i], ln_mix_g[i], ln_mix_b[i], ln_ffn_g[i], ln_ffn_b[i],
                        w_router[i], b_router[i], i, w_gate, w_up, w_down,
                        bf(ws_gate[i]), bf(ws_up[i]), bf(ws_down[i]), bf(w_ple[i]), bf(w_ple_gate[i]), name,
                        n_p if i == DEPTH - 1 else None)
        if i < DEPTH - 1:
            x, = x

    y_p, y_s = x
    st = lambda key: jnp.stack(outs[key])
    return (y_p.reshape(BATCH, SEQ, D_MODEL), y_s.reshape(DEC_BATCH, DEC_SEQ, D_MODEL),
            st("bk_p"), st("bv_p"), st("bk_s"), st("bv_s"), st("av_s"),
            st("ck_p"), st("cv_p"), st("ck_s"), st("cv_s"))
```

```python
import functools

import numpy as np
import jax
import jax.numpy as jnp
from jax import lax
from jax.experimental import pallas as pl
from jax.experimental.pallas import tpu as pltpu

D_MODEL = 2048
BATCH = 2
SEQ = 4096
DEPTH = 2
DEC_BATCH = 32
DEC_SEQ = 16
PAST_LEN = 2048

CHUNK = 64
N_EVEN = (DEPTH + 1) // 2
N_ODD = DEPTH // 2
A_CHUNK = 128
A_HEADS = 8
A_DIM = 128
A_WIDTH = A_HEADS * A_DIM
B_HEADS = 8
B_DIM = 128
B_WIDTH = B_HEADS * B_DIM
B_PREV_CHUNKS = 8
B_WINDOW = B_PREV_CHUNKS * CHUNK
REL_CLIP = 128
C_HEADS = 16
C_DIM = 128
C_WIDTH = C_HEADS * C_DIM
N_EXPERTS = 64
N_GROUPS = 8
E_PER_GROUP = N_EXPERTS // N_GROUPS
TOPK_GROUPS = 4
TOP_K = 8
EXPERT_FF = 512
SHARED_FF = 512
ROUTED_SCALE = 2.5
PLE_DIM = 256
LN_EPS = 1e-5
ALPHA = (2 * DEPTH) ** 0.25
NEG = -1e9

LANES = 128
VMEM_LIMIT_BYTES = 56 * 1024 * 1024

BF16 = jnp.bfloat16
F32 = jnp.float32


def _params(*sem):
    return pltpu.CompilerParams(dimension_semantics=sem, vmem_limit_bytes=VMEM_LIMIT_BYTES)


def _tile(n, pref):
    if n <= pref:
        return n
    for t in range(pref, 7, -1):
        if n % t == 0 and t % 8 == 0:
            return t
    return n


def _dot(a, b):
    return jnp.dot(a, b, preferred_element_type=F32)


def _dot_nt(a, b):
    return lax.dot_general(a, b, (((1,), (1,)), ((), ())), preferred_element_type=F32)


def _layer_norm(z, g, b):
    mu = jnp.mean(z, axis=-1, keepdims=True)
    zc = z - mu
    var = jnp.mean(zc * zc, axis=-1, keepdims=True)
    return zc * lax.rsqrt(var + LN_EPS) * g + b


def _gelu(x):
    return x * (lax.erf(x * np.float32(1.0 / np.sqrt(2.0))) + 1.0) * 0.5


def _softplus(z):
    return jnp.maximum(z, 0.0) + jnp.log(1.0 + jnp.exp(-jnp.abs(z)))


def _mm_kernel(x_ref, w_ref, *rest, split, n_prompt_tiles):
    o_refs, wb_ref = rest[:-1], rest[-1]
    i = pl.program_id(1)

    @pl.when(i == 0)
    def _():
        wb_ref[...] = w_ref[...].astype(BF16)

    acc = _dot(x_ref[...].astype(BF16), wb_ref[...])
    refs = iter(o_refs)
    for is_split in split:
        if is_split:
            o_p, o_s = next(refs), next(refs)

            @pl.when(i < n_prompt_tiles)
            def _():
                o_p[...] = acc.astype(o_p.dtype)

            @pl.when(i >= n_prompt_tiles)
            def _():
                o_s[...] = acc.astype(o_s.dtype)
        else:
            o = next(refs)
            o[...] = acc.astype(o.dtype)


def _matmul(x, w, col0, ncols, outs, name, n_prompt=None):
    m, k = x.shape
    n_prompt = m if n_prompt is None else n_prompt
    tm = _tile(np.gcd(n_prompt, m - n_prompt) if n_prompt < m else m, 512)
    tn = _tile(ncols, 1024)
    assert col0 % tn == 0 and m % tm == 0 and n_prompt % tm == 0
    nb0 = col0 // tn
    n_pt = n_prompt // tm
    out_specs, out_shape = [], []
    for dtype, is_split in outs:
        if is_split:
            out_specs += [pl.BlockSpec((tm, tn), lambda n, i: (jnp.minimum(i, n_pt - 1), n)),
                          pl.BlockSpec((tm, tn), lambda n, i: (jnp.maximum(i - n_pt, 0), n))]
            out_shape += [jax.ShapeDtypeStruct((n_prompt, ncols), dtype),
                          jax.ShapeDtypeStruct((m - n_prompt, ncols), dtype)]
        else:
            out_specs.append(pl.BlockSpec((tm, tn), lambda n, i: (i, n)))
            out_shape.append(jax.ShapeDtypeStruct((m, ncols), dtype))
    return pl.pallas_call(
        functools.partial(_mm_kernel, split=tuple(s for _, s in outs), n_prompt_tiles=n_pt),
        grid=(ncols // tn, m // tm),
        in_specs=[pl.BlockSpec((tm, k), lambda n, i: (i, 0)),
                  pl.BlockSpec((k, tn), lambda n, i: (0, n + nb0))],
        out_specs=out_specs, out_shape=out_shape,
        scratch_shapes=[pltpu.VMEM((k, tn), BF16)],
        compiler_params=_params("arbitrary", "arbitrary"),
        name=name,
    )(x, w)


def _sgu_kernel(hu_ref, hv_ref, w_ref, bs_ref, g_ref, b_ref, a_ref, *v_out, emit_v):
    rows = hu_ref.shape[0]
    r_i = lax.broadcasted_iota(jnp.int32, (rows, rows), 0)
    c_i = lax.broadcasted_iota(jnp.int32, (rows, rows), 1)
    causal = c_i <= r_i
    for h in range(A_HEADS):
        sl = slice(h * A_DIM, (h + 1) * A_DIM)
        u = _gelu(hu_ref[:, sl])
        v = _layer_norm(_gelu(hv_ref[:, sl]), g_ref[h:h + 1, :], b_ref[h:h + 1, :])
        w = jnp.where(causal, w_ref[h], 0.0).astype(BF16)
        mix = _dot(w, v.astype(BF16)) + bs_ref[:, h:h + 1]
        a_ref[:, sl] = (u * mix).astype(a_ref.dtype)
        if emit_v:
            v_out[0][:, sl] = v


def _sgu(hu, hv, w_s, bs_t, g, b, rows, row0, n_chunks, emit_v):
    off = row0 // rows
    in_spec = pl.BlockSpec((rows, A_WIDTH), lambda c: (c + off, 0))
    out_spec = pl.BlockSpec((rows, A_WIDTH), lambda c: (c, 0))
    full = lambda a: pl.BlockSpec(a.shape, lambda c: (0,) * a.ndim)
    out_specs = [out_spec]
    out_shape = [jax.ShapeDtypeStruct((n_chunks * rows, A_WIDTH), BF16)]
    if emit_v:
        out_specs.append(out_spec)
        out_shape.append(jax.ShapeDtypeStruct((n_chunks * rows, A_WIDTH), F32))
    return pl.pallas_call(
        functools.partial(_sgu_kernel, emit_v=emit_v),
        grid=(n_chunks,),
        in_specs=[in_spec, in_spec, full(w_s), full(bs_t), full(g), full(b)],
        out_specs=out_specs, out_shape=out_shape,
        compiler_params=_params("arbitrary"),
        name="sgu_sample" if emit_v else "sgu_prompt",
    )(hu, hv, w_s, bs_t, g, b)


BAND_TQ = 2 * CHUNK
BAND_NWB = B_WINDOW // BAND_TQ + 1


def _band_prompt_kernel(q_ref, k_ref, v_ref, bias_ref, o_ref):
    t = pl.program_id(2)
    tq = BAND_TQ
    q = q_ref[...]
    shift = int(np.log2(CHUNK))
    q_chunk = lax.shift_right_arithmetic(t * tq + lax.broadcasted_iota(jnp.int32, (tq, 1), 0), shift)
    lane = lax.broadcasted_iota(jnp.int32, (1, tq), 1)
    s_blocks = []
    for j in range(BAND_NWB):
        kb = t - (BAND_NWB - 1) + j
        start = pl.multiple_of(jnp.maximum(kb, 0) * tq, tq)
        s = _dot_nt(q, k_ref[pl.ds(start, tq), :]) * np.float32(B_DIM ** -0.5) + bias_ref[0, :, j * tq:(j + 1) * tq]
        kpos = kb * tq + lane
        dc = q_chunk - lax.shift_right_arithmetic(kpos, shift)
        valid = (kpos >= 0) & (dc >= 0) & (dc <= B_PREV_CHUNKS)
        s_blocks.append(jnp.where(valid, s, NEG))
    m = s_blocks[0].max(axis=-1, keepdims=True)
    for s in s_blocks[1:]:
        m = jnp.maximum(m, s.max(axis=-1, keepdims=True))
    acc = jnp.zeros((tq, B_DIM), F32)
    den = jnp.zeros((tq, 1), F32)
    for j, s in enumerate(s_blocks):
        kb = t - (BAND_NWB - 1) + j
        start = pl.multiple_of(jnp.maximum(kb, 0) * tq, tq)
        p = jnp.exp(s - m)
        den = den + p.sum(axis=-1, keepdims=True)
        acc = acc + _dot(p.astype(BF16), v_ref[pl.ds(start, tq), :])
    o_ref[...] = (acc / den).astype(o_ref.dtype)


def _band_prompt(q, k, v, bias):
    t_all = BATCH * SEQ
    n_t = SEQ // BAND_TQ
    q_spec = pl.BlockSpec((BAND_TQ, B_DIM), lambda b, h, t: (b * n_t + t, h))
    kv_spec = pl.BlockSpec((SEQ, B_DIM), lambda b, h, t: (b, h))
    return pl.pallas_call(
        _band_prompt_kernel,
        grid=(BATCH, B_HEADS, n_t),
        in_specs=[q_spec, kv_spec, kv_spec,
                  pl.BlockSpec((1, BAND_TQ, BAND_NWB * BAND_TQ), lambda b, h, t: (h, 0, 0))],
        out_specs=q_spec,
        out_shape=jax.ShapeDtypeStruct((t_all, B_WIDTH), BF16),
        compiler_params=_params("arbitrary", "arbitrary", "arbitrary"),
        name="band_prompt",
    )(q, k, v, bias)


def _band_sample_kernel(q_ref, kn_ref, vn_ref, kc_ref, vc_ref, bc_ref, bn_ref, mc_ref, mn_ref, o_ref):
    scale = np.float32(B_DIM ** -0.5)
    n_cache = kc_ref.shape[0] // B_HEADS
    for h in range(B_HEADS):
        sl = slice(h * B_DIM, (h + 1) * B_DIM)
        q = q_ref[:, sl]
        s_c = _dot_nt(q, kc_ref[pl.ds(h, n_cache, stride=B_HEADS), :].astype(BF16)) * scale + bc_ref[h]
        s_n = _dot_nt(q, kn_ref[:, sl]) * scale + bn_ref[h]
        s_c = jnp.where(mc_ref[...] > 0, s_c, NEG)
        s_n = jnp.where(mn_ref[...] > 0, s_n, NEG)
        m = jnp.maximum(s_c.max(axis=-1, keepdims=True), s_n.max(axis=-1, keepdims=True))
        p_c = jnp.exp(s_c - m)
        p_n = jnp.exp(s_n - m)
        den = p_c.sum(axis=-1, keepdims=True) + p_n.sum(axis=-1, keepdims=True)
        acc = (_dot(p_c.astype(BF16), vc_ref[pl.ds(h, n_cache, stride=B_HEADS), :].astype(BF16))
               + _dot(p_n.astype(BF16), vn_ref[:, sl]))
        o_ref[:, sl] = (acc / den).astype(o_ref.dtype)


def _band_sample(q, k, v, cache_k, cache_v, layer, bias_c, bias_n, mask_c, mask_n):
    n = DEC_SEQ
    off = BATCH * SEQ // n
    n_cache = cache_k.shape[2] // B_HEADS
    row_spec = pl.BlockSpec((n, B_WIDTH), lambda b: (b + off, 0))
    cache_spec = pl.BlockSpec((None, None, n_cache * B_HEADS, B_DIM), lambda b: (layer, b, 0, 0))
    full = lambda a: pl.BlockSpec(a.shape, lambda b: (0,) * a.ndim)
    return pl.pallas_call(
        _band_sample_kernel,
        grid=(DEC_BATCH,),
        in_specs=[row_spec, row_spec, row_spec, cache_spec, cache_spec,
                  full(bias_c), full(bias_n), full(mask_c), full(mask_n)],
        out_specs=pl.BlockSpec((n, B_WIDTH), lambda b: (b, 0)),
        out_shape=jax.ShapeDtypeStruct((DEC_BATCH * n, B_WIDTH), BF16),
        compiler_params=_params("arbitrary"),
        name="band_sample",
    )(q, k, v, cache_k, cache_v, bias_c, bias_n, mask_c, mask_n)


SB_TQ = 256
SB_TK = 256
SB_SAMPLE_TK = 512
SB_HP = 4


def _sb_weights(z, carry, upper):
    sp = _softplus(z)
    hi = sp.astype(BF16)
    lo = (sp - hi.astype(F32)).astype(BF16)
    later = _dot(hi, upper) + _dot(lo, upper)
    w = jnp.exp((z - sp) - (carry + later))
    return w, carry + later[:, 0:1] + sp[:, 0:1]


def _upper(n):
    r_i = lax.broadcasted_iota(jnp.int32, (n, n), 0)
    c_i = lax.broadcasted_iota(jnp.int32, (n, n), 1)
    return (r_i > c_i).astype(BF16)


SB_ROWS = 32


def _sb_prompt_kernel(q_ref, k_ref, v_ref, o_ref, z_ref, ls_ref, hl_ref, lat_ref, w_ref, carry_ref, acc_ref):
    i = pl.program_id(2)
    tq, tk = SB_TQ, SB_TK
    upper2 = jnp.concatenate([_upper(tk)] * 2, axis=0)
    scale = np.float32(C_DIM ** -0.5)
    chunks = [slice(r, r + SB_ROWS) for r in range(0, tq, SB_ROWS)]
    r_i = lax.broadcasted_iota(jnp.int32, (SB_ROWS, tk), 0)
    c_i = lax.broadcasted_iota(jnp.int32, (SB_ROWS, tk), 1)

    def block(j, diagonal):
        start = pl.multiple_of(j * tk, tk)
        for h in range(SB_HP):
            sl = slice(h * C_DIM, (h + 1) * C_DIM)
            z_ref[h] = _dot_nt(q_ref[:, sl], k_ref[pl.ds(start, tk), sl])
        for h in range(SB_HP):
            for ci, rows in enumerate(chunks):
                z = z_ref[h, rows, :] * scale
                if diagonal:
                    z = jnp.where(c_i < r_i + ci * SB_ROWS, z, NEG)
                sp = _softplus(z)
                hi = sp.astype(BF16)
                ls_ref[h, rows, :] = z - sp
                hl_ref[h, rows, 0:tk] = hi
                hl_ref[h, rows, tk:2 * tk] = (sp - hi.astype(F32)).astype(BF16)
        for h in range(SB_HP):
            lat_ref[h] = _dot(hl_ref[h], upper2)
        for h in range(SB_HP):
            for rows in chunks:
                later = lat_ref[h, rows, :]
                carry = carry_ref[h, rows, :]
                w_ref[h, rows, :] = jnp.exp(ls_ref[h, rows, :] - (carry + later)).astype(BF16)
                sp0 = hl_ref[h, rows, 0:1].astype(F32) + hl_ref[h, rows, tk:tk + 1].astype(F32)
                carry_ref[h, rows, :] = carry + later[:, 0:1] + sp0
        for h in range(SB_HP):
            sl = slice(h * C_DIM, (h + 1) * C_DIM)
            acc_ref[h] += _dot(w_ref[h], v_ref[pl.ds(start, tk), sl])

    carry_ref[...] = jnp.zeros_like(carry_ref)
    acc_ref[...] = jnp.zeros_like(acc_ref)
    block(i, True)

    def body(jj, c):
        block(i - 1 - jj, False)
        return c

    lax.fori_loop(0, i, body, 0)
    for h in range(SB_HP):
        o_ref[:, h * C_DIM:(h + 1) * C_DIM] = acc_ref[h].astype(o_ref.dtype)


def _sb_prompt(q, k, v):
    assert SB_TQ == SB_TK and C_HEADS % SB_HP == 0
    t_all = BATCH * SEQ
    n_q = SEQ // SB_TQ
    q_spec = pl.BlockSpec((SB_TQ, SB_HP * C_DIM), lambda b, h, i: (b * n_q + i, h))
    kv_spec = pl.BlockSpec((SEQ, SB_HP * C_DIM), lambda b, h, i: (b, h))
    return pl.pallas_call(
        _sb_prompt_kernel,
        grid=(BATCH, C_HEADS // SB_HP, n_q),
        in_specs=[q_spec, kv_spec, kv_spec],
        out_specs=q_spec,
        out_shape=jax.ShapeDtypeStruct((t_all, C_WIDTH), BF16),
        scratch_shapes=[pltpu.VMEM((SB_HP, SB_TQ, SB_TK), F32), pltpu.VMEM((SB_HP, SB_TQ, SB_TK), F32),
                        pltpu.VMEM((SB_HP, SB_TQ, 2 * SB_TK), BF16), pltpu.VMEM((SB_HP, SB_TQ, SB_TK), F32),
                        pltpu.VMEM((SB_HP, SB_TQ, SB_TK), BF16), pltpu.VMEM((SB_HP, SB_TQ, 1), F32),
                        pltpu.VMEM((SB_HP, SB_TQ, C_DIM), F32)],
        compiler_params=_params("arbitrary", "arbitrary", "arbitrary"),
        name="sb_prompt",
    )(q, k, v)


def _sb_sample_kernel(q_ref, kn_ref, vn_ref, kc_ref, vc_ref, o_ref, z_ref, w_ref, carry_ref, acc_ref):
    jj = pl.program_id(1)
    n = DEC_SEQ
    tk = kc_ref.shape[0] // C_HEADS
    scale = np.float32(C_DIM ** -0.5)
    col = lambda h: slice(h * C_DIM, (h + 1) * C_DIM)

    def all_heads(k_of, v_of, width, mask):
        for h in range(C_HEADS):
            z_ref[h * n:(h + 1) * n, 0:width] = _dot_nt(q_ref[:, col(h)], k_of(h)) * scale
        z = z_ref[:, 0:width]
        if mask is not None:
            z = jnp.where(mask, z, NEG)
        w, carry = _sb_weights(z, carry_ref[...], _upper(width))
        carry_ref[...] = carry
        w_ref[:, 0:width] = w.astype(BF16)
        for h in range(C_HEADS):
            acc_ref[:, col(h)] += _dot(w_ref[h * n:(h + 1) * n, 0:width], v_of(h))

    @pl.when(jj == 0)
    def _():
        carry_ref[...] = jnp.zeros_like(carry_ref)
        acc_ref[...] = jnp.zeros_like(acc_ref)
        r_i = lax.broadcasted_iota(jnp.int32, (C_HEADS * n, n), 0) % n
        c_i = lax.broadcasted_iota(jnp.int32, (C_HEADS * n, n), 1)
        all_heads(lambda h: kn_ref[:, col(h)], lambda h: vn_ref[:, col(h)], n, c_i < r_i)

    all_heads(lambda h: kc_ref[pl.ds(h, tk, stride=C_HEADS), :].astype(BF16),
              lambda h: vc_ref[pl.ds(h, tk, stride=C_HEADS), :].astype(BF16), tk, None)

    @pl.when(jj == pl.num_programs(1) - 1)
    def _():
        o_ref[...] = acc_ref[...].astype(o_ref.dtype)


def _sb_sample(q, k, v, cache_k, cache_v, layer):
    n = DEC_SEQ
    off = BATCH * SEQ // n
    past = cache_k.shape[2] // C_HEADS
    tk = _tile(past, SB_SAMPLE_TK)
    n_kb = past // tk
    row_spec = pl.BlockSpec((n, C_WIDTH), lambda b, j: (b + off, 0))
    cache_spec = pl.BlockSpec((None, None, tk * C_HEADS, C_DIM), lambda b, j: (layer, b, n_kb - 1 - j, 0))
    return pl.pallas_call(
        _sb_sample_kernel,
        grid=(DEC_BATCH, n_kb),
        in_specs=[row_spec, row_spec, row_spec, cache_spec, cache_spec],
        out_specs=pl.BlockSpec((n, C_WIDTH), lambda b, j: (b, 0)),
        out_shape=jax.ShapeDtypeStruct((DEC_BATCH * n, C_WIDTH), BF16),
        scratch_shapes=[pltpu.VMEM((C_HEADS * n, tk), F32), pltpu.VMEM((C_HEADS * n, tk), BF16),
                        pltpu.VMEM((C_HEADS * n, 1), F32), pltpu.VMEM((n, C_WIDTH), F32)],
        compiler_params=_params("arbitrary", "arbitrary"),
        name="sb_sample",
    )(q, k, v, cache_k, cache_v)


def _pack_bf16_pairs(x):
    half = x.shape[1] // 2
    bits = lax.bitcast_convert_type(x.astype(BF16).astype(F32), jnp.uint32)
    return (bits[:, :half] & jnp.uint32(0xFFFF0000)) | (bits[:, half:] >> 16)


def _unpack_bf16_pairs(w):
    hi = lax.bitcast_convert_type(w & jnp.uint32(0xFFFF0000), F32)
    lo = lax.bitcast_convert_type(w << 16, F32)
    return jnp.concatenate([hi, lo], axis=1).astype(BF16)


def _outproj_kernel(*refs, n_lhs, n_prompt_tiles):
    lhs = refs[:2 * n_lhs]
    w_ref, x_ref, g_ref, b_ref, wr_ref, x1_ref, x1p_ref, lg_ref = refs[2 * n_lhs:]
    is_prompt = pl.program_id(0) < n_prompt_tiles
    k0 = 0
    y = None
    for a_p, a_s in zip(lhs[0::2], lhs[1::2]):
        kk = a_p.shape[1]
        part = _dot(jnp.where(is_prompt, a_p[...], a_s[...]), w_ref[k0:k0 + kk, :])
        y = part if y is None else y + part
        k0 += kk
    x1 = _layer_norm(np.float32(ALPHA) * x_ref[...] + y, g_ref[...], b_ref[...])
    x1_ref[...] = x1
    x1p_ref[...] = _pack_bf16_pairs(x1)
    lg_ref[...] = lax.dot_general(wr_ref[...], x1, (((1,), (1,)), ((), ())), preferred_element_type=F32,
                                  precision=lax.Precision.HIGHEST)


def _outproj_ln(lhs, w, x, g, b, w_router_t, name):
    t = x.shape[0]
    tm = MOE_TM
    n_pt = lhs[0][0].shape[0] // tm
    assert all(a_p.shape[0] == n_pt * tm and a_s.shape[0] == t - n_pt * tm for a_p, a_s in lhs)
    row = lambda a: pl.BlockSpec((tm, a.shape[1]), lambda i: (i, 0))
    full = lambda a: pl.BlockSpec(a.shape, lambda i: (0,) * a.ndim)
    lhs_specs, lhs_args = [], []
    for a_p, a_s in lhs:
        lhs_specs += [pl.BlockSpec((tm, a_p.shape[1]), lambda i: (jnp.minimum(i, n_pt - 1), 0)),
                      pl.BlockSpec((tm, a_s.shape[1]), lambda i: (jnp.maximum(i - n_pt, 0), 0))]
        lhs_args += [a_p, a_s]
    return pl.pallas_call(
        functools.partial(_outproj_kernel, n_lhs=len(lhs), n_prompt_tiles=n_pt),
        grid=(t // tm,),
        in_specs=lhs_specs + [full(w), row(x), full(g), full(b), full(w_router_t)],
        out_specs=[pl.BlockSpec((tm, D_MODEL), lambda i: (i, 0)), pl.BlockSpec((tm, D_MODEL // 2), lambda i: (i, 0)),
                   pl.BlockSpec((N_EXPERTS, tm), lambda i: (0, i))],
        out_shape=[jax.ShapeDtypeStruct((t, D_MODEL), F32), jax.ShapeDtypeStruct((t, D_MODEL // 2), jnp.uint32),
                   jax.ShapeDtypeStruct((N_EXPERTS, t), F32)],
        compiler_params=_params("arbitrary"),
        name=name,
    )(*lhs_args, w, x, g, b, w_router_t)


MOE_TM = 256
EXPERT_BM = 256


def _route_select(lg, bias):
    e, tm = lg.shape
    ninf = np.float32(-np.inf)
    shift = int(np.log2(E_PER_GROUP))
    scores = jax.nn.sigmoid(lg)
    sel = scores + bias
    e_id = lax.broadcasted_iota(jnp.int32, (e, tm), 0)
    g_id = lax.shift_right_logical(e_id, shift)
    g3 = sel.reshape(N_GROUPS, E_PER_GROUP, tm)
    i3 = lax.broadcasted_iota(jnp.int32, g3.shape, 1)
    m1 = g3.max(axis=1, keepdims=True)
    first = jnp.where(g3 == m1, i3, E_PER_GROUP).min(axis=1, keepdims=True)
    m2 = jnp.where(i3 == first, ninf, g3).max(axis=1, keepdims=True)
    grp = jnp.broadcast_to(m1 + m2, g3.shape).reshape(e, tm)
    gsel = jnp.zeros((e, tm), jnp.int32)
    for _ in range(TOPK_GROUPS):
        m = grp.max(axis=0, keepdims=True)
        first = jnp.where(grp == m, e_id, e).min(axis=0, keepdims=True)
        chosen = g_id == lax.shift_right_logical(first, shift)
        gsel = jnp.where(chosen, 1, gsel)
        grp = jnp.where(chosen, ninf, grp)
    sel = jnp.where(gsel > 0, sel, NEG)
    picks = []
    for _ in range(TOP_K):
        m = sel.max(axis=0, keepdims=True)
        first = jnp.where(sel == m, e_id, e).min(axis=0, keepdims=True)
        oh = e_id == first
        picks.append(oh)
        sel = jnp.where(oh, ninf, sel)
    return scores, picks


def _route_kernel(lg_ref, bias_ref, dest_ref, gate_ref, cnt_ref, pad_ref, counts_ref, pads_ref, run_ref):
    phase = pl.program_id(0)
    i = pl.program_id(1)
    e, tm = lg_ref.shape
    scores, picks = _route_select(lg_ref[...], bias_ref[...])
    mask = jnp.zeros((e, tm), F32)
    for oh in picks:
        mask = jnp.where(oh, 1.0, mask)

    @pl.when((phase == 0) & (i == 0))
    def _():
        counts_ref[...] = jnp.zeros_like(counts_ref)

    @pl.when(phase == 0)
    def _():
        counts_ref[...] += mask.sum(axis=1, keepdims=True)

    @pl.when((phase == 1) & (i == 0))
    def _():
        counts = counts_ref[...]
        padded = jnp.ceil(counts * np.float32(1.0 / EXPERT_BM)) * np.float32(EXPERT_BM)
        r_i = lax.broadcasted_iota(jnp.int32, (e, e), 0)
        c_i = lax.broadcasted_iota(jnp.int32, (e, e), 1)
        before = (c_i < r_i).astype(F32)
        starts = jnp.dot(before, jnp.broadcast_to(padded, (e, LANES)), preferred_element_type=F32,
                         precision=lax.Precision.HIGHEST)
        pads_ref[...] = starts[:, 0:1]
        run_ref[...] = jnp.zeros_like(run_ref)
        cnt_ref[...] = jnp.broadcast_to(counts, cnt_ref.shape)
        pad_ref[...] = starts

    @pl.when(phase == 1)
    def _():
        r_i = lax.broadcasted_iota(jnp.int32, (tm, tm), 0)
        c_i = lax.broadcasted_iota(jnp.int32, (tm, tm), 1)
        earlier = (r_i < c_i).astype(BF16)
        slot = pads_ref[...] + run_ref[...] + _dot(mask.astype(BF16), earlier)
        run_ref[...] += mask.sum(axis=1, keepdims=True)
        k_i = lax.broadcasted_iota(jnp.int32, (TOP_K, tm), 0)
        dest = jnp.zeros((TOP_K, tm), F32)
        gate = jnp.zeros((TOP_K, tm), F32)
        total = jnp.zeros((1, tm), F32)
        for k, oh in enumerate(picks):
            d_k = jnp.where(oh, slot, 0.0).sum(axis=0, keepdims=True)
            g_k = jnp.where(oh, scores, 0.0).sum(axis=0, keepdims=True)
            total = total + g_k
            dest = jnp.where(k_i == k, d_k, dest)
            gate = jnp.where(k_i == k, g_k, gate)
        dest_ref[...] = dest.astype(jnp.int32)
        gate_ref[...] = gate / total * np.float32(ROUTED_SCALE)


def _route(lg_t, bias):
    e, t = lg_t.shape
    tm = MOE_TM
    n_t = t // tm
    tile = pl.BlockSpec((TOP_K, tm), lambda p, i: (0, i * p))
    meta = pl.BlockSpec((e, LANES), lambda p, i: (0, 0))
    return pl.pallas_call(
        _route_kernel,
        grid=(2, n_t),
        in_specs=[pl.BlockSpec((e, tm), lambda p, i: (0, i)), pl.BlockSpec((e, 1), lambda p, i: (0, 0))],
        out_specs=[tile, tile, meta, meta],
        out_shape=[jax.ShapeDtypeStruct((TOP_K, t), jnp.int32), jax.ShapeDtypeStruct((TOP_K, t), F32),
                   jax.ShapeDtypeStruct((e, LANES), F32), jax.ShapeDtypeStruct((e, LANES), F32)],
        scratch_shapes=[pltpu.VMEM((e, 1), F32), pltpu.VMEM((e, 1), F32), pltpu.VMEM((e, 1), F32)],
        compiler_params=_params("arbitrary", "arbitrary"),
        name="route",
    )(lg_t, bias)


def _row_copy(src, src_row, dst, dst_row, sem):
    return pltpu.make_async_copy(src.at[pl.ds(src_row, 1)], dst.at[pl.ds(dst_row, 1)], sem)


def _scatter_kernel(last_ref, nu_ref, dest_ref, x_ref, xs_out, zeros_ref, sem, zsem):
    i = pl.program_id(0)
    tm = x_ref.shape[0]
    n_blocks = xs_out.shape[0] // EXPERT_BM

    @pl.when(i == 0)
    def _():
        zeros_ref[...] = jnp.zeros_like(zeros_ref)

        def zero_copy(row):
            return pltpu.make_async_copy(zeros_ref, xs_out.at[pl.ds(pl.multiple_of(row, EXPERT_BM), EXPERT_BM)], zsem)

        def tail_row(b):
            return (nu_ref[0] + b) * EXPERT_BM

        def each(fn, n, act):
            def body(b, c):
                row = fn(b)

                @pl.when(row >= 0)
                def _():
                    act(zero_copy(row))
                return c
            lax.fori_loop(0, n, body, 0)

        start = lambda fn, n: each(fn, n, lambda c: c.start())
        wait = lambda fn, n: each(fn, n, lambda c: c.wait())

        start(lambda e: last_ref[e], N_EXPERTS)
        wait(lambda e: last_ref[e], N_EXPERTS)
        start(tail_row, n_blocks - nu_ref[0])
        wait(tail_row, n_blocks - nu_ref[0])

    def issue(t, c):
        for k in range(TOP_K):
            _row_copy(x_ref, t, xs_out, dest_ref[k * tm + t], sem).start(priority=k % 2)
        return c

    def drain(t, c):
        for k in range(TOP_K):
            _row_copy(x_ref, t, xs_out, dest_ref[k * tm + t], sem).wait()
        return c

    lax.fori_loop(0, tm, issue, 0)
    lax.fori_loop(0, tm, drain, 0)


def _scatter(last_row, n_used, dest_flat, x1p, n_slots):
    t, w = x1p.shape
    tm = MOE_TM
    grid_spec = pltpu.PrefetchScalarGridSpec(
        num_scalar_prefetch=2,
        grid=(t // tm,),
        in_specs=[pl.BlockSpec((TOP_K * tm,), lambda i, lr, nu: (i,), memory_space=pltpu.SMEM),
                  pl.BlockSpec((tm, w), lambda i, lr, nu: (i, 0))],
        out_specs=pl.BlockSpec(memory_space=pl.ANY),
        scratch_shapes=[pltpu.VMEM((EXPERT_BM, w), x1p.dtype), pltpu.SemaphoreType.DMA(()),
                        pltpu.SemaphoreType.DMA(())],
    )
    return pl.pallas_call(
        _scatter_kernel,
        grid_spec=grid_spec,
        out_shape=jax.ShapeDtypeStruct((n_slots, w), x1p.dtype),
        compiler_params=_params("arbitrary"),
        name="scatter",
    )(last_row, n_used, dest_flat, x1p)


def _expert_kernel(be_ref, nu_ref, slot_ref, nxt_ref, x_ref, wg_hbm, wu_hbm, wd_hbm, y_ref,
                   wgf, wuf, wdf, wgb, wub, wdb, sem, *, layer):
    i = pl.program_id(0)
    active = i < nu_ref[0]
    first = (i == 0) | (be_ref[i] != be_ref[jnp.maximum(i - 1, 0)])

    def weight_copies(e, s):
        return [pltpu.make_async_copy(src.at[layer, e], dst.at[s], sem.at[s])
                for src, dst in ((wg_hbm, wgf), (wu_hbm, wuf), (wd_hbm, wdf))]

    @pl.when(active & (i == 0))
    def _():
        for c in weight_copies(be_ref[0], slot_ref[0]):
            c.start()

    @pl.when(active & first)
    def _():
        s = slot_ref[i]
        for c in weight_copies(be_ref[i], s):
            c.wait()
        wgb[...] = wgf[s].astype(BF16)
        wub[...] = wuf[s].astype(BF16)
        wdb[...] = wdf[s].astype(BF16)

        @pl.when(nxt_ref[i] >= 0)
        def _():
            for c in weight_copies(nxt_ref[i], 1 - s):
                c.start()

    @pl.when(active)
    def _():
        x = _unpack_bf16_pairs(x_ref[...])
        g = _dot(x, wgb[...])
        h = (g * jax.nn.sigmoid(g)) * _dot(x, wub[...])
        y_ref[...] = _dot(h.astype(BF16), wdb[...])

    @pl.when(jnp.logical_not(active))
    def _():
        y_ref[...] = jnp.zeros_like(y_ref)


def _experts(block_e, n_used, slot, nxt, xs, layer, wg, wu, wd):
    n_slots = xs.shape[0]
    n_blocks = n_slots // EXPERT_BM
    ff = wg.shape[3]
    any_spec = pl.BlockSpec(memory_space=pl.ANY)
    grid_spec = pltpu.PrefetchScalarGridSpec(
        num_scalar_prefetch=4,
        grid=(n_blocks,),
        in_specs=[pl.BlockSpec((EXPERT_BM, D_MODEL // 2), lambda i, be, nu, sl, nx: (jnp.minimum(i, nu[0] - 1), 0)),
                  any_spec, any_spec, any_spec],
        out_specs=pl.BlockSpec((EXPERT_BM, D_MODEL), lambda i, be, nu, sl, nx: (i, 0)),
        scratch_shapes=[pltpu.VMEM((2, D_MODEL, ff), F32), pltpu.VMEM((2, D_MODEL, ff), F32),
                        pltpu.VMEM((2, ff, D_MODEL), F32),
                        pltpu.VMEM((D_MODEL, ff), BF16), pltpu.VMEM((D_MODEL, ff), BF16),
                        pltpu.VMEM((ff, D_MODEL), BF16), pltpu.SemaphoreType.DMA((2,))],
    )
    return pl.pallas_call(
        functools.partial(_expert_kernel, layer=layer),
        grid_spec=grid_spec,
        out_shape=jax.ShapeDtypeStruct((n_slots, D_MODEL), F32),
        compiler_params=_params("arbitrary"),
        name="experts",
    )(block_e, n_used, slot, nxt, xs, wg, wu, wd)


def _ffn_ln_kernel(dest_ref, xp_ref, x_ref, gate_ref, y_hbm, wg_ref, wu_ref, wd_ref, g_ref, b_ref, o_ref, buf, sem):
    tm = x_ref.shape[0]

    def issue(t, c):
        for k in range(TOP_K):
            _row_copy(y_hbm, dest_ref[k * tm + t], buf.at[k], t, sem).start(priority=k % 2)
        return c

    def drain(t, c):
        for k in range(TOP_K):
            _row_copy(y_hbm, dest_ref[k * tm + t], buf.at[k], t, sem).wait()
        return c

    lax.fori_loop(0, tm, issue, 0)
    xb = _unpack_bf16_pairs(xp_ref[...])
    gte = _dot(xb, wg_ref[...])
    h = (gte * jax.nn.sigmoid(gte)) * _dot(xb, wu_ref[...])
    shared = _dot(h.astype(BF16), wd_ref[...])
    lax.fori_loop(0, tm, drain, 0)
    routed = None
    for k in range(TOP_K):
        part = buf[k] * gate_ref[:, k:k + 1]
        routed = part if routed is None else routed + part
    z = np.float32(ALPHA) * x_ref[...] + (routed + shared)
    o_ref[...] = _layer_norm(z, g_ref[...], b_ref[...])


def _ffn_ln(dest_flat, x1p, x1, gate_t, y, wg, wu, wd, g, b):
    t = x1.shape[0]
    tm = MOE_TM
    row = lambda a: pl.BlockSpec((tm, a.shape[1]), lambda i: (i, 0))
    full = lambda a: pl.BlockSpec(a.shape, lambda i: (0,) * a.ndim)
    return pl.pallas_call(
        _ffn_ln_kernel,
        grid=(t // tm,),
        in_specs=[pl.BlockSpec((TOP_K * tm,), lambda i: (i,), memory_space=pltpu.SMEM),
                  row(x1p), row(x1), row(gate_t), pl.BlockSpec(memory_space=pl.ANY),
                  full(wg), full(wu), full(wd), full(g), full(b)],
        out_specs=row(x1),
        out_shape=jax.ShapeDtypeStruct((t, D_MODEL), F32),
        scratch_shapes=[pltpu.VMEM((TOP_K, tm, D_MODEL), F32), pltpu.SemaphoreType.DMA(())],
        compiler_params=_params("arbitrary"),
        name="ffn_ln",
    )(dest_flat, x1p, x1, gate_t, y, wg, wu, wd, g, b)


def _ple_kernel(xf_ref, xs_ref, p_ref, wp_ref, wg_ref, *o_refs, n_prompt_tiles):
    gate = jax.nn.sigmoid(_dot(xf_ref[...].astype(BF16), wg_ref[...]))
    emb = _dot(p_ref[...].astype(BF16), wp_ref[...])
    out = xs_ref[...] + emb * gate
    if len(o_refs) == 1:
        o_refs[0][...] = out
    else:
        i = pl.program_id(1)

        @pl.when(i < n_prompt_tiles)
        def _():
            o_refs[0][...] = out

        @pl.when(i >= n_prompt_tiles)
        def _():
            o_refs[1][...] = out


def _ple(x2, p, w_ple, w_gate, n_prompt=None):
    t = x2.shape[0]
    tm = _tile(t if n_prompt is None else np.gcd(n_prompt, t - n_prompt), 512)
    tn = _tile(D_MODEL, 1024)
    n_pt = None if n_prompt is None else n_prompt // tm
    if n_prompt is None:
        out_specs = [pl.BlockSpec((tm, tn), lambda n, i: (i, n))]
        out_shape = [jax.ShapeDtypeStruct((t, D_MODEL), F32)]
    else:
        out_specs = [pl.BlockSpec((tm, tn), lambda n, i: (jnp.minimum(i, n_pt - 1), n)),
                     pl.BlockSpec((tm, tn), lambda n, i: (jnp.maximum(i - n_pt, 0), n))]
        out_shape = [jax.ShapeDtypeStruct((n_prompt, D_MODEL), F32),
                     jax.ShapeDtypeStruct((t - n_prompt, D_MODEL), F32)]
    return pl.pallas_call(
        functools.partial(_ple_kernel, n_prompt_tiles=n_pt),
        grid=(D_MODEL // tn, t // tm),
        in_specs=[pl.BlockSpec((tm, D_MODEL), lambda n, i: (i, 0)),
                  pl.BlockSpec((tm, tn), lambda n, i: (i, n)),
                  pl.BlockSpec((tm, p.shape[1]), lambda n, i: (i, 0)),
                  pl.BlockSpec((w_ple.shape[0], tn), lambda n, i: (0, n)),
                  pl.BlockSpec((D_MODEL, tn), lambda n, i: (0, n))],
        out_specs=out_specs, out_shape=out_shape,
        compiler_params=_params("arbitrary", "arbitrary"),
        name="ple",
    )(x2, x2, p, w_ple, w_gate)


def _rel_bias(table, qpos, kpos):
    idx = np.clip(qpos[:, None] - kpos[None, :], -REL_CLIP, REL_CLIP) + REL_CLIP
    return table[:, idx]


def _band_prompt_bias(table):
    tq, width = BAND_TQ, BAND_NWB * BAND_TQ
    span = width + tq - 1
    rel = (BAND_NWB - 1) * tq + tq - 1 - np.arange(span)
    strip = table[:, np.clip(rel, -REL_CLIP, REL_CLIP) + REL_CLIP]
    u = jnp.concatenate([strip[:, tq - 1:], strip[:, :1], strip[:, :tq - 1]], axis=1)
    flat = jnp.tile(u, (1, tq))[:, :tq * span]
    return flat.reshape(table.shape[0], tq, span)[:, :, :width]


def _band_mask(qpos, kpos):
    dc = qpos // CHUNK - kpos // CHUNK
    return (kpos >= 0) & (dc >= 0) & (dc <= B_PREV_CHUNKS)


def _post_block(x, lhs, w_out, p, ln1_g, ln1_b, ln2_g, ln2_b, w_router, b_router,
                layer, w_gate, w_up, w_down, ws_gate, ws_up, ws_down, w_ple, w_ple_gate, name, n_prompt):
    row = lambda a: a.reshape(1, -1)
    t = x.shape[0]
    x1, x1p, lg_t = _outproj_ln(lhs, w_out, x, row(ln1_g), row(ln1_b), w_router.T, name)
    dest, gate, cnt, pad = _route(lg_t, b_router.astype(F32).reshape(N_EXPERTS, 1))

    n_blocks = -(-(t * TOP_K + N_EXPERTS * (EXPERT_BM - 1)) // EXPERT_BM)
    counts = cnt[:, 0].astype(jnp.int32)
    pad_end = pad[:, 0].astype(jnp.int32) + (counts + EXPERT_BM - 1) // EXPERT_BM * EXPERT_BM
    block_e = jnp.minimum((pad_end[:, None] <= jnp.arange(n_blocks)[None, :] * EXPERT_BM).sum(axis=0),
                          N_EXPERTS - 1).astype(jnp.int32)
    n_used = (pad_end[-1:] // EXPERT_BM).astype(jnp.int32)
    nonempty = counts > 0
    e_ids = jnp.arange(N_EXPERTS)
    later_ids = jnp.where(nonempty[None, :] & (e_ids[None, :] > e_ids[:, None]), e_ids[None, :], N_EXPERTS).min(axis=1)
    nxt = jnp.where(later_ids < N_EXPERTS, later_ids, -1).astype(jnp.int32)[block_e]
    slot = ((jnp.cumsum(nonempty) - 1) % 2).astype(jnp.int32)[block_e]
    dest_flat = dest.reshape(TOP_K, t // MOE_TM, MOE_TM).transpose(1, 0, 2).reshape(-1)

    last_row = jnp.where(counts > 0, pad_end - EXPERT_BM, -1).astype(jnp.int32)
    xs = _scatter(last_row, n_used, dest_flat, x1p, n_blocks * EXPERT_BM)
    y = _experts(block_e, n_used, slot, nxt, xs, layer, w_gate, w_up, w_down)
    x2 = _ffn_ln(dest_flat, x1p, x1, gate.T, y, ws_gate, ws_up, ws_down, row(ln2_g), row(ln2_b))
    return _ple(x2, p, w_ple, w_ple_gate, n_prompt)


def kernel(x_prompt, x_sample, cache_b_k, cache_b_v, cache_c_k, cache_c_v, p_prompt, p_sample, w_in_ab, w_out_ab, sgu_w, sgu_b, sgu_ln_g, sgu_ln_b, rel_bias_tab, w_in_c, w_out_c, ln_mix_g, ln_mix_b, ln_ffn_g, ln_ffn_b, w_router, b_router, w_gate, w_up, w_down, ws_gate, ws_up, ws_down, w_ple, w_ple_gate):
    n_p = BATCH * SEQ
    n_s = DEC_BATCH * DEC_SEQ
    x = jnp.concatenate([x_prompt.reshape(n_p, D_MODEL), x_sample.reshape(n_s, D_MODEL)], axis=0)
    p_all = jnp.concatenate([p_prompt.reshape(DEPTH, n_p, PLE_DIM), p_sample.reshape(DEPTH, n_s, PLE_DIM)], axis=1)
    bf = lambda a: a.astype(BF16)
    outs = {k: [] for k in ("bk_p", "bv_p", "bk_s", "bv_s", "av_s", "ck_p", "cv_p", "ck_s", "cv_s")}
    kv_outs = ((F32, True), (BF16, False))

    for i in range(DEPTH):
        j = i // 2
        if i % 2 == 0:
            w_in = w_in_ab[j]
            hu, = _matmul(x, w_in, 0, A_WIDTH, ((F32, False),), "proj_u")
            hv, = _matmul(x, w_in, A_WIDTH, A_WIDTH, ((F32, False),), "proj_v")
            q, = _matmul(x, w_in, 2 * A_WIDTH, B_WIDTH, ((BF16, False),), "proj_q")
            k_p, k_s, kb16 = _matmul(x, w_in, 2 * A_WIDTH + B_WIDTH, B_WIDTH, kv_outs, "proj_k", n_p)
            v_p, v_s, vb16 = _matmul(x, w_in, 2 * A_WIDTH + 2 * B_WIDTH, B_WIDTH, kv_outs, "proj_vv", n_p)

            bs_t = sgu_b[j].T
            a_p, = _sgu(hu, hv, sgu_w[j], bs_t, sgu_ln_g[j], sgu_ln_b[j], A_CHUNK, 0, n_p // A_CHUNK, False)
            a_s, va_s = _sgu(hu, hv, sgu_w[j][:, :DEC_SEQ, :DEC_SEQ], bs_t[:DEC_SEQ], sgu_ln_g[j], sgu_ln_b[j],
                             DEC_SEQ, n_p, DEC_BATCH, True)

            b_p = _band_prompt(q, kb16, vb16, _band_prompt_bias(rel_bias_tab[j]))
            n_cache = cache_b_k.shape[2]
            qs = PAST_LEN + np.arange(DEC_SEQ)
            kc = PAST_LEN - n_cache + np.arange(n_cache)
            b_s = _band_sample(
                q, kb16, vb16, cache_b_k.reshape(N_EVEN, DEC_BATCH, n_cache * B_HEADS, B_DIM),
                cache_b_v.reshape(N_EVEN, DEC_BATCH, n_cache * B_HEADS, B_DIM), j,
                _rel_bias(rel_bias_tab[j], qs, kc), _rel_bias(rel_bias_tab[j], qs, qs),
                jnp.asarray(_band_mask(qs[:, None], kc[None, :]), F32),
                jnp.asarray(_band_mask(qs[:, None], qs[None, :]), F32))

            keep = min(B_WINDOW, SEQ)
            outs["bk_p"].append(k_p.reshape(BATCH, SEQ, B_HEADS, B_DIM)[:, SEQ - keep:])
            outs["bv_p"].append(v_p.reshape(BATCH, SEQ, B_HEADS, B_DIM)[:, SEQ - keep:])
            outs["bk_s"].append(k_s.reshape(DEC_BATCH, DEC_SEQ, B_HEADS, B_DIM))
            outs["bv_s"].append(v_s.reshape(DEC_BATCH, DEC_SEQ, B_HEADS, B_DIM))
            outs["av_s"].append(va_s.reshape(DEC_BATCH, DEC_SEQ, A_HEADS, A_DIM))
            lhs, w_out, name = [(a_p, a_s), (b_p, b_s)], bf(w_out_ab[j]), "outproj_ab"
        else:
            w_in = w_in_c[j]
            q, = _matmul(x, w_in, 0, C_WIDTH, ((BF16, False),), "proj_cq")
            k_p, k_s, kb16 = _matmul(x, w_in, C_WIDTH, C_WIDTH, kv_outs, "proj_ck", n_p)
            v_p, v_s, vb16 = _matmul(x, w_in, 2 * C_WIDTH, C_WIDTH, kv_outs, "proj_cv", n_p)
            o_p = _sb_prompt(q, kb16, vb16)
            rows = lambda c: c.reshape(N_ODD, DEC_BATCH, PAST_LEN * C_HEADS, C_DIM)
            o_s = _sb_sample(q, kb16, vb16, rows(cache_c_k), rows(cache_c_v), j)
            outs["ck_p"].append(k_p.reshape(BATCH, SEQ, C_HEADS, C_DIM))
            outs["cv_p"].append(v_p.reshape(BATCH, SEQ, C_HEADS, C_DIM))
            outs["ck_s"].append(k_s.reshape(DEC_BATCH, DEC_SEQ, C_HEADS, C_DIM))
            outs["cv_s"].append(v_s.reshape(DEC_BATCH, DEC_SEQ, C_HEADS, C_DIM))
            lhs, w_out, name = [(o_p, o_s)], bf(w_out_c[j]), "outproj_c"

        x = _post_block(x, lhs, w_out, p_all[i], ln_mix_g[i], ln_mix_b[i], ln_ffn_g[i], ln_ffn_b[i],
                        w_router[i], b_router[i], i, w_gate, w_up, w_down,
                        bf(ws_gate[i]), bf(ws_up[i]), bf(ws_down[i]), bf(w_ple[i]), bf(w_ple_gate[i]), name,
                        n_p if i == DEPTH - 1 else None)
        if i < DEPTH - 1:
            x, = x

    y_p, y_s = x
    st = lambda key: jnp.stack(outs[key])
    return (y_p.reshape(BATCH, SEQ, D_MODEL), y_s.reshape(DEC_BATCH, DEC_SEQ, D_MODEL),
            st("bk_p"), st("bv_p"), st("bk_s"), st("bv_s"), st("av_s"),
            st("ck_p"), st("cv_p"), st("ck_s"), st("cv_s"))
```

```python
import functools

import numpy as np
import jax
import jax.numpy as jnp
from jax import lax
from jax.experimental import pallas as pl
from jax.experimental.pallas import tpu as pltpu

D_MODEL = 2048
BATCH = 2
SEQ = 4096
DEPTH = 2
DEC_BATCH = 32
DEC_SEQ = 16
PAST_LEN = 2048

CHUNK = 64
N_EVEN = (DEPTH + 1) // 2
N_ODD = DEPTH // 2
A_CHUNK = 128
A_HEADS = 8
A_DIM = 128
A_WIDTH = A_HEADS * A_DIM
B_HEADS = 8
B_DIM = 128
B_WIDTH = B_HEADS * B_DIM
B_PREV_CHUNKS = 8
B_WINDOW = B_PREV_CHUNKS * CHUNK
REL_CLIP = 128
C_HEADS = 16
C_DIM = 128
C_WIDTH = C_HEADS * C_DIM
N_EXPERTS = 64
N_GROUPS = 8
E_PER_GROUP = N_EXPERTS // N_GROUPS
TOPK_GROUPS = 4
TOP_K = 8
EXPERT_FF = 512
SHARED_FF = 512
ROUTED_SCALE = 2.5
PLE_DIM = 256
LN_EPS = 1e-5
ALPHA = (2 * DEPTH) ** 0.25
NEG = -1e9

LANES = 128
VMEM_LIMIT_BYTES = 56 * 1024 * 1024

BF16 = jnp.bfloat16
F32 = jnp.float32


def _params(*sem):
    return pltpu.CompilerParams(dimension_semantics=sem, vmem_limit_bytes=VMEM_LIMIT_BYTES)


def _tile(n, pref):
    if n <= pref:
        return n
    for t in range(pref, 7, -1):
        if n % t == 0 and t % 8 == 0:
            return t
    return n


def _dot(a, b):
    return jnp.dot(a, b, preferred_element_type=F32)


def _dot_nt(a, b):
    return lax.dot_general(a, b, (((1,), (1,)), ((), ())), preferred_element_type=F32)


def _layer_norm(z, g, b):
    mu = jnp.mean(z, axis=-1, keepdims=True)
    zc = z - mu
    var = jnp.mean(zc * zc, axis=-1, keepdims=True)
    return zc * lax.rsqrt(var + LN_EPS) * g + b


def _gelu(x):
    return x * (lax.erf(x * np.float32(1.0 / np.sqrt(2.0))) + 1.0) * 0.5


def _softplus(z):
    return jnp.maximum(z, 0.0) + jnp.log(1.0 + jnp.exp(-jnp.abs(z)))


def _mm_kernel(x_ref, w_ref, *rest, split, n_prompt_tiles):
    o_refs, wb_ref = rest[:-1], rest[-1]
    i = pl.program_id(1)

    @pl.when(i == 0)
    def _():
        wb_ref[...] = w_ref[...].astype(BF16)

    acc = _dot(x_ref[...].astype(BF16), wb_ref[...])
    refs = iter(o_refs)
    for is_split in split:
        if is_split:
            o_p, o_s = next(refs), next(refs)

            @pl.when(i < n_prompt_tiles)
            def _():
                o_p[...] = acc.astype(o_p.dtype)

            @pl.when(i >= n_prompt_tiles)
            def _():
                o_s[...] = acc.astype(o_s.dtype)
        else:
            o = next(refs)
            o[...] = acc.astype(o.dtype)


def _matmul(x, w, col0, ncols, outs, name, n_prompt=None):
    m, k = x.shape
    n_prompt = m if n_prompt is None else n_prompt
    tm = _tile(np.gcd(n_prompt, m - n_prompt) if n_prompt < m else m, 512)
    tn = _tile(ncols, 1024)
    assert col0 % tn == 0 and m % tm == 0 and n_prompt % tm == 0
    nb0 = col0 // tn
    n_pt = n_prompt // tm
    out_specs, out_shape = [], []
    for dtype, is_split in outs:
        if is_split:
            out_specs += [pl.BlockSpec((tm, tn), lambda n, i: (jnp.minimum(i, n_pt - 1), n)),
                          pl.BlockSpec((tm, tn), lambda n, i: (jnp.maximum(i - n_pt, 0), n))]
            out_shape += [jax.ShapeDtypeStruct((n_prompt, ncols), dtype),
                          jax.ShapeDtypeStruct((m - n_prompt, ncols), dtype)]
        else:
            out_specs.append(pl.BlockSpec((tm, tn), lambda n, i: (i, n)))
            out_shape.append(jax.ShapeDtypeStruct((m, ncols), dtype))
    return pl.pallas_call(
        functools.partial(_mm_kernel, split=tuple(s for _, s in outs), n_prompt_tiles=n_pt),
        grid=(ncols // tn, m // tm),
        in_specs=[pl.BlockSpec((tm, k), lambda n, i: (i, 0)),
                  pl.BlockSpec((k, tn), lambda n, i: (0, n + nb0))],
        out_specs=out_specs, out_shape=out_shape,
        scratch_shapes=[pltpu.VMEM((k, tn), BF16)],
        compiler_params=_params("arbitrary", "arbitrary"),
        name=name,
    )(x, w)


def _sgu_kernel(hu_ref, hv_ref, w_ref, bs_ref, g_ref, b_ref, a_ref, *v_out, emit_v):
    rows = hu_ref.shape[0]
    r_i = lax.broadcasted_iota(jnp.int32, (rows, rows), 0)
    c_i = lax.broadcasted_iota(jnp.int32, (rows, rows), 1)
    causal = c_i <= r_i
    for h in range(A_HEADS):
        sl = slice(h * A_DIM, (h + 1) * A_DIM)
        u = _gelu(hu_ref[:, sl])
        v = _layer_norm(_gelu(hv_ref[:, sl]), g_ref[h:h + 1, :], b_ref[h:h + 1, :])
        w = jnp.where(causal, w_ref[h], 0.0).astype(BF16)
        mix = _dot(w, v.astype(BF16)) + bs_ref[:, h:h + 1]
        a_ref[:, sl] = (u * mix).astype(a_ref.dtype)
        if emit_v:
            v_out[0][:, sl] = v


def _sgu(hu, hv, w_s, bs_t, g, b, rows, row0, n_chunks, emit_v):
    off = row0 // rows
    in_spec = pl.BlockSpec((rows, A_WIDTH), lambda c: (c + off, 0))
    out_spec = pl.BlockSpec((rows, A_WIDTH), lambda c: (c, 0))
    full = lambda a: pl.BlockSpec(a.shape, lambda c: (0,) * a.ndim)
    out_specs = [out_spec]
    out_shape = [jax.ShapeDtypeStruct((n_chunks * rows, A_WIDTH), BF16)]
    if emit_v:
        out_specs.append(out_spec)
        out_shape.append(jax.ShapeDtypeStruct((n_chunks * rows, A_WIDTH), F32))
    return pl.pallas_call(
        functools.partial(_sgu_kernel, emit_v=emit_v),
        grid=(n_chunks,),
        in_specs=[in_spec, in_spec, full(w_s), full(bs_t), full(g), full(b)],
        out_specs=out_specs, out_shape=out_shape,
        compiler_params=_params("arbitrary"),
        name="sgu_sample" if emit_v else "sgu_prompt",
    )(hu, hv, w_s, bs_t, g, b)


BAND_TQ = 2 * CHUNK
BAND_NWB = B_WINDOW // BAND_TQ + 1


BAND_HP = 8


def _band_prompt_kernel(q_ref, k_ref, v_ref, bias_ref, o_ref):
    t = pl.program_id(2)
    tq = BAND_TQ
    shift = int(np.log2(CHUNK))
    q_chunk = lax.shift_right_arithmetic(t * tq + lax.broadcasted_iota(jnp.int32, (tq, 1), 0), shift)
    lane = lax.broadcasted_iota(jnp.int32, (1, tq), 1)
    starts, valids = [], []
    for j in range(BAND_NWB):
        kb = t - (BAND_NWB - 1) + j
        starts.append(pl.multiple_of(jnp.maximum(kb, 0) * tq, tq))
        kpos = kb * tq + lane
        dc = q_chunk - lax.shift_right_arithmetic(kpos, shift)
        valids.append((kpos >= 0) & (dc >= 0) & (dc <= B_PREV_CHUNKS))
    for h in range(BAND_HP):
        sl = slice(h * B_DIM, (h + 1) * B_DIM)
        q = q_ref[:, sl]
        s_blocks = []
        for j in range(BAND_NWB):
            s = (_dot_nt(q, k_ref[pl.ds(starts[j], tq), sl]) * np.float32(B_DIM ** -0.5)
                 + bias_ref[h, :, j * tq:(j + 1) * tq])
            s_blocks.append(jnp.where(valids[j], s, NEG))
        m = s_blocks[0].max(axis=-1, keepdims=True)
        for s in s_blocks[1:]:
            m = jnp.maximum(m, s.max(axis=-1, keepdims=True))
        acc = jnp.zeros((tq, B_DIM), F32)
        den = jnp.zeros((tq, 1), F32)
        for j, s in enumerate(s_blocks):
            p = jnp.exp(s - m)
            den = den + p.sum(axis=-1, keepdims=True)
            acc = acc + _dot(p.astype(BF16), v_ref[pl.ds(starts[j], tq), sl])
        o_ref[:, sl] = (acc / den).astype(o_ref.dtype)


def _band_prompt(q, k, v, bias):
    assert B_HEADS % BAND_HP == 0
    t_all = BATCH * SEQ
    n_t = SEQ // BAND_TQ
    q_spec = pl.BlockSpec((BAND_TQ, BAND_HP * B_DIM), lambda b, h, t: (b * n_t + t, h))
    kv_spec = pl.BlockSpec((SEQ, BAND_HP * B_DIM), lambda b, h, t: (b, h))
    return pl.pallas_call(
        _band_prompt_kernel,
        grid=(BATCH, B_HEADS // BAND_HP, n_t),
        in_specs=[q_spec, kv_spec, kv_spec,
                  pl.BlockSpec((BAND_HP, BAND_TQ, BAND_NWB * BAND_TQ), lambda b, h, t: (h, 0, 0))],
        out_specs=q_spec,
        out_shape=jax.ShapeDtypeStruct((t_all, B_WIDTH), BF16),
        compiler_params=_params("arbitrary", "arbitrary", "arbitrary"),
        name="band_prompt",
    )(q, k, v, bias)


def _band_sample_kernel(q_ref, kn_ref, vn_ref, kc_ref, vc_ref, bc_ref, bn_ref, mc_ref, mn_ref, o_ref):
    scale = np.float32(B_DIM ** -0.5)
    n_cache = kc_ref.shape[0] // B_HEADS
    for h in range(B_HEADS):
        sl = slice(h * B_DIM, (h + 1) * B_DIM)
        q = q_ref[:, sl]
        s_c = _dot_nt(q, kc_ref[pl.ds(h, n_cache, stride=B_HEADS), :].astype(BF16)) * scale + bc_ref[h]
        s_n = _dot_nt(q, kn_ref[:, sl]) * scale + bn_ref[h]
        s_c = jnp.where(mc_ref[...] > 0, s_c, NEG)
        s_n = jnp.where(mn_ref[...] > 0, s_n, NEG)
        m = jnp.maximum(s_c.max(axis=-1, keepdims=True), s_n.max(axis=-1, keepdims=True))
        p_c = jnp.exp(s_c - m)
        p_n = jnp.exp(s_n - m)
        den = p_c.sum(axis=-1, keepdims=True) + p_n.sum(axis=-1, keepdims=True)
        acc = (_dot(p_c.astype(BF16), vc_ref[pl.ds(h, n_cache, stride=B_HEADS), :].astype(BF16))
               + _dot(p_n.astype(BF16), vn_ref[:, sl]))
        o_ref[:, sl] = (acc / den).astype(o_ref.dtype)


def _band_sample(q, k, v, cache_k, cache_v, layer, bias_c, bias_n, mask_c, mask_n):
    n = DEC_SEQ
    off = BATCH * SEQ // n
    n_cache = cache_k.shape[2] // B_HEADS
    row_spec = pl.BlockSpec((n, B_WIDTH), lambda b: (b + off, 0))
    cache_spec = pl.BlockSpec((None, None, n_cache * B_HEADS, B_DIM), lambda b: (layer, b, 0, 0))
    full = lambda a: pl.BlockSpec(a.shape, lambda b: (0,) * a.ndim)
    return pl.pallas_call(
        _band_sample_kernel,
        grid=(DEC_BATCH,),
        in_specs=[row_spec, row_spec, row_spec, cache_spec, cache_spec,
                  full(bias_c), full(bias_n), full(mask_c), full(mask_n)],
        out_specs=pl.BlockSpec((n, B_WIDTH), lambda b: (b, 0)),
        out_shape=jax.ShapeDtypeStruct((DEC_BATCH * n, B_WIDTH), BF16),
        compiler_params=_params("arbitrary"),
        name="band_sample",
    )(q, k, v, cache_k, cache_v, bias_c, bias_n, mask_c, mask_n)


SB_TQ = 256
SB_TK = 256
SB_SAMPLE_TK = 512
SB_HP = 4


def _sb_weights(z, carry, upper):
    sp = _softplus(z)
    hi = sp.astype(BF16)
    lo = (sp - hi.astype(F32)).astype(BF16)
    later = _dot(hi, upper) + _dot(lo, upper)
    w = jnp.exp((z - sp) - (carry + later))
    return w, carry + later[:, 0:1] + sp[:, 0:1]


def _upper(n):
    r_i = lax.broadcasted_iota(jnp.int32, (n, n), 0)
    c_i = lax.broadcasted_iota(jnp.int32, (n, n), 1)
    return (r_i > c_i).astype(BF16)


SB_ROWS = 32


def _sb_prompt_kernel(q_ref, k_ref, v_ref, o_ref, z_ref, ls_ref, hl_ref, lat_ref, w_ref, carry_ref, acc_ref):
    i = pl.program_id(2)
    tq, tk = SB_TQ, SB_TK
    upper2 = jnp.concatenate([_upper(tk)] * 2, axis=0)
    scale = np.float32(C_DIM ** -0.5)
    chunks = [slice(r, r + SB_ROWS) for r in range(0, tq, SB_ROWS)]
    r_i = lax.broadcasted_iota(jnp.int32, (SB_ROWS, tk), 0)
    c_i = lax.broadcasted_iota(jnp.int32, (SB_ROWS, tk), 1)

    def block(j, diagonal):
        start = pl.multiple_of(j * tk, tk)
        for h in range(SB_HP):
            sl = slice(h * C_DIM, (h + 1) * C_DIM)
            z_ref[h] = _dot_nt(q_ref[:, sl], k_ref[pl.ds(start, tk), sl])
        for h in range(SB_HP):
            for ci, rows in enumerate(chunks):
                z = z_ref[h, rows, :] * scale
                if diagonal:
                    z = jnp.where(c_i < r_i + ci * SB_ROWS, z, NEG)
                sp = _softplus(z)
                hi = sp.astype(BF16)
                ls_ref[h, rows, :] = z - sp
                hl_ref[h, rows, 0:tk] = hi
                hl_ref[h, rows, tk:2 * tk] = (sp - hi.astype(F32)).astype(BF16)
        for h in range(SB_HP):
            lat_ref[h] = _dot(hl_ref[h], upper2)
        for h in range(SB_HP):
            for rows in chunks:
                later = lat_ref[h, rows, :]
                carry = carry_ref[h, rows, :]
                w_ref[h, rows, :] = jnp.exp(ls_ref[h, rows, :] - (carry + later)).astype(BF16)
                sp0 = hl_ref[h, rows, 0:1].astype(F32) + hl_ref[h, rows, tk:tk + 1].astype(F32)
                carry_ref[h, rows, :] = carry + later[:, 0:1] + sp0
        for h in range(SB_HP):
            sl = slice(h * C_DIM, (h + 1) * C_DIM)
            acc_ref[h] += _dot(w_ref[h], v_ref[pl.ds(start, tk), sl])

    carry_ref[...] = jnp.zeros_like(carry_ref)
    acc_ref[...] = jnp.zeros_like(acc_ref)
    block(i, True)

    def body(jj, c):
        block(i - 1 - jj, False)
        return c

    lax.fori_loop(0, i, body, 0)
    for h in range(SB_HP):
        o_ref[:, h * C_DIM:(h + 1) * C_DIM] = acc_ref[h].astype(o_ref.dtype)


def _sb_prompt(q, k, v):
    assert SB_TQ == SB_TK and C_HEADS % SB_HP == 0
    t_all = BATCH * SEQ
    n_q = SEQ // SB_TQ
    q_spec = pl.BlockSpec((SB_TQ, SB_HP * C_DIM), lambda b, h, i: (b * n_q + i, h))
    kv_spec = pl.BlockSpec((SEQ, SB_HP * C_DIM), lambda b, h, i: (b, h))
    return pl.pallas_call(
        _sb_prompt_kernel,
        grid=(BATCH, C_HEADS // SB_HP, n_q),
        in_specs=[q_spec, kv_spec, kv_spec],
        out_specs=q_spec,
        out_shape=jax.ShapeDtypeStruct((t_all, C_WIDTH), BF16),
        scratch_shapes=[pltpu.VMEM((SB_HP, SB_TQ, SB_TK), F32), pltpu.VMEM((SB_HP, SB_TQ, SB_TK), F32),
                        pltpu.VMEM((SB_HP, SB_TQ, 2 * SB_TK), BF16), pltpu.VMEM((SB_HP, SB_TQ, SB_TK), F32),
                        pltpu.VMEM((SB_HP, SB_TQ, SB_TK), BF16), pltpu.VMEM((SB_HP, SB_TQ, 1), F32),
                        pltpu.VMEM((SB_HP, SB_TQ, C_DIM), F32)],
        compiler_params=_params("arbitrary", "arbitrary", "arbitrary"),
        name="sb_prompt",
    )(q, k, v)


def _sb_sample_kernel(q_ref, kn_ref, vn_ref, kc_ref, vc_ref, o_ref, z_ref, w_ref, carry_ref, acc_ref):
    jj = pl.program_id(1)
    n = DEC_SEQ
    tk = kc_ref.shape[0] // C_HEADS
    scale = np.float32(C_DIM ** -0.5)
    col = lambda h: slice(h * C_DIM, (h + 1) * C_DIM)

    def all_heads(k_of, v_of, width, mask):
        for h in range(C_HEADS):
            z_ref[h * n:(h + 1) * n, 0:width] = _dot_nt(q_ref[:, col(h)], k_of(h)) * scale
        z = z_ref[:, 0:width]
        if mask is not None:
            z = jnp.where(mask, z, NEG)
        w, carry = _sb_weights(z, carry_ref[...], _upper(width))
        carry_ref[...] = carry
        w_ref[:, 0:width] = w.astype(BF16)
        for h in range(C_HEADS):
            acc_ref[:, col(h)] += _dot(w_ref[h * n:(h + 1) * n, 0:width], v_of(h))

    @pl.when(jj == 0)
    def _():
        carry_ref[...] = jnp.zeros_like(carry_ref)
        acc_ref[...] = jnp.zeros_like(acc_ref)
        r_i = lax.broadcasted_iota(jnp.int32, (C_HEADS * n, n), 0) % n
        c_i = lax.broadcasted_iota(jnp.int32, (C_HEADS * n, n), 1)
        all_heads(lambda h: kn_ref[:, col(h)], lambda h: vn_ref[:, col(h)], n, c_i < r_i)

    all_heads(lambda h: kc_ref[pl.ds(h, tk, stride=C_HEADS), :].astype(BF16),
              lambda h: vc_ref[pl.ds(h, tk, stride=C_HEADS), :].astype(BF16), tk, None)

    @pl.when(jj == pl.num_programs(1) - 1)
    def _():
        o_ref[...] = acc_ref[...].astype(o_ref.dtype)


def _sb_sample(q, k, v, cache_k, cache_v, layer):
    n = DEC_SEQ
    off = BATCH * SEQ // n
    past = cache_k.shape[2] // C_HEADS
    tk = _tile(past, SB_SAMPLE_TK)
    n_kb = past // tk
    row_spec = pl.BlockSpec((n, C_WIDTH), lambda b, j: (b + off, 0))
    cache_spec = pl.BlockSpec((None, None, tk * C_HEADS, C_DIM), lambda b, j: (layer, b, n_kb - 1 - j, 0))
    return pl.pallas_call(
        _sb_sample_kernel,
        grid=(DEC_BATCH, n_kb),
        in_specs=[row_spec, row_spec, row_spec, cache_spec, cache_spec],
        out_specs=pl.BlockSpec((n, C_WIDTH), lambda b, j: (b, 0)),
        out_shape=jax.ShapeDtypeStruct((DEC_BATCH * n, C_WIDTH), BF16),
        scratch_shapes=[pltpu.VMEM((C_HEADS * n, tk), F32), pltpu.VMEM((C_HEADS * n, tk), BF16),
                        pltpu.VMEM((C_HEADS * n, 1), F32), pltpu.VMEM((n, C_WIDTH), F32)],
        compiler_params=_params("arbitrary", "arbitrary"),
        name="sb_sample",
    )(q, k, v, cache_k, cache_v)


def _pack_bf16_pairs(x):
    half = x.shape[1] // 2
    bits = lax.bitcast_convert_type(x.astype(BF16).astype(F32), jnp.uint32)
    return (bits[:, :half] & jnp.uint32(0xFFFF0000)) | (bits[:, half:] >> 16)


def _unpack_bf16_pairs(w):
    hi = lax.bitcast_convert_type(w & jnp.uint32(0xFFFF0000), F32)
    lo = lax.bitcast_convert_type(w << 16, F32)
    return jnp.concatenate([hi, lo], axis=1).astype(BF16)


def _outproj_kernel(*refs, n_lhs, n_prompt_tiles):
    lhs = refs[:2 * n_lhs]
    w_ref, x_ref, g_ref, b_ref, wr_ref, x1_ref, x1p_ref, lg_ref = refs[2 * n_lhs:]
    is_prompt = pl.program_id(0) < n_prompt_tiles
    k0 = 0
    y = None
    for a_p, a_s in zip(lhs[0::2], lhs[1::2]):
        kk = a_p.shape[1]
        part = _dot(jnp.where(is_prompt, a_p[...], a_s[...]), w_ref[k0:k0 + kk, :])
        y = part if y is None else y + part
        k0 += kk
    x1 = _layer_norm(np.float32(ALPHA) * x_ref[...] + y, g_ref[...], b_ref[...])
    x1_ref[...] = x1
    x1p_ref[...] = _pack_bf16_pairs(x1)
    lg_ref[...] = lax.dot_general(wr_ref[...], x1, (((1,), (1,)), ((), ())), preferred_element_type=F32,
                                  precision=lax.Precision.HIGHEST)


def _outproj_ln(lhs, w, x, g, b, w_router_t, name):
    t = x.shape[0]
    tm = MOE_TM
    n_pt = lhs[0][0].shape[0] // tm
    assert all(a_p.shape[0] == n_pt * tm and a_s.shape[0] == t - n_pt * tm for a_p, a_s in lhs)
    row = lambda a: pl.BlockSpec((tm, a.shape[1]), lambda i: (i, 0))
    full = lambda a: pl.BlockSpec(a.shape, lambda i: (0,) * a.ndim)
    lhs_specs, lhs_args = [], []
    for a_p, a_s in lhs:
        lhs_specs += [pl.BlockSpec((tm, a_p.shape[1]), lambda i: (jnp.minimum(i, n_pt - 1), 0)),
                      pl.BlockSpec((tm, a_s.shape[1]), lambda i: (jnp.maximum(i - n_pt, 0), 0))]
        lhs_args += [a_p, a_s]
    return pl.pallas_call(
        functools.partial(_outproj_kernel, n_lhs=len(lhs), n_prompt_tiles=n_pt),
        grid=(t // tm,),
        in_specs=lhs_specs + [full(w), row(x), full(g), full(b), full(w_router_t)],
        out_specs=[pl.BlockSpec((tm, D_MODEL), lambda i: (i, 0)), pl.BlockSpec((tm, D_MODEL // 2), lambda i: (i, 0)),
                   pl.BlockSpec((N_EXPERTS, tm), lambda i: (0, i))],
        out_shape=[jax.ShapeDtypeStruct((t, D_MODEL), F32), jax.ShapeDtypeStruct((t, D_MODEL // 2), jnp.uint32),
                   jax.ShapeDtypeStruct((N_EXPERTS, t), F32)],
        compiler_params=_params("arbitrary"),
        name=name,
    )(*lhs_args, w, x, g, b, w_router_t)


MOE_TM = 256
EXPERT_BM = 256


def _route_select(lg, bias):
    e, tm = lg.shape
    ninf = np.float32(-np.inf)
    shift = int(np.log2(E_PER_GROUP))
    scores = jax.nn.sigmoid(lg)
    sel = scores + bias
    e_id = lax.broadcasted_iota(jnp.int32, (e, tm), 0)
    g_id = lax.shift_right_logical(e_id, shift)
    g3 = sel.reshape(N_GROUPS, E_PER_GROUP, tm)
    i3 = lax.broadcasted_iota(jnp.int32, g3.shape, 1)
    m1 = g3.max(axis=1, keepdims=True)
    first = jnp.where(g3 == m1, i3, E_PER_GROUP).min(axis=1, keepdims=True)
    m2 = jnp.where(i3 == first, ninf, g3).max(axis=1, keepdims=True)
    grp = jnp.broadcast_to(m1 + m2, g3.shape).reshape(e, tm)
    gsel = jnp.zeros((e, tm), jnp.int32)
    for _ in range(TOPK_GROUPS):
        m = grp.max(axis=0, keepdims=True)
        first = jnp.where(grp == m, e_id, e).min(axis=0, keepdims=True)
        chosen = g_id == lax.shift_right_logical(first, shift)
        gsel = jnp.where(chosen, 1, gsel)
        grp = jnp.where(chosen, ninf, grp)
    sel = jnp.where(gsel > 0, sel, NEG)
    picks = []
    for _ in range(TOP_K):
        m = sel.max(axis=0, keepdims=True)
        first = jnp.where(sel == m, e_id, e).min(axis=0, keepdims=True)
        oh = e_id == first
        picks.append(oh)
        sel = jnp.where(oh, ninf, sel)
    return scores, picks


def _route_kernel(lg_ref, bias_ref, dest_ref, gate_ref, cnt_ref, pad_ref, counts_ref, pads_ref, run_ref):
    phase = pl.program_id(0)
    i = pl.program_id(1)
    e, tm = lg_ref.shape
    scores, picks = _route_select(lg_ref[...], bias_ref[...])
    mask = jnp.zeros((e, tm), F32)
    for oh in picks:
        mask = jnp.where(oh, 1.0, mask)

    @pl.when((phase == 0) & (i == 0))
    def _():
        counts_ref[...] = jnp.zeros_like(counts_ref)

    @pl.when(phase == 0)
    def _():
        counts_ref[...] += mask.sum(axis=1, keepdims=True)

    @pl.when((phase == 1) & (i == 0))
    def _():
        counts = counts_ref[...]
        padded = jnp.ceil(counts * np.float32(1.0 / EXPERT_BM)) * np.float32(EXPERT_BM)
        r_i = lax.broadcasted_iota(jnp.int32, (e, e), 0)
        c_i = lax.broadcasted_iota(jnp.int32, (e, e), 1)
        before = (c_i < r_i).astype(F32)
        starts = jnp.dot(before, jnp.broadcast_to(padded, (e, LANES)), preferred_element_type=F32,
                         precision=lax.Precision.HIGHEST)
        pads_ref[...] = starts[:, 0:1]
        run_ref[...] = jnp.zeros_like(run_ref)
        cnt_ref[...] = jnp.broadcast_to(counts, cnt_ref.shape)
        pad_ref[...] = starts

    @pl.when(phase == 1)
    def _():
        r_i = lax.broadcasted_iota(jnp.int32, (tm, tm), 0)
        c_i = lax.broadcasted_iota(jnp.int32, (tm, tm), 1)
        earlier = (r_i < c_i).astype(BF16)
        slot = pads_ref[...] + run_ref[...] + _dot(mask.astype(BF16), earlier)
        run_ref[...] += mask.sum(axis=1, keepdims=True)
        k_i = lax.broadcasted_iota(jnp.int32, (TOP_K, tm), 0)
        dest = jnp.zeros((TOP_K, tm), F32)
        gate = jnp.zeros((TOP_K, tm), F32)
        total = jnp.zeros((1, tm), F32)
        for k, oh in enumerate(picks):
            d_k = jnp.where(oh, slot, 0.0).sum(axis=0, keepdims=True)
            g_k = jnp.where(oh, scores, 0.0).sum(axis=0, keepdims=True)
            total = total + g_k
            dest = jnp.where(k_i == k, d_k, dest)
            gate = jnp.where(k_i == k, g_k, gate)
        dest_ref[...] = dest.astype(jnp.int32)
        gate_ref[...] = gate / total * np.float32(ROUTED_SCALE)


def _route(lg_t, bias):
    e, t = lg_t.shape
    tm = MOE_TM
    n_t = t // tm
    tile = pl.BlockSpec((TOP_K, tm), lambda p, i: (0, i * p))
    meta = pl.BlockSpec((e, LANES), lambda p, i: (0, 0))
    return pl.pallas_call(
        _route_kernel,
        grid=(2, n_t),
        in_specs=[pl.BlockSpec((e, tm), lambda p, i: (0, i)), pl.BlockSpec((e, 1), lambda p, i: (0, 0))],
        out_specs=[tile, tile, meta, meta],
        out_shape=[jax.ShapeDtypeStruct((TOP_K, t), jnp.int32), jax.ShapeDtypeStruct((TOP_K, t), F32),
                   jax.ShapeDtypeStruct((e, LANES), F32), jax.ShapeDtypeStruct((e, LANES), F32)],
        scratch_shapes=[pltpu.VMEM((e, 1), F32), pltpu.VMEM((e, 1), F32), pltpu.VMEM((e, 1), F32)],
        compiler_params=_params("arbitrary", "arbitrary"),
        name="route",
    )(lg_t, bias)


def _row_copy(src, src_row, dst, dst_row, sem):
    return pltpu.make_async_copy(src.at[pl.ds(src_row, 1)], dst.at[pl.ds(dst_row, 1)], sem)


def _scatter_kernel(last_ref, nu_ref, dest_ref, x_ref, xs_out, zeros_ref, sem, zsem):
    i = pl.program_id(0)
    tm = x_ref.shape[0]
    n_blocks = xs_out.shape[0] // EXPERT_BM

    @pl.when(i == 0)
    def _():
        zeros_ref[...] = jnp.zeros_like(zeros_ref)

        def zero_copy(row):
            return pltpu.make_async_copy(zeros_ref, xs_out.at[pl.ds(pl.multiple_of(row, EXPERT_BM), EXPERT_BM)], zsem)

        def tail_row(b):
            return (nu_ref[0] + b) * EXPERT_BM

        def each(fn, n, act):
            def body(b, c):
                row = fn(b)

                @pl.when(row >= 0)
                def _():
                    act(zero_copy(row))
                return c
            lax.fori_loop(0, n, body, 0)

        start = lambda fn, n: each(fn, n, lambda c: c.start())
        wait = lambda fn, n: each(fn, n, lambda c: c.wait())

        start(lambda e: last_ref[e], N_EXPERTS)
        wait(lambda e: last_ref[e], N_EXPERTS)
        start(tail_row, n_blocks - nu_ref[0])
        wait(tail_row, n_blocks - nu_ref[0])

    def issue(t, c):
        for k in range(TOP_K):
            _row_copy(x_ref, t, xs_out, dest_ref[k * tm + t], sem).start(priority=k % 2)
        return c

    def drain(t, c):
        for k in range(TOP_K):
            _row_copy(x_ref, t, xs_out, dest_ref[k * tm + t], sem).wait()
        return c

    lax.fori_loop(0, tm, issue, 0)
    lax.fori_loop(0, tm, drain, 0)


def _scatter(last_row, n_used, dest_flat, x1p, n_slots):
    t, w = x1p.shape
    tm = MOE_TM
    grid_spec = pltpu.PrefetchScalarGridSpec(
        num_scalar_prefetch=2,
        grid=(t // tm,),
        in_specs=[pl.BlockSpec((TOP_K * tm,), lambda i, lr, nu: (i,), memory_space=pltpu.SMEM),
                  pl.BlockSpec((tm, w), lambda i, lr, nu: (i, 0))],
        out_specs=pl.BlockSpec(memory_space=pl.ANY),
        scratch_shapes=[pltpu.VMEM((EXPERT_BM, w), x1p.dtype), pltpu.SemaphoreType.DMA(()),
                        pltpu.SemaphoreType.DMA(())],
    )
    return pl.pallas_call(
        _scatter_kernel,
        grid_spec=grid_spec,
        out_shape=jax.ShapeDtypeStruct((n_slots, w), x1p.dtype),
        compiler_params=_params("arbitrary"),
        name="scatter",
    )(last_row, n_used, dest_flat, x1p)


def _expert_kernel(be_ref, nu_ref, slot_ref, nxt_ref, x_ref, wg_hbm, wu_hbm, wd_hbm, y_ref,
                   wgf, wuf, wdf, wgb, wub, wdb, sem, *, layer):
    i = pl.program_id(0)
    active = i < nu_ref[0]
    first = (i == 0) | (be_ref[i] != be_ref[jnp.maximum(i - 1, 0)])

    def weight_copies(e, s):
        return [pltpu.make_async_copy(src.at[layer, e], dst.at[s], sem.at[s])
                for src, dst in ((wg_hbm, wgf), (wu_hbm, wuf), (wd_hbm, wdf))]

    @pl.when(active & (i == 0))
    def _():
        for c in weight_copies(be_ref[0], slot_ref[0]):
            c.start()

    @pl.when(active & first)
    def _():
        s = slot_ref[i]
        for c in weight_copies(be_ref[i], s):
            c.wait()
        wgb[...] = wgf[s].astype(BF16)
        wub[...] = wuf[s].astype(BF16)
        wdb[...] = wdf[s].astype(BF16)

        @pl.when(nxt_ref[i] >= 0)
        def _():
            for c in weight_copies(nxt_ref[i], 1 - s):
                c.start()

    @pl.when(active)
    def _():
        x = _unpack_bf16_pairs(x_ref[...])
        g = _dot(x, wgb[...])
        h = (g * jax.nn.sigmoid(g)) * _dot(x, wub[...])
        y_ref[...] = _pack_bf16_pairs(_dot(h.astype(BF16), wdb[...]))

    @pl.when(jnp.logical_not(active))
    def _():
        y_ref[...] = jnp.zeros_like(y_ref)


def _experts(block_e, n_used, slot, nxt, xs, layer, wg, wu, wd):
    n_slots = xs.shape[0]
    n_blocks = n_slots // EXPERT_BM
    ff = wg.shape[3]
    any_spec = pl.BlockSpec(memory_space=pl.ANY)
    grid_spec = pltpu.PrefetchScalarGridSpec(
        num_scalar_prefetch=4,
        grid=(n_blocks,),
        in_specs=[pl.BlockSpec((EXPERT_BM, D_MODEL // 2), lambda i, be, nu, sl, nx: (jnp.minimum(i, nu[0] - 1), 0)),
                  any_spec, any_spec, any_spec],
        out_specs=pl.BlockSpec((EXPERT_BM, D_MODEL // 2), lambda i, be, nu, sl, nx: (i, 0)),
        scratch_shapes=[pltpu.VMEM((2, D_MODEL, ff), F32), pltpu.VMEM((2, D_MODEL, ff), F32),
                        pltpu.VMEM((2, ff, D_MODEL), F32),
                        pltpu.VMEM((D_MODEL, ff), BF16), pltpu.VMEM((D_MODEL, ff), BF16),
                        pltpu.VMEM((ff, D_MODEL), BF16), pltpu.SemaphoreType.DMA((2,))],
    )
    return pl.pallas_call(
        functools.partial(_expert_kernel, layer=layer),
        grid_spec=grid_spec,
        out_shape=jax.ShapeDtypeStruct((n_slots, D_MODEL // 2), jnp.uint32),
        compiler_params=_params("arbitrary"),
        name="experts",
    )(block_e, n_used, slot, nxt, xs, wg, wu, wd)


def _ffn_ln_kernel(dest_ref, xp_ref, x_ref, gate_ref, y_hbm, wg_ref, wu_ref, wd_ref, g_ref, b_ref, o_ref, buf, sem):
    tm = x_ref.shape[0]

    def issue(t, c):
        for k in range(TOP_K):
            _row_copy(y_hbm, dest_ref[k * tm + t], buf.at[k], t, sem).start(priority=k % 2)
        return c

    def drain(t, c):
        for k in range(TOP_K):
            _row_copy(y_hbm, dest_ref[k * tm + t], buf.at[k], t, sem).wait()
        return c

    lax.fori_loop(0, tm, issue, 0)
    xb = _unpack_bf16_pairs(xp_ref[...])
    gte = _dot(xb, wg_ref[...])
    h = (gte * jax.nn.sigmoid(gte)) * _dot(xb, wu_ref[...])
    shared = _dot(h.astype(BF16), wd_ref[...])
    lax.fori_loop(0, tm, drain, 0)
    lo_half = hi_half = None
    for k in range(TOP_K):
        w = buf[k]
        g = gate_ref[:, k:k + 1]
        hi = lax.bitcast_convert_type(w & jnp.uint32(0xFFFF0000), F32) * g
        lo = lax.bitcast_convert_type(w << 16, F32) * g
        hi_half = hi if hi_half is None else hi_half + hi
        lo_half = lo if lo_half is None else lo_half + lo
    routed = jnp.concatenate([hi_half, lo_half], axis=1)
    z = np.float32(ALPHA) * x_ref[...] + (routed + shared)
    o_ref[...] = _layer_norm(z, g_ref[...], b_ref[...])


def _ffn_ln(dest_flat, x1p, x1, gate_t, y, wg, wu, wd, g, b):
    t = x1.shape[0]
    tm = MOE_TM
    row = lambda a: pl.BlockSpec((tm, a.shape[1]), lambda i: (i, 0))
    full = lambda a: pl.BlockSpec(a.shape, lambda i: (0,) * a.ndim)
    return pl.pallas_call(
        _ffn_ln_kernel,
        grid=(t // tm,),
        in_specs=[pl.BlockSpec((TOP_K * tm,), lambda i: (i,), memory_space=pltpu.SMEM),
                  row(x1p), row(x1), row(gate_t), pl.BlockSpec(memory_space=pl.ANY),
                  full(wg), full(wu), full(wd), full(g), full(b)],
        out_specs=row(x1),
        out_shape=jax.ShapeDtypeStruct((t, D_MODEL), F32),
        scratch_shapes=[pltpu.VMEM((TOP_K, tm, D_MODEL // 2), jnp.uint32), pltpu.SemaphoreType.DMA(())],
        compiler_params=_params("arbitrary"),
        name="ffn_ln",
    )(dest_flat, x1p, x1, gate_t, y, wg, wu, wd, g, b)


def _ple_kernel(x_ref, p_ref, wp_ref, wg_ref, *o_refs, n_prompt_tiles):
    x = x_ref[...]
    gate = jax.nn.sigmoid(_dot(x.astype(BF16), wg_ref[...]))
    emb = _dot(p_ref[...].astype(BF16), wp_ref[...])
    out = x + emb * gate
    if len(o_refs) == 1:
        o_refs[0][...] = out
    else:
        i = pl.program_id(0)

        @pl.when(i < n_prompt_tiles)
        def _():
            o_refs[0][...] = out

        @pl.when(i >= n_prompt_tiles)
        def _():
            o_refs[1][...] = out


def _ple(x2, p, w_ple, w_gate, n_prompt=None):
    t = x2.shape[0]
    tm = _tile(t if n_prompt is None else np.gcd(n_prompt, t - n_prompt), 256)
    n_pt = None if n_prompt is None else n_prompt // tm
    row = lambda a: pl.BlockSpec((tm, a.shape[1]), lambda i: (i, 0))
    full = lambda a: pl.BlockSpec(a.shape, lambda i: (0,) * a.ndim)
    if n_prompt is None:
        out_specs = [row(x2)]
        out_shape = [jax.ShapeDtypeStruct((t, D_MODEL), F32)]
    else:
        out_specs = [pl.BlockSpec((tm, D_MODEL), lambda i: (jnp.minimum(i, n_pt - 1), 0)),
                     pl.BlockSpec((tm, D_MODEL), lambda i: (jnp.maximum(i - n_pt, 0), 0))]
        out_shape = [jax.ShapeDtypeStruct((n_prompt, D_MODEL), F32),
                     jax.ShapeDtypeStruct((t - n_prompt, D_MODEL), F32)]
    return pl.pallas_call(
        functools.partial(_ple_kernel, n_prompt_tiles=n_pt),
        grid=(t // tm,),
        in_specs=[row(x2), row(p), full(w_ple), full(w_gate)],
        out_specs=out_specs, out_shape=out_shape,
        compiler_params=_params("arbitrary"),
        name="ple",
    )(x2, p, w_ple, w_gate)


def _rel_bias(table, qpos, kpos):
    idx = np.clip(qpos[:, None] - kpos[None, :], -REL_CLIP, REL_CLIP) + REL_CLIP
    return table[:, idx]


def _band_prompt_bias(table):
    tq, width = BAND_TQ, BAND_NWB * BAND_TQ
    span = width + tq - 1
    rel = (BAND_NWB - 1) * tq + tq - 1 - np.arange(span)
    strip = table[:, np.clip(rel, -REL_CLIP, REL_CLIP) + REL_CLIP]
    u = jnp.concatenate([strip[:, tq - 1:], strip[:, :1], strip[:, :tq - 1]], axis=1)
    flat = jnp.tile(u, (1, tq))[:, :tq * span]
    return flat.reshape(table.shape[0], tq, span)[:, :, :width]


def _band_mask(qpos, kpos):
    dc = qpos // CHUNK - kpos // CHUNK
    return (kpos >= 0) & (dc >= 0) & (dc <= B_PREV_CHUNKS)


def _post_block(x, lhs, w_out, p, ln1_g, ln1_b, ln2_g, ln2_b, w_router, b_router,
                layer, w_gate, w_up, w_down, ws_gate, ws_up, ws_down, w_ple, w_ple_gate, name, n_prompt):
    row = lambda a: a.reshape(1, -1)
    t = x.shape[0]
    x1, x1p, lg_t = _outproj_ln(lhs, w_out, x, row(ln1_g), row(ln1_b), w_router.T, name)
    dest, gate, cnt, pad = _route(lg_t, b_router.astype(F32).reshape(N_EXPERTS, 1))

    n_blocks = -(-(t * TOP_K + N_EXPERTS * (EXPERT_BM - 1)) // EXPERT_BM)
    counts = cnt[:, 0].astype(jnp.int32)
    pad_end = pad[:, 0].astype(jnp.int32) + (counts + EXPERT_BM - 1) // EXPERT_BM * EXPERT_BM
    block_e = jnp.minimum((pad_end[:, None] <= jnp.arange(n_blocks)[None, :] * EXPERT_BM).sum(axis=0),
                          N_EXPERTS - 1).astype(jnp.int32)
    n_used = (pad_end[-1:] // EXPERT_BM).astype(jnp.int32)
    nonempty = counts > 0
    e_ids = jnp.arange(N_EXPERTS)
    later_ids = jnp.where(nonempty[None, :] & (e_ids[None, :] > e_ids[:, None]), e_ids[None, :], N_EXPERTS).min(axis=1)
    nxt_of = jnp.where(later_ids < N_EXPERTS, later_ids, -1)
    slot_of = (jnp.cumsum(nonempty) - 1) % 2
    in_block = (block_e[:, None] == e_ids[None, :]).astype(jnp.int32)
    nxt = (in_block * nxt_of[None, :]).sum(axis=1).astype(jnp.int32)
    slot = (in_block * slot_of[None, :]).sum(axis=1).astype(jnp.int32)
    dest_flat = dest.reshape(TOP_K, t // MOE_TM, MOE_TM).transpose(1, 0, 2).reshape(-1)

    last_row = jnp.where(counts > 0, pad_end - EXPERT_BM, -1).astype(jnp.int32)
    xs = _scatter(last_row, n_used, dest_flat, x1p, n_blocks * EXPERT_BM)
    y = _experts(block_e, n_used, slot, nxt, xs, layer, w_gate, w_up, w_down)
    x2 = _ffn_ln(dest_flat, x1p, x1, gate.T, y, ws_gate, ws_up, ws_down, row(ln2_g), row(ln2_b))
    return _ple(x2, p, w_ple, w_ple_gate, n_prompt)


def kernel(x_prompt, x_sample, cache_b_k, cache_b_v, cache_c_k, cache_c_v, p_prompt, p_sample, w_in_ab, w_out_ab, sgu_w, sgu_b, sgu_ln_g, sgu_ln_b, rel_bias_tab, w_in_c, w_out_c, ln_mix_g, ln_mix_b, ln_ffn_g, ln_ffn_b, w_router, b_router, w_gate, w_up, w_down, ws_gate, ws_up, ws_down, w_ple, w_ple_gate):
    n_p = BATCH * SEQ
    n_s = DEC_BATCH * DEC_SEQ
    x = jnp.concatenate([x_prompt.reshape(n_p, D_MODEL), x_sample.reshape(n_s, D_MODEL)], axis=0)
    p_all = jnp.concatenate([p_prompt.reshape(DEPTH, n_p, PLE_DIM), p_sample.reshape(DEPTH, n_s, PLE_DIM)], axis=1)
    bf = lambda a: a.astype(BF16)
    outs = {k: [] for k in ("bk_p", "bv_p", "bk_s", "bv_s", "av_s", "ck_p", "cv_p", "ck_s", "cv_s")}
    kv_outs = ((F32, True), (BF16, False))

    for i in range(DEPTH):
        j = i // 2
        if i % 2 == 0:
            w_in = w_in_ab[j]
            hu, = _matmul(x, w_in, 0, A_WIDTH, ((F32, False),), "proj_u")
            hv, = _matmul(x, w_in, A_WIDTH, A_WIDTH, ((F32, False),), "proj_v")
            q, = _matmul(x, w_in, 2 * A_WIDTH, B_WIDTH, ((BF16, False),), "proj_q")
            k_p, k_s, kb16 = _matmul(x, w_in, 2 * A_WIDTH + B_WIDTH, B_WIDTH, kv_outs, "proj_k", n_p)
            v_p, v_s, vb16 = _matmul(x, w_in, 2 * A_WIDTH + 2 * B_WIDTH, B_WIDTH, kv_outs, "proj_vv", n_p)

            bs_t = sgu_b[j].T
            a_p, = _sgu(hu, hv, sgu_w[j], bs_t, sgu_ln_g[j], sgu_ln_b[j], A_CHUNK, 0, n_p // A_CHUNK, False)
            a_s, va_s = _sgu(hu, hv, sgu_w[j][:, :DEC_SEQ, :DEC_SEQ], bs_t[:DEC_SEQ], sgu_ln_g[j], sgu_ln_b[j],
                             DEC_SEQ, n_p, DEC_BATCH, True)

            b_p = _band_prompt(q, kb16, vb16, _band_prompt_bias(rel_bias_tab[j]))
            n_cache = cache_b_k.shape[2]
            qs = PAST_LEN + np.arange(DEC_SEQ)
            kc = PAST_LEN - n_cache + np.arange(n_cache)
            b_s = _band_sample(
                q, kb16, vb16, cache_b_k.reshape(N_EVEN, DEC_BATCH, n_cache * B_HEADS, B_DIM),
                cache_b_v.reshape(N_EVEN, DEC_BATCH, n_cache * B_HEADS, B_DIM), j,
                _rel_bias(rel_bias_tab[j], qs, kc), _rel_bias(rel_bias_tab[j], qs, qs),
                jnp.asarray(_band_mask(qs[:, None], kc[None, :]), F32),
                jnp.asarray(_band_mask(qs[:, None], qs[None, :]), F32))

            keep = min(B_WINDOW, SEQ)
            outs["bk_p"].append(k_p.reshape(BATCH, SEQ, B_HEADS, B_DIM)[:, SEQ - keep:])
            outs["bv_p"].append(v_p.reshape(BATCH, SEQ, B_HEADS, B_DIM)[:, SEQ - keep:])
            outs["bk_s"].append(k_s.reshape(DEC_BATCH, DEC_SEQ, B_HEADS, B_DIM))
            outs["bv_s"].append(v_s.reshape(DEC_BATCH, DEC_SEQ, B_HEADS, B_DIM))
            outs["av_s"].append(va_s.reshape(DEC_BATCH, DEC_SEQ, A_HEADS, A_DIM))
            lhs, w_out, name = [(a_p, a_s), (b_p, b_s)], bf(w_out_ab[j]), "outproj_ab"
        else:
            w_in = w_in_c[j]
            q, = _matmul(x, w_in, 0, C_WIDTH, ((BF16, False),), "proj_cq")
            k_p, k_s, kb16 = _matmul(x, w_in, C_WIDTH, C_WIDTH, kv_outs, "proj_ck", n_p)
            v_p, v_s, vb16 = _matmul(x, w_in, 2 * C_WIDTH, C_WIDTH, kv_outs, "proj_cv", n_p)
            o_p = _sb_prompt(q, kb16, vb16)
            rows = lambda c: c.reshape(N_ODD, DEC_BATCH, PAST_LEN * C_HEADS, C_DIM)
            o_s = _sb_sample(q, kb16, vb16, rows(cache_c_k), rows(cache_c_v), j)
            outs["ck_p"].append(k_p.reshape(BATCH, SEQ, C_HEADS, C_DIM))
            outs["cv_p"].append(v_p.reshape(BATCH, SEQ, C_HEADS, C_DIM))
            outs["ck_s"].append(k_s.reshape(DEC_BATCH, DEC_SEQ, C_HEADS, C_DIM))
            outs["cv_s"].append(v_s.reshape(DEC_BATCH, DEC_SEQ, C_HEADS, C_DIM))
            lhs, w_out, name = [(o_p, o_s)], bf(w_out_c[j]), "outproj_c"

        x = _post_block(x, lhs, w_out, p_all[i], ln_mix_g[i], ln_mix_b[i], ln_ffn_g[i], ln_ffn_b[i],
                        w_router[i], b_router[i], i, w_gate, w_up, w_down,
                        bf(ws_gate[i]), bf(ws_up[i]), bf(ws_down[i]), bf(w_ple[i]), bf(w_ple_gate[i]), name,
                        n_p if i == DEPTH - 1 else None)
        if i < DEPTH - 1:
            x, = x

    y_p, y_s = x
    st = lambda key: jnp.stack(outs[key])
    return (y_p.reshape(BATCH, SEQ, D_MODEL), y_s.reshape(DEC_BATCH, DEC_SEQ, D_MODEL),
            st("bk_p"), st("bv_p"), st("bk_s"), st("bv_s"), st("av_s"),
            st("ck_p"), st("cv_p"), st("ck_s"), st("cv_s"))
```

```python
import functools

import numpy as np
import jax
import jax.numpy as jnp
from jax import lax
from jax.experimental import pallas as pl
from jax.experimental.pallas import tpu as pltpu

D_MODEL = 2048
BATCH = 2
SEQ = 4096
DEPTH = 2
DEC_BATCH = 32
DEC_SEQ = 16
PAST_LEN = 2048

CHUNK = 64
N_EVEN = (DEPTH + 1) // 2
N_ODD = DEPTH // 2
A_CHUNK = 128
A_HEADS = 8
A_DIM = 128
A_WIDTH = A_HEADS * A_DIM
B_HEADS = 8
B_DIM = 128
B_WIDTH = B_HEADS * B_DIM
B_PREV_CHUNKS = 8
B_WINDOW = B_PREV_CHUNKS * CHUNK
REL_CLIP = 128
C_HEADS = 16
C_DIM = 128
C_WIDTH = C_HEADS * C_DIM
N_EXPERTS = 64
N_GROUPS = 8
E_PER_GROUP = N_EXPERTS // N_GROUPS
TOPK_GROUPS = 4
TOP_K = 8
EXPERT_FF = 512
SHARED_FF = 512
ROUTED_SCALE = 2.5
PLE_DIM = 256
LN_EPS = 1e-5
ALPHA = (2 * DEPTH) ** 0.25
NEG = -1e9

LANES = 128
VMEM_LIMIT_BYTES = 56 * 1024 * 1024

BF16 = jnp.bfloat16
F32 = jnp.float32


def _params(*sem):
    return pltpu.CompilerParams(dimension_semantics=sem, vmem_limit_bytes=VMEM_LIMIT_BYTES)


def _tile(n, pref):
    if n <= pref:
        return n
    for t in range(pref, 7, -1):
        if n % t == 0 and t % 8 == 0:
            return t
    return n


def _dot(a, b):
    return jnp.dot(a, b, preferred_element_type=F32)


def _dot_nt(a, b):
    return lax.dot_general(a, b, (((1,), (1,)), ((), ())), preferred_element_type=F32)


def _layer_norm(z, g, b):
    mu = jnp.mean(z, axis=-1, keepdims=True)
    zc = z - mu
    var = jnp.mean(zc * zc, axis=-1, keepdims=True)
    return zc * lax.rsqrt(var + LN_EPS) * g + b


def _gelu(x):
    return x * (lax.erf(x * np.float32(1.0 / np.sqrt(2.0))) + 1.0) * 0.5


def _softplus(z):
    return jnp.maximum(z, 0.0) + jnp.log(1.0 + jnp.exp(-jnp.abs(z)))


def _proj_kernel(x_ref, w_ref, *o_refs, segs, n_prompt_tiles):
    xb = x_ref[...].astype(BF16)
    i = pl.program_id(0)
    refs = iter(o_refs)
    for col0, ncols, outs in segs:
        acc = _dot(xb, w_ref[:, col0:col0 + ncols])
        for _, is_split in outs:
            if is_split:
                o_p, o_s = next(refs), next(refs)

                @pl.when(i < n_prompt_tiles)
                def _(o_p=o_p, acc=acc):
                    o_p[...] = acc.astype(o_p.dtype)

                @pl.when(i >= n_prompt_tiles)
                def _(o_s=o_s, acc=acc):
                    o_s[...] = acc.astype(o_s.dtype)
            else:
                o = next(refs)
                o[...] = acc.astype(o.dtype)


def _proj(x, w, col_block, segs, name, n_prompt):
    m, k = x.shape
    width = sum(n for _, n, _ in segs)
    tm = _tile(np.gcd(n_prompt, m - n_prompt), 256)
    n_pt = n_prompt // tm
    out_specs, out_shape = [], []
    for _, ncols, outs in segs:
        for dtype, is_split in outs:
            if is_split:
                out_specs += [pl.BlockSpec((tm, ncols), lambda i: (jnp.minimum(i, n_pt - 1), 0)),
                              pl.BlockSpec((tm, ncols), lambda i: (jnp.maximum(i - n_pt, 0), 0))]
                out_shape += [jax.ShapeDtypeStruct((n_prompt, ncols), dtype),
                              jax.ShapeDtypeStruct((m - n_prompt, ncols), dtype)]
            else:
                out_specs.append(pl.BlockSpec((tm, ncols), lambda i: (i, 0)))
                out_shape.append(jax.ShapeDtypeStruct((m, ncols), dtype))
    return pl.pallas_call(
        functools.partial(_proj_kernel, segs=tuple(segs), n_prompt_tiles=n_pt),
        grid=(m // tm,),
        in_specs=[pl.BlockSpec((tm, k), lambda i: (i, 0)),
                  pl.BlockSpec((k, width), lambda i: (0, col_block), pipeline_mode=pl.Buffered(1))],
        out_specs=out_specs, out_shape=out_shape,
        compiler_params=_params("arbitrary"),
        name=name,
    )(x, w)


def _sgu_kernel(hu_ref, hv_ref, w_ref, bs_ref, g_ref, b_ref, a_ref, *v_out, emit_v):
    rows = hu_ref.shape[0]
    r_i = lax.broadcasted_iota(jnp.int32, (rows, rows), 0)
    c_i = lax.broadcasted_iota(jnp.int32, (rows, rows), 1)
    causal = c_i <= r_i
    for h in range(A_HEADS):
        sl = slice(h * A_DIM, (h + 1) * A_DIM)
        u = _gelu(hu_ref[:, sl])
        v = _layer_norm(_gelu(hv_ref[:, sl]), g_ref[h:h + 1, :], b_ref[h:h + 1, :])
        w = jnp.where(causal, w_ref[h], 0.0).astype(BF16)
        mix = _dot(w, v.astype(BF16)) + bs_ref[:, h:h + 1]
        a_ref[:, sl] = (u * mix).astype(a_ref.dtype)
        if emit_v:
            v_out[0][:, sl] = v


def _sgu(hu, hv, w_s, bs_t, g, b, rows, row0, n_chunks, emit_v):
    off = row0 // rows
    in_spec = pl.BlockSpec((rows, A_WIDTH), lambda c: (c + off, 0))
    out_spec = pl.BlockSpec((rows, A_WIDTH), lambda c: (c, 0))
    full = lambda a: pl.BlockSpec(a.shape, lambda c: (0,) * a.ndim)
    out_specs = [out_spec]
    out_shape = [jax.ShapeDtypeStruct((n_chunks * rows, A_WIDTH), BF16)]
    if emit_v:
        out_specs.append(out_spec)
        out_shape.append(jax.ShapeDtypeStruct((n_chunks * rows, A_WIDTH), F32))
    return pl.pallas_call(
        functools.partial(_sgu_kernel, emit_v=emit_v),
        grid=(n_chunks,),
        in_specs=[in_spec, in_spec, full(w_s), full(bs_t), full(g), full(b)],
        out_specs=out_specs, out_shape=out_shape,
        compiler_params=_params("arbitrary"),
        name="sgu_sample" if emit_v else "sgu_prompt",
    )(hu, hv, w_s, bs_t, g, b)


BAND_TQ = 2 * CHUNK
BAND_NWB = B_WINDOW // BAND_TQ + 1


BAND_HP = 8


def _band_prompt_kernel(q_ref, k_ref, v_ref, bias_ref, o_ref):
    t = pl.program_id(2)
    tq = BAND_TQ
    shift = int(np.log2(CHUNK))
    q_chunk = lax.shift_right_arithmetic(t * tq + lax.broadcasted_iota(jnp.int32, (tq, 1), 0), shift)
    lane = lax.broadcasted_iota(jnp.int32, (1, tq), 1)
    starts, valids = [], []
    for j in range(BAND_NWB):
        kb = t - (BAND_NWB - 1) + j
        starts.append(pl.multiple_of(jnp.maximum(kb, 0) * tq, tq))
        kpos = kb * tq + lane
        dc = q_chunk - lax.shift_right_arithmetic(kpos, shift)
        valids.append((kpos >= 0) & (dc >= 0) & (dc <= B_PREV_CHUNKS))
    for h in range(BAND_HP):
        sl = slice(h * B_DIM, (h + 1) * B_DIM)
        q = q_ref[:, sl]
        s_blocks = []
        for j in range(BAND_NWB):
            s = (_dot_nt(q, k_ref[pl.ds(starts[j], tq), sl]) * np.float32(B_DIM ** -0.5)
                 + bias_ref[h, :, j * tq:(j + 1) * tq])
            s_blocks.append(jnp.where(valids[j], s, NEG))
        m = s_blocks[0].max(axis=-1, keepdims=True)
        for s in s_blocks[1:]:
            m = jnp.maximum(m, s.max(axis=-1, keepdims=True))
        acc = jnp.zeros((tq, B_DIM), F32)
        den = jnp.zeros((tq, 1), F32)
        for j, s in enumerate(s_blocks):
            p = jnp.exp(s - m)
            den = den + p.sum(axis=-1, keepdims=True)
            acc = acc + _dot(p.astype(BF16), v_ref[pl.ds(starts[j], tq), sl])
        o_ref[:, sl] = (acc / den).astype(o_ref.dtype)


def _band_prompt(q, k, v, bias):
    assert B_HEADS % BAND_HP == 0
    t_all = BATCH * SEQ
    n_t = SEQ // BAND_TQ
    q_spec = pl.BlockSpec((BAND_TQ, BAND_HP * B_DIM), lambda b, h, t: (b * n_t + t, h))
    kv_spec = pl.BlockSpec((SEQ, BAND_HP * B_DIM), lambda b, h, t: (b, h))
    return pl.pallas_call(
        _band_prompt_kernel,
        grid=(BATCH, B_HEADS // BAND_HP, n_t),
        in_specs=[q_spec, kv_spec, kv_spec,
                  pl.BlockSpec((BAND_HP, BAND_TQ, BAND_NWB * BAND_TQ), lambda b, h, t: (h, 0, 0))],
        out_specs=q_spec,
        out_shape=jax.ShapeDtypeStruct((t_all, B_WIDTH), BF16),
        compiler_params=_params("arbitrary", "arbitrary", "arbitrary"),
        name="band_prompt",
    )(q, k, v, bias)


def _band_sample_kernel(q_ref, kn_ref, vn_ref, kc_ref, vc_ref, bc_ref, bn_ref, mc_ref, mn_ref, o_ref):
    scale = np.float32(B_DIM ** -0.5)
    n_cache = kc_ref.shape[0] // B_HEADS
    for h in range(B_HEADS):
        sl = slice(h * B_DIM, (h + 1) * B_DIM)
        q = q_ref[:, sl]
        s_c = _dot_nt(q, kc_ref[pl.ds(h, n_cache, stride=B_HEADS), :].astype(BF16)) * scale + bc_ref[h]
        s_n = _dot_nt(q, kn_ref[:, sl]) * scale + bn_ref[h]
        s_c = jnp.where(mc_ref[...] > 0, s_c, NEG)
        s_n = jnp.where(mn_ref[...] > 0, s_n, NEG)
        m = jnp.maximum(s_c.max(axis=-1, keepdims=True), s_n.max(axis=-1, keepdims=True))
        p_c = jnp.exp(s_c - m)
        p_n = jnp.exp(s_n - m)
        den = p_c.sum(axis=-1, keepdims=True) + p_n.sum(axis=-1, keepdims=True)
        acc = (_dot(p_c.astype(BF16), vc_ref[pl.ds(h, n_cache, stride=B_HEADS), :].astype(BF16))
               + _dot(p_n.astype(BF16), vn_ref[:, sl]))
        o_ref[:, sl] = (acc / den).astype(o_ref.dtype)


def _band_sample(q, k, v, cache_k, cache_v, layer, bias_c, bias_n, mask_c, mask_n):
    n = DEC_SEQ
    off = BATCH * SEQ // n
    n_cache = cache_k.shape[2] // B_HEADS
    row_spec = pl.BlockSpec((n, B_WIDTH), lambda b: (b + off, 0))
    cache_spec = pl.BlockSpec((None, None, n_cache * B_HEADS, B_DIM), lambda b: (layer, b, 0, 0))
    full = lambda a: pl.BlockSpec(a.shape, lambda b: (0,) * a.ndim)
    return pl.pallas_call(
        _band_sample_kernel,
        grid=(DEC_BATCH,),
        in_specs=[row_spec, row_spec, row_spec, cache_spec, cache_spec,
                  full(bias_c), full(bias_n), full(mask_c), full(mask_n)],
        out_specs=pl.BlockSpec((n, B_WIDTH), lambda b: (b, 0)),
        out_shape=jax.ShapeDtypeStruct((DEC_BATCH * n, B_WIDTH), BF16),
        compiler_params=_params("arbitrary"),
        name="band_sample",
    )(q, k, v, cache_k, cache_v, bias_c, bias_n, mask_c, mask_n)


SB_TQ = 256
SB_TK = 256
SB_SAMPLE_TK = 512
SB_HP = 4


def _sb_weights(z, carry, upper):
    sp = _softplus(z)
    hi = sp.astype(BF16)
    lo = (sp - hi.astype(F32)).astype(BF16)
    later = _dot(hi, upper) + _dot(lo, upper)
    w = jnp.exp((z - sp) - (carry + later))
    return w, carry + later[:, 0:1] + sp[:, 0:1]


def _upper(n):
    r_i = lax.broadcasted_iota(jnp.int32, (n, n), 0)
    c_i = lax.broadcasted_iota(jnp.int32, (n, n), 1)
    return (r_i > c_i).astype(BF16)


SB_ROWS = 32


def _sb_prompt_kernel(q_ref, k_ref, v_ref, o_ref, z_ref, ls_ref, hl_ref, lat_ref, w_ref, carry_ref, acc_ref):
    i = pl.program_id(2)
    tq, tk = SB_TQ, SB_TK
    upper2 = jnp.concatenate([_upper(tk)] * 2, axis=0)
    scale = np.float32(C_DIM ** -0.5)
    chunks = [slice(r, r + SB_ROWS) for r in range(0, tq, SB_ROWS)]
    r_i = lax.broadcasted_iota(jnp.int32, (SB_ROWS, tk), 0)
    c_i = lax.broadcasted_iota(jnp.int32, (SB_ROWS, tk), 1)

    def block(j, diagonal):
        start = pl.multiple_of(j * tk, tk)
        for h in range(SB_HP):
            sl = slice(h * C_DIM, (h + 1) * C_DIM)
            z_ref[h] = _dot_nt(q_ref[:, sl], k_ref[pl.ds(start, tk), sl])
        for h in range(SB_HP):
            for ci, rows in enumerate(chunks):
                z = z_ref[h, rows, :] * scale
                if diagonal:
                    z = jnp.where(c_i < r_i + ci * SB_ROWS, z, NEG)
                sp = _softplus(z)
                hi = sp.astype(BF16)
                ls_ref[h, rows, :] = z - sp
                hl_ref[h, rows, 0:tk] = hi
                hl_ref[h, rows, tk:2 * tk] = (sp - hi.astype(F32)).astype(BF16)
        for h in range(SB_HP):
            lat_ref[h] = _dot(hl_ref[h], upper2)
        for h in range(SB_HP):
            for rows in chunks:
                later = lat_ref[h, rows, :]
                carry = carry_ref[h, rows, :]
                w_ref[h, rows, :] = jnp.exp(ls_ref[h, rows, :] - (carry + later)).astype(BF16)
                sp0 = hl_ref[h, rows, 0:1].astype(F32) + hl_ref[h, rows, tk:tk + 1].astype(F32)
                carry_ref[h, rows, :] = carry + later[:, 0:1] + sp0
        for h in range(SB_HP):
            sl = slice(h * C_DIM, (h + 1) * C_DIM)
            acc_ref[h] += _dot(w_ref[h], v_ref[pl.ds(start, tk), sl])

    carry_ref[...] = jnp.zeros_like(carry_ref)
    acc_ref[...] = jnp.zeros_like(acc_ref)
    block(i, True)

    def body(jj, c):
        block(i - 1 - jj, False)
        return c

    lax.fori_loop(0, i, body, 0)
    for h in range(SB_HP):
        o_ref[:, h * C_DIM:(h + 1) * C_DIM] = acc_ref[h].astype(o_ref.dtype)


def _sb_prompt(q, k, v):
    assert SB_TQ == SB_TK and C_HEADS % SB_HP == 0
    t_all = BATCH * SEQ
    n_q = SEQ // SB_TQ
    q_spec = pl.BlockSpec((SB_TQ, SB_HP * C_DIM), lambda b, h, i: (b * n_q + i, h))
    kv_spec = pl.BlockSpec((SEQ, SB_HP * C_DIM), lambda b, h, i: (b, h))
    return pl.pallas_call(
        _sb_prompt_kernel,
        grid=(BATCH, C_HEADS // SB_HP, n_q),
        in_specs=[q_spec, kv_spec, kv_spec],
        out_specs=q_spec,
        out_shape=jax.ShapeDtypeStruct((t_all, C_WIDTH), BF16),
        scratch_shapes=[pltpu.VMEM((SB_HP, SB_TQ, SB_TK), F32), pltpu.VMEM((SB_HP, SB_TQ, SB_TK), F32),
                        pltpu.VMEM((SB_HP, SB_TQ, 2 * SB_TK), BF16), pltpu.VMEM((SB_HP, SB_TQ, SB_TK), F32),
                        pltpu.VMEM((SB_HP, SB_TQ, SB_TK), BF16), pltpu.VMEM((SB_HP, SB_TQ, 1), F32),
                        pltpu.VMEM((SB_HP, SB_TQ, C_DIM), F32)],
        compiler_params=_params("arbitrary", "arbitrary", "arbitrary"),
        name="sb_prompt",
    )(q, k, v)


def _sb_sample_kernel(q_ref, kn_ref, vn_ref, kc_ref, vc_ref, o_ref, z_ref, w_ref, carry_ref, acc_ref):
    jj = pl.program_id(1)
    n = DEC_SEQ
    tk = kc_ref.shape[0] // C_HEADS
    scale = np.float32(C_DIM ** -0.5)
    col = lambda h: slice(h * C_DIM, (h + 1) * C_DIM)

    def all_heads(k_of, v_of, width, mask):
        for h in range(C_HEADS):
            z_ref[h * n:(h + 1) * n, 0:width] = _dot_nt(q_ref[:, col(h)], k_of(h)) * scale
        z = z_ref[:, 0:width]
        if mask is not None:
            z = jnp.where(mask, z, NEG)
        w, carry = _sb_weights(z, carry_ref[...], _upper(width))
        carry_ref[...] = carry
        w_ref[:, 0:width] = w.astype(BF16)
        for h in range(C_HEADS):
            acc_ref[:, col(h)] += _dot(w_ref[h * n:(h + 1) * n, 0:width], v_of(h))

    @pl.when(jj == 0)
    def _():
        carry_ref[...] = jnp.zeros_like(carry_ref)
        acc_ref[...] = jnp.zeros_like(acc_ref)
        r_i = lax.broadcasted_iota(jnp.int32, (C_HEADS * n, n), 0) % n
        c_i = lax.broadcasted_iota(jnp.int32, (C_HEADS * n, n), 1)
        all_heads(lambda h: kn_ref[:, col(h)], lambda h: vn_ref[:, col(h)], n, c_i < r_i)

    all_heads(lambda h: kc_ref[pl.ds(h, tk, stride=C_HEADS), :].astype(BF16),
              lambda h: vc_ref[pl.ds(h, tk, stride=C_HEADS), :].astype(BF16), tk, None)

    @pl.when(jj == pl.num_programs(1) - 1)
    def _():
        o_ref[...] = acc_ref[...].astype(o_ref.dtype)


def _sb_sample(q, k, v, cache_k, cache_v, layer):
    n = DEC_SEQ
    off = BATCH * SEQ // n
    past = cache_k.shape[2] // C_HEADS
    tk = _tile(past, SB_SAMPLE_TK)
    n_kb = past // tk
    row_spec = pl.BlockSpec((n, C_WIDTH), lambda b, j: (b + off, 0))
    cache_spec = pl.BlockSpec((None, None, tk * C_HEADS, C_DIM), lambda b, j: (layer, b, n_kb - 1 - j, 0))
    return pl.pallas_call(
        _sb_sample_kernel,
        grid=(DEC_BATCH, n_kb),
        in_specs=[row_spec, row_spec, row_spec, cache_spec, cache_spec],
        out_specs=pl.BlockSpec((n, C_WIDTH), lambda b, j: (b, 0)),
        out_shape=jax.ShapeDtypeStruct((DEC_BATCH * n, C_WIDTH), BF16),
        scratch_shapes=[pltpu.VMEM((C_HEADS * n, tk), F32), pltpu.VMEM((C_HEADS * n, tk), BF16),
                        pltpu.VMEM((C_HEADS * n, 1), F32), pltpu.VMEM((n, C_WIDTH), F32)],
        compiler_params=_params("arbitrary", "arbitrary"),
        name="sb_sample",
    )(q, k, v, cache_k, cache_v)


def _pack_bf16_pairs(x):
    half = x.shape[1] // 2
    bits = lax.bitcast_convert_type(x.astype(BF16).astype(F32), jnp.uint32)
    return (bits[:, :half] & jnp.uint32(0xFFFF0000)) | (bits[:, half:] >> 16)


def _unpack_bf16_pairs(w):
    hi = lax.bitcast_convert_type(w & jnp.uint32(0xFFFF0000), F32)
    lo = lax.bitcast_convert_type(w << 16, F32)
    return jnp.concatenate([hi, lo], axis=1).astype(BF16)


def _outproj_kernel(*refs, n_lhs, n_prompt_tiles):
    lhs = refs[:2 * n_lhs]
    w_ref, x_ref, g_ref, b_ref, wr_ref, x1_ref, x1p_ref, lg_ref = refs[2 * n_lhs:]
    is_prompt = pl.program_id(0) < n_prompt_tiles
    tm = x_ref.shape[0]
    half = tm // 2
    for rows in (slice(0, half), slice(half, tm)):
        k0 = 0
        y = None
        for a_p, a_s in zip(lhs[0::2], lhs[1::2]):
            kk = a_p.shape[1]
            part = _dot(jnp.where(is_prompt, a_p[rows, :], a_s[rows, :]), w_ref[k0:k0 + kk, :])
            y = part if y is None else y + part
            k0 += kk
        x1 = _layer_norm(np.float32(ALPHA) * x_ref[rows, :] + y, g_ref[...], b_ref[...])
        x1_ref[rows, :] = x1
        x1p_ref[rows, :] = _pack_bf16_pairs(x1)
        lg_ref[:, rows] = lax.dot_general(wr_ref[...], x1, (((1,), (1,)), ((), ())), preferred_element_type=F32,
                                          precision=lax.Precision.HIGHEST)


def _outproj_ln(lhs, w, x, g, b, w_router_t, name):
    t = x.shape[0]
    tm = MOE_TM
    n_pt = lhs[0][0].shape[0] // tm
    assert all(a_p.shape[0] == n_pt * tm and a_s.shape[0] == t - n_pt * tm for a_p, a_s in lhs)
    row = lambda a: pl.BlockSpec((tm, a.shape[1]), lambda i: (i, 0))
    full = lambda a: pl.BlockSpec(a.shape, lambda i: (0,) * a.ndim)
    lhs_specs, lhs_args = [], []
    for a_p, a_s in lhs:
        lhs_specs += [pl.BlockSpec((tm, a_p.shape[1]), lambda i: (jnp.minimum(i, n_pt - 1), 0)),
                      pl.BlockSpec((tm, a_s.shape[1]), lambda i: (jnp.maximum(i - n_pt, 0), 0))]
        lhs_args += [a_p, a_s]
    return pl.pallas_call(
        functools.partial(_outproj_kernel, n_lhs=len(lhs), n_prompt_tiles=n_pt),
        grid=(t // tm,),
        in_specs=lhs_specs + [full(w), row(x), full(g), full(b), full(w_router_t)],
        out_specs=[pl.BlockSpec((tm, D_MODEL), lambda i: (i, 0)), pl.BlockSpec((tm, D_MODEL // 2), lambda i: (i, 0)),
                   pl.BlockSpec((N_EXPERTS, tm), lambda i: (0, i))],
        out_shape=[jax.ShapeDtypeStruct((t, D_MODEL), F32), jax.ShapeDtypeStruct((t, D_MODEL // 2), jnp.uint32),
                   jax.ShapeDtypeStruct((N_EXPERTS, t), F32)],
        compiler_params=_params("arbitrary"),
        name=name,
    )(*lhs_args, w, x, g, b, w_router_t)


MOE_TM = 256
EXPERT_BM = 256


def _route_select(lg, bias):
    e, tm = lg.shape
    ninf = np.float32(-np.inf)
    shift = int(np.log2(E_PER_GROUP))
    scores = jax.nn.sigmoid(lg)
    sel = scores + bias
    e_id = lax.broadcasted_iota(jnp.int32, (e, tm), 0)
    g_id = lax.shift_right_logical(e_id, shift)
    g3 = sel.reshape(N_GROUPS, E_PER_GROUP, tm)
    i3 = lax.broadcasted_iota(jnp.int32, g3.shape, 1)
    m1 = g3.max(axis=1, keepdims=True)
    first = jnp.where(g3 == m1, i3, E_PER_GROUP).min(axis=1, keepdims=True)
    m2 = jnp.where(i3 == first, ninf, g3).max(axis=1, keepdims=True)
    grp = jnp.broadcast_to(m1 + m2, g3.shape).reshape(e, tm)
    gsel = jnp.zeros((e, tm), jnp.int32)
    for _ in range(TOPK_GROUPS):
        m = grp.max(axis=0, keepdims=True)
        first = jnp.where(grp == m, e_id, e).min(axis=0, keepdims=True)
        chosen = g_id == lax.shift_right_logical(first, shift)
        gsel = jnp.where(chosen, 1, gsel)
        grp = jnp.where(chosen, ninf, grp)
    sel = jnp.where(gsel > 0, sel, NEG)
    picks = []
    for _ in range(TOP_K):
        m = sel.max(axis=0, keepdims=True)
        first = jnp.where(sel == m, e_id, e).min(axis=0, keepdims=True)
        oh = e_id == first
        picks.append(oh)
        sel = jnp.where(oh, ninf, sel)
    return scores, picks


def _route_kernel(lg_ref, bias_ref, dest_ref, gate_ref, cnt_ref, pad_ref, counts_ref, pads_ref, run_ref):
    phase = pl.program_id(0)
    i = pl.program_id(1)
    e, tm = lg_ref.shape
    scores, picks = _route_select(lg_ref[...], bias_ref[...])
    mask = jnp.zeros((e, tm), F32)
    for oh in picks:
        mask = jnp.where(oh, 1.0, mask)

    @pl.when((phase == 0) & (i == 0))
    def _():
        counts_ref[...] = jnp.zeros_like(counts_ref)

    @pl.when(phase == 0)
    def _():
        counts_ref[...] += mask.sum(axis=1, keepdims=True)

    @pl.when((phase == 1) & (i == 0))
    def _():
        counts = counts_ref[...]
        padded = jnp.ceil(counts * np.float32(1.0 / EXPERT_BM)) * np.float32(EXPERT_BM)
        r_i = lax.broadcasted_iota(jnp.int32, (e, e), 0)
        c_i = lax.broadcasted_iota(jnp.int32, (e, e), 1)
        before = (c_i < r_i).astype(F32)
        starts = jnp.dot(before, jnp.broadcast_to(padded, (e, LANES)), preferred_element_type=F32,
                         precision=lax.Precision.HIGHEST)
        pads_ref[...] = starts[:, 0:1]
        run_ref[...] = jnp.zeros_like(run_ref)
        cnt_ref[...] = jnp.broadcast_to(counts, cnt_ref.shape)
        pad_ref[...] = starts

    @pl.when(phase == 1)
    def _():
        r_i = lax.broadcasted_iota(jnp.int32, (tm, tm), 0)
        c_i = lax.broadcasted_iota(jnp.int32, (tm, tm), 1)
        earlier = (r_i < c_i).astype(BF16)
        slot = pads_ref[...] + run_ref[...] + _dot(mask.astype(BF16), earlier)
        run_ref[...] += mask.sum(axis=1, keepdims=True)
        k_i = lax.broadcasted_iota(jnp.int32, (TOP_K, tm), 0)
        dest = jnp.zeros((TOP_K, tm), F32)
        gate = jnp.zeros((TOP_K, tm), F32)
        total = jnp.zeros((1, tm), F32)
        for k, oh in enumerate(picks):
            d_k = jnp.where(oh, slot, 0.0).sum(axis=0, keepdims=True)
            g_k = jnp.where(oh, scores, 0.0).sum(axis=0, keepdims=True)
            total = total + g_k
            dest = jnp.where(k_i == k, d_k, dest)
            gate = jnp.where(k_i == k, g_k, gate)
        dest_ref[...] = dest.astype(jnp.int32)
        gate_ref[...] = gate / total * np.float32(ROUTED_SCALE)


def _route(lg_t, bias):
    e, t = lg_t.shape
    tm = MOE_TM
    n_t = t // tm
    tile = pl.BlockSpec((TOP_K, tm), lambda p, i: (0, i * p))
    meta = pl.BlockSpec((e, LANES), lambda p, i: (0, 0))
    return pl.pallas_call(
        _route_kernel,
        grid=(2, n_t),
        in_specs=[pl.BlockSpec((e, tm), lambda p, i: (0, i)), pl.BlockSpec((e, 1), lambda p, i: (0, 0))],
        out_specs=[tile, tile, meta, meta],
        out_shape=[jax.ShapeDtypeStruct((TOP_K, t), jnp.int32), jax.ShapeDtypeStruct((TOP_K, t), F32),
                   jax.ShapeDtypeStruct((e, LANES), F32), jax.ShapeDtypeStruct((e, LANES), F32)],
        scratch_shapes=[pltpu.VMEM((e, 1), F32), pltpu.VMEM((e, 1), F32), pltpu.VMEM((e, 1), F32)],
        compiler_params=_params("arbitrary", "arbitrary"),
        name="route",
    )(lg_t, bias)


def _row_copy(src, src_row, dst, dst_row, sem):
    return pltpu.make_async_copy(src.at[pl.ds(src_row, 1)], dst.at[pl.ds(dst_row, 1)], sem)


def _scatter_kernel(last_ref, nu_ref, dest_ref, x_ref, xs_out, zeros_ref, sem, zsem):
    i = pl.program_id(0)
    tm = x_ref.shape[0]
    n_blocks = xs_out.shape[0] // EXPERT_BM

    @pl.when(i == 0)
    def _():
        zeros_ref[...] = jnp.zeros_like(zeros_ref)

        def zero_copy(row):
            return pltpu.make_async_copy(zeros_ref, xs_out.at[pl.ds(pl.multiple_of(row, EXPERT_BM), EXPERT_BM)], zsem)

        def tail_row(b):
            return (nu_ref[0] + b) * EXPERT_BM

        def each(fn, n, act):
            def body(b, c):
                row = fn(b)

                @pl.when(row >= 0)
                def _():
                    act(zero_copy(row))
                return c
            lax.fori_loop(0, n, body, 0)

        start = lambda fn, n: each(fn, n, lambda c: c.start())
        wait = lambda fn, n: each(fn, n, lambda c: c.wait())

        start(lambda e: last_ref[e], N_EXPERTS)
        wait(lambda e: last_ref[e], N_EXPERTS)
        start(tail_row, n_blocks - nu_ref[0])
        wait(tail_row, n_blocks - nu_ref[0])

    def issue(t, c):
        for k in range(TOP_K):
            _row_copy(x_ref, t, xs_out, dest_ref[k * tm + t], sem).start(priority=k % 2)
        return c

    def drain(t, c):
        for k in range(TOP_K):
            _row_copy(x_ref, t, xs_out, dest_ref[k * tm + t], sem).wait()
        return c

    lax.fori_loop(0, tm, issue, 0)
    lax.fori_loop(0, tm, drain, 0)


def _scatter(last_row, n_used, dest_flat, x1p, n_slots):
    t, w = x1p.shape
    tm = MOE_TM
    grid_spec = pltpu.PrefetchScalarGridSpec(
        num_scalar_prefetch=2,
        grid=(t // tm,),
        in_specs=[pl.BlockSpec((TOP_K * tm,), lambda i, lr, nu: (i,), memory_space=pltpu.SMEM),
                  pl.BlockSpec((tm, w), lambda i, lr, nu: (i, 0))],
        out_specs=pl.BlockSpec(memory_space=pl.ANY),
        scratch_shapes=[pltpu.VMEM((EXPERT_BM, w), x1p.dtype), pltpu.SemaphoreType.DMA(()),
                        pltpu.SemaphoreType.DMA(())],
    )
    return pl.pallas_call(
        _scatter_kernel,
        grid_spec=grid_spec,
        out_shape=jax.ShapeDtypeStruct((n_slots, w), x1p.dtype),
        compiler_params=_params("arbitrary"),
        name="scatter",
    )(last_row, n_used, dest_flat, x1p)


def _expert_kernel(be_ref, nu_ref, slot_ref, nxt_ref, x_ref, wg_hbm, wu_hbm, wd_hbm, y_ref,
                   wgf, wuf, wdf, wgb, wub, wdb, sem, *, layer):
    i = pl.program_id(0)
    active = i < nu_ref[0]
    first = (i == 0) | (be_ref[i] != be_ref[jnp.maximum(i - 1, 0)])

    def weight_copies(e, s):
        return [pltpu.make_async_copy(src.at[layer, e], dst.at[s], sem.at[s])
                for src, dst in ((wg_hbm, wgf), (wu_hbm, wuf), (wd_hbm, wdf))]

    @pl.when(active & (i == 0))
    def _():
        for c in weight_copies(be_ref[0], slot_ref[0]):
            c.start()

    @pl.when(active & first)
    def _():
        s = slot_ref[i]
        for c in weight_copies(be_ref[i], s):
            c.wait()
        wgb[...] = wgf[s].astype(BF16)
        wub[...] = wuf[s].astype(BF16)
        wdb[...] = wdf[s].astype(BF16)

        @pl.when(nxt_ref[i] >= 0)
        def _():
            for c in weight_copies(nxt_ref[i], 1 - s):
                c.start()

    @pl.when(active)
    def _():
        x = _unpack_bf16_pairs(x_ref[...])
        g = _dot(x, wgb[...])
        h = (g * jax.nn.sigmoid(g)) * _dot(x, wub[...])
        y_ref[...] = _pack_bf16_pairs(_dot(h.astype(BF16), wdb[...]))

    @pl.when(jnp.logical_not(active))
    def _():
        y_ref[...] = jnp.zeros_like(y_ref)


def _experts(block_e, n_used, slot, nxt, xs, layer, wg, wu, wd):
    n_slots = xs.shape[0]
    n_blocks = n_slots // EXPERT_BM
    ff = wg.shape[3]
    any_spec = pl.BlockSpec(memory_space=pl.ANY)
    grid_spec = pltpu.PrefetchScalarGridSpec(
        num_scalar_prefetch=4,
        grid=(n_blocks,),
        in_specs=[pl.BlockSpec((EXPERT_BM, D_MODEL // 2), lambda i, be, nu, sl, nx: (jnp.minimum(i, nu[0] - 1), 0)),
                  any_spec, any_spec, any_spec],
        out_specs=pl.BlockSpec((EXPERT_BM, D_MODEL // 2), lambda i, be, nu, sl, nx: (i, 0)),
        scratch_shapes=[pltpu.VMEM((2, D_MODEL, ff), F32), pltpu.VMEM((2, D_MODEL, ff), F32),
                        pltpu.VMEM((2, ff, D_MODEL), F32),
                        pltpu.VMEM((D_MODEL, ff), BF16), pltpu.VMEM((D_MODEL, ff), BF16),
                        pltpu.VMEM((ff, D_MODEL), BF16), pltpu.SemaphoreType.DMA((2,))],
    )
    return pl.pallas_call(
        functools.partial(_expert_kernel, layer=layer),
        grid_spec=grid_spec,
        out_shape=jax.ShapeDtypeStruct((n_slots, D_MODEL // 2), jnp.uint32),
        compiler_params=_params("arbitrary"),
        name="experts",
    )(block_e, n_used, slot, nxt, xs, wg, wu, wd)


def _ffn_ln_kernel(dest_ref, xp_ref, x_ref, gate_ref, y_hbm, wg_ref, wu_ref, wd_ref, g_ref, b_ref, o_ref, buf, sem):
    tm = x_ref.shape[0]

    def issue(t, c):
        for k in range(TOP_K):
            _row_copy(y_hbm, dest_ref[k * tm + t], buf.at[k], t, sem).start(priority=k % 2)
        return c

    def drain(t, c):
        for k in range(TOP_K):
            _row_copy(y_hbm, dest_ref[k * tm + t], buf.at[k], t, sem).wait()
        return c

    lax.fori_loop(0, tm, issue, 0)
    xb = _unpack_bf16_pairs(xp_ref[...])
    gte = _dot(xb, wg_ref[...])
    h = (gte * jax.nn.sigmoid(gte)) * _dot(xb, wu_ref[...])
    shared = _dot(h.astype(BF16), wd_ref[...])
    lax.fori_loop(0, tm, drain, 0)
    lo_half = hi_half = None
    for k in range(TOP_K):
        w = buf[k]
        g = gate_ref[:, k:k + 1]
        hi = lax.bitcast_convert_type(w & jnp.uint32(0xFFFF0000), F32) * g
        lo = lax.bitcast_convert_type(w << 16, F32) * g
        hi_half = hi if hi_half is None else hi_half + hi
        lo_half = lo if lo_half is None else lo_half + lo
    routed = jnp.concatenate([hi_half, lo_half], axis=1)
    z = np.float32(ALPHA) * x_ref[...] + (routed + shared)
    o_ref[...] = _layer_norm(z, g_ref[...], b_ref[...])


def _ffn_ln(dest_flat, x1p, x1, gate_t, y, wg, wu, wd, g, b):
    t = x1.shape[0]
    tm = MOE_TM
    row = lambda a: pl.BlockSpec((tm, a.shape[1]), lambda i: (i, 0))
    full = lambda a: pl.BlockSpec(a.shape, lambda i: (0,) * a.ndim)
    return pl.pallas_call(
        _ffn_ln_kernel,
        grid=(t // tm,),
        in_specs=[pl.BlockSpec((TOP_K * tm,), lambda i: (i,), memory_space=pltpu.SMEM),
                  row(x1p), row(x1), row(gate_t), pl.BlockSpec(memory_space=pl.ANY),
                  full(wg), full(wu), full(wd), full(g), full(b)],
        out_specs=row(x1),
        out_shape=jax.ShapeDtypeStruct((t, D_MODEL), F32),
        scratch_shapes=[pltpu.VMEM((TOP_K, tm, D_MODEL // 2), jnp.uint32), pltpu.SemaphoreType.DMA(())],
        compiler_params=_params("arbitrary"),
        name="ffn_ln",
    )(dest_flat, x1p, x1, gate_t, y, wg, wu, wd, g, b)


def _ple_kernel(x_ref, p_ref, wp_ref, wg_ref, *o_refs, n_prompt_tiles):
    x = x_ref[...]
    gate = jax.nn.sigmoid(_dot(x.astype(BF16), wg_ref[...]))
    emb = _dot(p_ref[...].astype(BF16), wp_ref[...])
    out = x + emb * gate
    if len(o_refs) == 1:
        o_refs[0][...] = out
    else:
        i = pl.program_id(0)

        @pl.when(i < n_prompt_tiles)
        def _():
            o_refs[0][...] = out

        @pl.when(i >= n_prompt_tiles)
        def _():
            o_refs[1][...] = out


def _ple(x2, p, w_ple, w_gate, n_prompt=None):
    t = x2.shape[0]
    tm = _tile(t if n_prompt is None else np.gcd(n_prompt, t - n_prompt), 256)
    n_pt = None if n_prompt is None else n_prompt // tm
    row = lambda a: pl.BlockSpec((tm, a.shape[1]), lambda i: (i, 0))
    full = lambda a: pl.BlockSpec(a.shape, lambda i: (0,) * a.ndim)
    if n_prompt is None:
        out_specs = [row(x2)]
        out_shape = [jax.ShapeDtypeStruct((t, D_MODEL), F32)]
    else:
        out_specs = [pl.BlockSpec((tm, D_MODEL), lambda i: (jnp.minimum(i, n_pt - 1), 0)),
                     pl.BlockSpec((tm, D_MODEL), lambda i: (jnp.maximum(i - n_pt, 0), 0))]
        out_shape = [jax.ShapeDtypeStruct((n_prompt, D_MODEL), F32),
                     jax.ShapeDtypeStruct((t - n_prompt, D_MODEL), F32)]
    return pl.pallas_call(
        functools.partial(_ple_kernel, n_prompt_tiles=n_pt),
        grid=(t // tm,),
        in_specs=[row(x2), row(p), full(w_ple), full(w_gate)],
        out_specs=out_specs, out_shape=out_shape,
        compiler_params=_params("arbitrary"),
        name="ple",
    )(x2, p, w_ple, w_gate)


def _rel_bias(table, qpos, kpos):
    idx = np.clip(qpos[:, None] - kpos[None, :], -REL_CLIP, REL_CLIP) + REL_CLIP
    return table[:, idx]


def _band_prompt_bias(table):
    tq, width = BAND_TQ, BAND_NWB * BAND_TQ
    span = width + tq - 1
    rel = (BAND_NWB - 1) * tq + tq - 1 - np.arange(span)
    strip = table[:, np.clip(rel, -REL_CLIP, REL_CLIP) + REL_CLIP]
    u = jnp.concatenate([strip[:, tq - 1:], strip[:, :1], strip[:, :tq - 1]], axis=1)
    flat = jnp.tile(u, (1, tq))[:, :tq * span]
    return flat.reshape(table.shape[0], tq, span)[:, :, :width]


def _band_mask(qpos, kpos):
    dc = qpos // CHUNK - kpos // CHUNK
    return (kpos >= 0) & (dc >= 0) & (dc <= B_PREV_CHUNKS)


def _post_block(x, lhs, w_out, p, ln1_g, ln1_b, ln2_g, ln2_b, w_router, b_router,
                layer, w_gate, w_up, w_down, ws_gate, ws_up, ws_down, w_ple, w_ple_gate, name, n_prompt):
    row = lambda a: a.reshape(1, -1)
    t = x.shape[0]
    x1, x1p, lg_t = _outproj_ln(lhs, w_out, x, row(ln1_g), row(ln1_b), w_router.T, name)
    dest, gate, cnt, pad = _route(lg_t, b_router.astype(F32).reshape(N_EXPERTS, 1))

    n_blocks = -(-(t * TOP_K + N_EXPERTS * (EXPERT_BM - 1)) // EXPERT_BM)
    counts = cnt[:, 0].astype(jnp.int32)
    pad_end = pad[:, 0].astype(jnp.int32) + (counts + EXPERT_BM - 1) // EXPERT_BM * EXPERT_BM
    block_e = jnp.minimum((pad_end[:, None] <= jnp.arange(n_blocks)[None, :] * EXPERT_BM).sum(axis=0),
                          N_EXPERTS - 1).astype(jnp.int32)
    n_used = (pad_end[-1:] // EXPERT_BM).astype(jnp.int32)
    nonempty = counts > 0
    e_ids = jnp.arange(N_EXPERTS)
    later_ids = jnp.where(nonempty[None, :] & (e_ids[None, :] > e_ids[:, None]), e_ids[None, :], N_EXPERTS).min(axis=1)
    nxt_of = jnp.where(later_ids < N_EXPERTS, later_ids, -1)
    slot_of = (jnp.cumsum(nonempty) - 1) % 2
    in_block = (block_e[:, None] == e_ids[None, :]).astype(jnp.int32)
    nxt = (in_block * nxt_of[None, :]).sum(axis=1).astype(jnp.int32)
    slot = (in_block * slot_of[None, :]).sum(axis=1).astype(jnp.int32)
    dest_flat = dest.reshape(TOP_K, t // MOE_TM, MOE_TM).transpose(1, 0, 2).reshape(-1)

    last_row = jnp.where(counts > 0, pad_end - EXPERT_BM, -1).astype(jnp.int32)
    xs = _scatter(last_row, n_used, dest_flat, x1p, n_blocks * EXPERT_BM)
    y = _experts(block_e, n_used, slot, nxt, xs, layer, w_gate, w_up, w_down)
    x2 = _ffn_ln(dest_flat, x1p, x1, gate.T, y, ws_gate, ws_up, ws_down, row(ln2_g), row(ln2_b))
    return _ple(x2, p, w_ple, w_ple_gate, n_prompt)


def kernel(x_prompt, x_sample, cache_b_k, cache_b_v, cache_c_k, cache_c_v, p_prompt, p_sample, w_in_ab, w_out_ab, sgu_w, sgu_b, sgu_ln_g, sgu_ln_b, rel_bias_tab, w_in_c, w_out_c, ln_mix_g, ln_mix_b, ln_ffn_g, ln_ffn_b, w_router, b_router, w_gate, w_up, w_down, ws_gate, ws_up, ws_down, w_ple, w_ple_gate):
    n_p = BATCH * SEQ
    n_s = DEC_BATCH * DEC_SEQ
    x = jnp.concatenate([x_prompt.reshape(n_p, D_MODEL), x_sample.reshape(n_s, D_MODEL)], axis=0)
    p_all = jnp.concatenate([p_prompt.reshape(DEPTH, n_p, PLE_DIM), p_sample.reshape(DEPTH, n_s, PLE_DIM)], axis=1)
    bf = lambda a: a.astype(BF16)
    outs = {k: [] for k in ("bk_p", "bv_p", "bk_s", "bv_s", "av_s", "ck_p", "cv_p", "ck_s", "cv_s")}
    kv_outs = ((F32, True), (BF16, False))

    for i in range(DEPTH):
        j = i // 2
        if i % 2 == 0:
            plain = lambda d: ((d, False),)
            hu, hv, q, k_p, k_s, kb16, v_p, v_s, vb16 = _proj(
                x, bf(w_in_ab[j]), 0,
                [(0, A_WIDTH, plain(F32)), (A_WIDTH, A_WIDTH, plain(F32)), (2 * A_WIDTH, B_WIDTH, plain(BF16)),
                 (2 * A_WIDTH + B_WIDTH, B_WIDTH, kv_outs), (2 * A_WIDTH + 2 * B_WIDTH, B_WIDTH, kv_outs)],
                "proj_ab", n_p)

            bs_t = sgu_b[j].T
            a_p, = _sgu(hu, hv, sgu_w[j], bs_t, sgu_ln_g[j], sgu_ln_b[j], A_CHUNK, 0, n_p // A_CHUNK, False)
            a_s, va_s = _sgu(hu, hv, sgu_w[j][:, :DEC_SEQ, :DEC_SEQ], bs_t[:DEC_SEQ], sgu_ln_g[j], sgu_ln_b[j],
                             DEC_SEQ, n_p, DEC_BATCH, True)

            b_p = _band_prompt(q, kb16, vb16, _band_prompt_bias(rel_bias_tab[j]))
            n_cache = cache_b_k.shape[2]
            qs = PAST_LEN + np.arange(DEC_SEQ)
            kc = PAST_LEN - n_cache + np.arange(n_cache)
            b_s = _band_sample(
                q, kb16, vb16, cache_b_k.reshape(N_EVEN, DEC_BATCH, n_cache * B_HEADS, B_DIM),
                cache_b_v.reshape(N_EVEN, DEC_BATCH, n_cache * B_HEADS, B_DIM), j,
                _rel_bias(rel_bias_tab[j], qs, kc), _rel_bias(rel_bias_tab[j], qs, qs),
                jnp.asarray(_band_mask(qs[:, None], kc[None, :]), F32),
                jnp.asarray(_band_mask(qs[:, None], qs[None, :]), F32))

            keep = min(B_WINDOW, SEQ)
            outs["bk_p"].append(k_p.reshape(BATCH, SEQ, B_HEADS, B_DIM)[:, SEQ - keep:])
            outs["bv_p"].append(v_p.reshape(BATCH, SEQ, B_HEADS, B_DIM)[:, SEQ - keep:])
            outs["bk_s"].append(k_s.reshape(DEC_BATCH, DEC_SEQ, B_HEADS, B_DIM))
            outs["bv_s"].append(v_s.reshape(DEC_BATCH, DEC_SEQ, B_HEADS, B_DIM))
            outs["av_s"].append(va_s.reshape(DEC_BATCH, DEC_SEQ, A_HEADS, A_DIM))
            lhs, w_out, name = [(a_p, a_s), (b_p, b_s)], bf(w_out_ab[j]), "outproj_ab"
        else:
            w_in = bf(w_in_c[j])
            q, k_p, k_s, kb16 = _proj(x, w_in, 0, [(0, C_WIDTH, ((BF16, False),)), (C_WIDTH, C_WIDTH, kv_outs)],
                                      "proj_cqk", n_p)
            v_p, v_s, vb16 = _proj(x, w_in, 2, [(0, C_WIDTH, kv_outs)], "proj_cv", n_p)
            o_p = _sb_prompt(q, kb16, vb16)
            rows = lambda c: c.reshape(N_ODD, DEC_BATCH, PAST_LEN * C_HEADS, C_DIM)
            o_s = _sb_sample(q, kb16, vb16, rows(cache_c_k), rows(cache_c_v), j)
            outs["ck_p"].append(k_p.reshape(BATCH, SEQ, C_HEADS, C_DIM))
            outs["cv_p"].append(v_p.reshape(BATCH, SEQ, C_HEADS, C_DIM))
            outs["ck_s"].append(k_s.reshape(DEC_BATCH, DEC_SEQ, C_HEADS, C_DIM))
            outs["cv_s"].append(v_s.reshape(DEC_BATCH, DEC_SEQ, C_HEADS, C_DIM))
            lhs, w_out, name = [(o_p, o_s)], bf(w_out_c[j]), "outproj_c"

        x = _post_block(x, lhs, w_out, p_all[i], ln_mix_g[i], ln_mix_b[i], ln_ffn_g[i], ln_ffn_b[i],
                        w_router[i], b_router[i], i, w_gate, w_up, w_down,
                        bf(ws_gate[i]), bf(ws_up[i]), bf(ws_down[i]), bf(w_ple[i]), bf(w_ple_gate[i]), name,
                        n_p if i == DEPTH - 1 else None)
        if i < DEPTH - 1:
            x, = x

    y_p, y_s = x
    st = lambda key: jnp.stack(outs[key])
    return (y_p.reshape(BATCH, SEQ, D_MODEL), y_s.reshape(DEC_BATCH, DEC_SEQ, D_MODEL),
            st("bk_p"), st("bv_p"), st("bk_s"), st("bv_s"), st("av_s"),
            st("ck_p"), st("cv_p"), st("ck_s"), st("cv_s"))
```

```python
import functools

import numpy as np
import jax
import jax.numpy as jnp
from jax import lax
from jax.experimental import pallas as pl
from jax.experimental.pallas import tpu as pltpu

D_MODEL = 2048
BATCH = 2
SEQ = 4096
DEPTH = 2
DEC_BATCH = 32
DEC_SEQ = 16
PAST_LEN = 2048

CHUNK = 64
N_EVEN = (DEPTH + 1) // 2
N_ODD = DEPTH // 2
A_CHUNK = 128
A_HEADS = 8
A_DIM = 128
A_WIDTH = A_HEADS * A_DIM
B_HEADS = 8
B_DIM = 128
B_WIDTH = B_HEADS * B_DIM
B_PREV_CHUNKS = 8
B_WINDOW = B_PREV_CHUNKS * CHUNK
REL_CLIP = 128
C_HEADS = 16
C_DIM = 128
C_WIDTH = C_HEADS * C_DIM
N_EXPERTS = 64
N_GROUPS = 8
E_PER_GROUP = N_EXPERTS // N_GROUPS
TOPK_GROUPS = 4
TOP_K = 8
EXPERT_FF = 512
SHARED_FF = 512
ROUTED_SCALE = 2.5
PLE_DIM = 256
LN_EPS = 1e-5
ALPHA = (2 * DEPTH) ** 0.25
NEG = -1e9

LANES = 128
VMEM_LIMIT_BYTES = 56 * 1024 * 1024

BF16 = jnp.bfloat16
F32 = jnp.float32


def _params(*sem):
    return pltpu.CompilerParams(dimension_semantics=sem, vmem_limit_bytes=VMEM_LIMIT_BYTES)


def _tile(n, pref):
    if n <= pref:
        return n
    for t in range(pref, 7, -1):
        if n % t == 0 and t % 8 == 0:
            return t
    return n


def _dot(a, b):
    return jnp.dot(a, b, preferred_element_type=F32)


def _dot_nt(a, b):
    return lax.dot_general(a, b, (((1,), (1,)), ((), ())), preferred_element_type=F32)


def _layer_norm(z, g, b):
    mu = jnp.mean(z, axis=-1, keepdims=True)
    zc = z - mu
    var = jnp.mean(zc * zc, axis=-1, keepdims=True)
    return zc * lax.rsqrt(var + LN_EPS) * g + b


def _gelu(x):
    return x * (lax.erf(x * np.float32(1.0 / np.sqrt(2.0))) + 1.0) * 0.5


def _softplus(z):
    return jnp.maximum(z, 0.0) + jnp.log(1.0 + jnp.exp(-jnp.abs(z)))


def _proj_kernel(x_ref, w_ref, *o_refs, segs, n_prompt_tiles):
    xb = x_ref[...].astype(BF16)
    i = pl.program_id(0)
    refs = iter(o_refs)
    for col0, ncols, outs in segs:
        acc = _dot(xb, w_ref[:, col0:col0 + ncols])
        for _, is_split in outs:
            if is_split:
                o_p, o_s = next(refs), next(refs)

                @pl.when(i < n_prompt_tiles)
                def _(o_p=o_p, acc=acc):
                    o_p[...] = acc.astype(o_p.dtype)

                @pl.when(i >= n_prompt_tiles)
                def _(o_s=o_s, acc=acc):
                    o_s[...] = acc.astype(o_s.dtype)
            else:
                o = next(refs)
                o[...] = acc.astype(o.dtype)


def _proj(x, w, col_block, segs, name, n_prompt):
    m, k = x.shape
    width = sum(n for _, n, _ in segs)
    tm = _tile(np.gcd(n_prompt, m - n_prompt), 256)
    n_pt = n_prompt // tm
    out_specs, out_shape = [], []
    for _, ncols, outs in segs:
        for dtype, is_split in outs:
            if is_split:
                out_specs += [pl.BlockSpec((tm, ncols), lambda i: (jnp.minimum(i, n_pt - 1), 0)),
                              pl.BlockSpec((tm, ncols), lambda i: (jnp.maximum(i - n_pt, 0), 0))]
                out_shape += [jax.ShapeDtypeStruct((n_prompt, ncols), dtype),
                              jax.ShapeDtypeStruct((m - n_prompt, ncols), dtype)]
            else:
                out_specs.append(pl.BlockSpec((tm, ncols), lambda i: (i, 0)))
                out_shape.append(jax.ShapeDtypeStruct((m, ncols), dtype))
    return pl.pallas_call(
        functools.partial(_proj_kernel, segs=tuple(segs), n_prompt_tiles=n_pt),
        grid=(m // tm,),
        in_specs=[pl.BlockSpec((tm, k), lambda i: (i, 0)),
                  pl.BlockSpec((k, width), lambda i: (0, col_block), pipeline_mode=pl.Buffered(1))],
        out_specs=out_specs, out_shape=out_shape,
        compiler_params=_params("arbitrary"),
        name=name,
    )(x, w)


def _sgu_kernel(hu_ref, hv_ref, w_ref, bs_ref, g_ref, b_ref, a_ref, *v_out, emit_v):
    rows = hu_ref.shape[0]
    r_i = lax.broadcasted_iota(jnp.int32, (rows, rows), 0)
    c_i = lax.broadcasted_iota(jnp.int32, (rows, rows), 1)
    causal = c_i <= r_i
    for h in range(A_HEADS):
        sl = slice(h * A_DIM, (h + 1) * A_DIM)
        u = _gelu(hu_ref[:, sl])
        v = _layer_norm(_gelu(hv_ref[:, sl]), g_ref[h:h + 1, :], b_ref[h:h + 1, :])
        w = jnp.where(causal, w_ref[h], 0.0).astype(BF16)
        mix = _dot(w, v.astype(BF16)) + bs_ref[:, h:h + 1]
        a_ref[:, sl] = (u * mix).astype(a_ref.dtype)
        if emit_v:
            v_out[0][:, sl] = v


def _sgu(hu, hv, w_s, bs_t, g, b, rows, row0, n_chunks, emit_v):
    off = row0 // rows
    in_spec = pl.BlockSpec((rows, A_WIDTH), lambda c: (c + off, 0))
    out_spec = pl.BlockSpec((rows, A_WIDTH), lambda c: (c, 0))
    full = lambda a: pl.BlockSpec(a.shape, lambda c: (0,) * a.ndim)
    out_specs = [out_spec]
    out_shape = [jax.ShapeDtypeStruct((n_chunks * rows, A_WIDTH), BF16)]
    if emit_v:
        out_specs.append(out_spec)
        out_shape.append(jax.ShapeDtypeStruct((n_chunks * rows, A_WIDTH), F32))
    return pl.pallas_call(
        functools.partial(_sgu_kernel, emit_v=emit_v),
        grid=(n_chunks,),
        in_specs=[in_spec, in_spec, full(w_s), full(bs_t), full(g), full(b)],
        out_specs=out_specs, out_shape=out_shape,
        compiler_params=_params("arbitrary"),
        name="sgu_sample" if emit_v else "sgu_prompt",
    )(hu, hv, w_s, bs_t, g, b)


BAND_TQ = 2 * CHUNK
BAND_NWB = B_WINDOW // BAND_TQ + 1


BAND_HP = 8


def _band_prompt_kernel(q_ref, k_ref, v_ref, bias_ref, o_ref):
    t = pl.program_id(2)
    tq = BAND_TQ
    shift = int(np.log2(CHUNK))
    q_chunk = lax.shift_right_arithmetic(t * tq + lax.broadcasted_iota(jnp.int32, (tq, 1), 0), shift)
    lane = lax.broadcasted_iota(jnp.int32, (1, tq), 1)
    starts, valids = [], []
    for j in range(BAND_NWB):
        kb = t - (BAND_NWB - 1) + j
        starts.append(pl.multiple_of(jnp.maximum(kb, 0) * tq, tq))
        kpos = kb * tq + lane
        dc = q_chunk - lax.shift_right_arithmetic(kpos, shift)
        valids.append((kpos >= 0) & (dc >= 0) & (dc <= B_PREV_CHUNKS))
    for h in range(BAND_HP):
        sl = slice(h * B_DIM, (h + 1) * B_DIM)
        q = q_ref[:, sl]
        s_blocks = []
        for j in range(BAND_NWB):
            s = (_dot_nt(q, k_ref[pl.ds(starts[j], tq), sl]) * np.float32(B_DIM ** -0.5)
                 + bias_ref[h, :, j * tq:(j + 1) * tq])
            s_blocks.append(jnp.where(valids[j], s, NEG))
        m = s_blocks[0].max(axis=-1, keepdims=True)
        for s in s_blocks[1:]:
            m = jnp.maximum(m, s.max(axis=-1, keepdims=True))
        acc = jnp.zeros((tq, B_DIM), F32)
        den = jnp.zeros((tq, 1), F32)
        for j, s in enumerate(s_blocks):
            p = jnp.exp(s - m)
            den = den + p.sum(axis=-1, keepdims=True)
            acc = acc + _dot(p.astype(BF16), v_ref[pl.ds(starts[j], tq), sl])
        o_ref[:, sl] = (acc / den).astype(o_ref.dtype)


def _band_prompt(q, k, v, bias):
    assert B_HEADS % BAND_HP == 0
    t_all = BATCH * SEQ
    n_t = SEQ // BAND_TQ
    q_spec = pl.BlockSpec((BAND_TQ, BAND_HP * B_DIM), lambda b, h, t: (b * n_t + t, h))
    kv_spec = pl.BlockSpec((SEQ, BAND_HP * B_DIM), lambda b, h, t: (b, h))
    return pl.pallas_call(
        _band_prompt_kernel,
        grid=(BATCH, B_HEADS // BAND_HP, n_t),
        in_specs=[q_spec, kv_spec, kv_spec,
                  pl.BlockSpec((BAND_HP, BAND_TQ, BAND_NWB * BAND_TQ), lambda b, h, t: (h, 0, 0))],
        out_specs=q_spec,
        out_shape=jax.ShapeDtypeStruct((t_all, B_WIDTH), BF16),
        compiler_params=_params("arbitrary", "arbitrary", "arbitrary"),
        name="band_prompt",
    )(q, k, v, bias)


def _band_sample_kernel(q_ref, kn_ref, vn_ref, kc_ref, vc_ref, bc_ref, bn_ref, mc_ref, mn_ref, o_ref):
    scale = np.float32(B_DIM ** -0.5)
    n_cache = kc_ref.shape[0] // B_HEADS
    for h in range(B_HEADS):
        sl = slice(h * B_DIM, (h + 1) * B_DIM)
        q = q_ref[:, sl]
        s_c = _dot_nt(q, kc_ref[pl.ds(h, n_cache, stride=B_HEADS), :].astype(BF16)) * scale + bc_ref[h]
        s_n = _dot_nt(q, kn_ref[:, sl]) * scale + bn_ref[h]
        s_c = jnp.where(mc_ref[...] > 0, s_c, NEG)
        s_n = jnp.where(mn_ref[...] > 0, s_n, NEG)
        m = jnp.maximum(s_c.max(axis=-1, keepdims=True), s_n.max(axis=-1, keepdims=True))
        p_c = jnp.exp(s_c - m)
        p_n = jnp.exp(s_n - m)
        den = p_c.sum(axis=-1, keepdims=True) + p_n.sum(axis=-1, keepdims=True)
        acc = (_dot(p_c.astype(BF16), vc_ref[pl.ds(h, n_cache, stride=B_HEADS), :].astype(BF16))
               + _dot(p_n.astype(BF16), vn_ref[:, sl]))
        o_ref[:, sl] = (acc / den).astype(o_ref.dtype)


def _band_sample(q, k, v, cache_k, cache_v, layer, bias_c, bias_n, mask_c, mask_n):
    n = DEC_SEQ
    off = BATCH * SEQ // n
    n_cache = cache_k.shape[2] // B_HEADS
    row_spec = pl.BlockSpec((n, B_WIDTH), lambda b: (b + off, 0))
    cache_spec = pl.BlockSpec((None, None, n_cache * B_HEADS, B_DIM), lambda b: (layer, b, 0, 0))
    full = lambda a: pl.BlockSpec(a.shape, lambda b: (0,) * a.ndim)
    return pl.pallas_call(
        _band_sample_kernel,
        grid=(DEC_BATCH,),
        in_specs=[row_spec, row_spec, row_spec, cache_spec, cache_spec,
                  full(bias_c), full(bias_n), full(mask_c), full(mask_n)],
        out_specs=pl.BlockSpec((n, B_WIDTH), lambda b: (b, 0)),
        out_shape=jax.ShapeDtypeStruct((DEC_BATCH * n, B_WIDTH), BF16),
        compiler_params=_params("arbitrary"),
        name="band_sample",
    )(q, k, v, cache_k, cache_v, bias_c, bias_n, mask_c, mask_n)


SB_TQ = 256
SB_TK = 256
SB_SAMPLE_TK = 512
SB_HP = 4


def _sb_weights(z, carry, upper):
    sp = _softplus(z)
    hi = sp.astype(BF16)
    lo = (sp - hi.astype(F32)).astype(BF16)
    later = _dot(hi, upper) + _dot(lo, upper)
    w = jnp.exp((z - sp) - (carry + later))
    return w, carry + later[:, 0:1] + sp[:, 0:1]


def _upper(n):
    r_i = lax.broadcasted_iota(jnp.int32, (n, n), 0)
    c_i = lax.broadcasted_iota(jnp.int32, (n, n), 1)
    return (r_i > c_i).astype(BF16)


SB_ROWS = 32


def _sb_prompt_kernel(q_ref, k_ref, v_ref, o_ref, z_ref, ls_ref, hl_ref, lat_ref, w_ref, carry_ref, acc_ref):
    i = pl.program_id(2)
    tq, tk = SB_TQ, SB_TK
    upper2 = jnp.concatenate([_upper(tk)] * 2, axis=0)
    scale = np.float32(C_DIM ** -0.5)
    chunks = [slice(r, r + SB_ROWS) for r in range(0, tq, SB_ROWS)]
    r_i = lax.broadcasted_iota(jnp.int32, (SB_ROWS, tk), 0)
    c_i = lax.broadcasted_iota(jnp.int32, (SB_ROWS, tk), 1)

    def block(j, diagonal):
        start = pl.multiple_of(j * tk, tk)
        for h in range(SB_HP):
            sl = slice(h * C_DIM, (h + 1) * C_DIM)
            z_ref[h] = _dot_nt(q_ref[:, sl], k_ref[pl.ds(start, tk), sl])
        for h in range(SB_HP):
            for ci, rows in enumerate(chunks):
                z = z_ref[h, rows, :] * scale
                if diagonal:
                    z = jnp.where(c_i < r_i + ci * SB_ROWS, z, NEG)
                sp = _softplus(z)
                hi = sp.astype(BF16)
                ls_ref[h, rows, :] = z - sp
                hl_ref[h, rows, 0:tk] = hi
                hl_ref[h, rows, tk:2 * tk] = (sp - hi.astype(F32)).astype(BF16)
        for h in range(SB_HP):
            lat_ref[h] = _dot(hl_ref[h], upper2)
        for h in range(SB_HP):
            for rows in chunks:
                later = lat_ref[h, rows, :]
                carry = carry_ref[h, rows, :]
                w_ref[h, rows, :] = jnp.exp(ls_ref[h, rows, :] - (carry + later)).astype(BF16)
                sp0 = hl_ref[h, rows, 0:1].astype(F32) + hl_ref[h, rows, tk:tk + 1].astype(F32)
                carry_ref[h, rows, :] = carry + later[:, 0:1] + sp0
        for h in range(SB_HP):
            sl = slice(h * C_DIM, (h + 1) * C_DIM)
            acc_ref[h] += _dot(w_ref[h], v_ref[pl.ds(start, tk), sl])

    carry_ref[...] = jnp.zeros_like(carry_ref)
    acc_ref[...] = jnp.zeros_like(acc_ref)
    block(i, True)

    def body(jj, c):
        block(i - 1 - jj, False)
        return c

    lax.fori_loop(0, i, body, 0)
    for h in range(SB_HP):
        o_ref[:, h * C_DIM:(h + 1) * C_DIM] = acc_ref[h].astype(o_ref.dtype)


def _sb_prompt(q, k, v):
    assert SB_TQ == SB_TK and C_HEADS % SB_HP == 0
    t_all = BATCH * SEQ
    n_q = SEQ // SB_TQ
    q_spec = pl.BlockSpec((SB_TQ, SB_HP * C_DIM), lambda b, h, i: (b * n_q + i, h))
    kv_spec = pl.BlockSpec((SEQ, SB_HP * C_DIM), lambda b, h, i: (b, h))
    return pl.pallas_call(
        _sb_prompt_kernel,
        grid=(BATCH, C_HEADS // SB_HP, n_q),
        in_specs=[q_spec, kv_spec, kv_spec],
        out_specs=q_spec,
        out_shape=jax.ShapeDtypeStruct((t_all, C_WIDTH), BF16),
        scratch_shapes=[pltpu.VMEM((SB_HP, SB_TQ, SB_TK), F32), pltpu.VMEM((SB_HP, SB_TQ, SB_TK), F32),
                        pltpu.VMEM((SB_HP, SB_TQ, 2 * SB_TK), BF16), pltpu.VMEM((SB_HP, SB_TQ, SB_TK), F32),
                        pltpu.VMEM((SB_HP, SB_TQ, SB_TK), BF16), pltpu.VMEM((SB_HP, SB_TQ, 1), F32),
                        pltpu.VMEM((SB_HP, SB_TQ, C_DIM), F32)],
        compiler_params=_params("arbitrary", "arbitrary", "arbitrary"),
        name="sb_prompt",
    )(q, k, v)


def _sb_sample_kernel(q_ref, kn_ref, vn_ref, kc_ref, vc_ref, o_ref, z_ref, w_ref, carry_ref, acc_ref):
    jj = pl.program_id(1)
    n = DEC_SEQ
    tk = kc_ref.shape[0] // C_HEADS
    scale = np.float32(C_DIM ** -0.5)
    col = lambda h: slice(h * C_DIM, (h + 1) * C_DIM)

    def all_heads(k_of, v_of, width, mask):
        for h in range(C_HEADS):
            z_ref[h * n:(h + 1) * n, 0:width] = _dot_nt(q_ref[:, col(h)], k_of(h)) * scale
        z = z_ref[:, 0:width]
        if mask is not None:
            z = jnp.where(mask, z, NEG)
        w, carry = _sb_weights(z, carry_ref[...], _upper(width))
        carry_ref[...] = carry
        w_ref[:, 0:width] = w.astype(BF16)
        for h in range(C_HEADS):
            acc_ref[:, col(h)] += _dot(w_ref[h * n:(h + 1) * n, 0:width], v_of(h))

    @pl.when(jj == 0)
    def _():
        carry_ref[...] = jnp.zeros_like(carry_ref)
        acc_ref[...] = jnp.zeros_like(acc_ref)
        r_i = lax.broadcasted_iota(jnp.int32, (C_HEADS * n, n), 0) % n
        c_i = lax.broadcasted_iota(jnp.int32, (C_HEADS * n, n), 1)
        all_heads(lambda h: kn_ref[:, col(h)], lambda h: vn_ref[:, col(h)], n, c_i < r_i)

    all_heads(lambda h: kc_ref[pl.ds(h, tk, stride=C_HEADS), :].astype(BF16),
              lambda h: vc_ref[pl.ds(h, tk, stride=C_HEADS), :].astype(BF16), tk, None)

    @pl.when(jj == pl.num_programs(1) - 1)
    def _():
        o_ref[...] = acc_ref[...].astype(o_ref.dtype)


def _sb_sample(q, k, v, cache_k, cache_v, layer):
    n = DEC_SEQ
    off = BATCH * SEQ // n
    past = cache_k.shape[2] // C_HEADS
    tk = _tile(past, SB_SAMPLE_TK)
    n_kb = past // tk
    row_spec = pl.BlockSpec((n, C_WIDTH), lambda b, j: (b + off, 0))
    cache_spec = pl.BlockSpec((None, None, tk * C_HEADS, C_DIM), lambda b, j: (layer, b, n_kb - 1 - j, 0))
    return pl.pallas_call(
        _sb_sample_kernel,
        grid=(DEC_BATCH, n_kb),
        in_specs=[row_spec, row_spec, row_spec, cache_spec, cache_spec],
        out_specs=pl.BlockSpec((n, C_WIDTH), lambda b, j: (b, 0)),
        out_shape=jax.ShapeDtypeStruct((DEC_BATCH * n, C_WIDTH), BF16),
        scratch_shapes=[pltpu.VMEM((C_HEADS * n, tk), F32), pltpu.VMEM((C_HEADS * n, tk), BF16),
                        pltpu.VMEM((C_HEADS * n, 1), F32), pltpu.VMEM((n, C_WIDTH), F32)],
        compiler_params=_params("arbitrary", "arbitrary"),
        name="sb_sample",
    )(q, k, v, cache_k, cache_v)


def _pack_bf16_pairs(x):
    half = x.shape[1] // 2
    bits = lax.bitcast_convert_type(x.astype(BF16).astype(F32), jnp.uint32)
    return (bits[:, :half] & jnp.uint32(0xFFFF0000)) | (bits[:, half:] >> 16)


def _unpack_bf16_pairs(w):
    hi = lax.bitcast_convert_type(w & jnp.uint32(0xFFFF0000), F32)
    lo = lax.bitcast_convert_type(w << 16, F32)
    return jnp.concatenate([hi, lo], axis=1).astype(BF16)


def _outproj_kernel(*refs, n_lhs, n_prompt_tiles):
    lhs = refs[:2 * n_lhs]
    w_ref, x_ref, g_ref, b_ref, wr_ref, x1_ref, x1p_ref, lg_ref = refs[2 * n_lhs:]
    is_prompt = pl.program_id(0) < n_prompt_tiles
    tm = x_ref.shape[0]
    half = tm // 2
    for rows in (slice(0, half), slice(half, tm)):
        k0 = 0
        y = None
        for a_p, a_s in zip(lhs[0::2], lhs[1::2]):
            kk = a_p.shape[1]
            part = _dot(jnp.where(is_prompt, a_p[rows, :], a_s[rows, :]), w_ref[k0:k0 + kk, :])
            y = part if y is None else y + part
            k0 += kk
        x1 = _layer_norm(np.float32(ALPHA) * x_ref[rows, :] + y, g_ref[...], b_ref[...])
        x1_ref[rows, :] = x1
        x1p_ref[rows, :] = _pack_bf16_pairs(x1)
        lg_ref[:, rows] = lax.dot_general(wr_ref[...], x1, (((1,), (1,)), ((), ())), preferred_element_type=F32,
                                          precision=lax.Precision.HIGHEST)


def _outproj_ln(lhs, w, x, g, b, w_router_t, name):
    t = x.shape[0]
    tm = MOE_TM
    n_pt = lhs[0][0].shape[0] // tm
    assert all(a_p.shape[0] == n_pt * tm and a_s.shape[0] == t - n_pt * tm for a_p, a_s in lhs)
    row = lambda a: pl.BlockSpec((tm, a.shape[1]), lambda i: (i, 0))
    full = lambda a: pl.BlockSpec(a.shape, lambda i: (0,) * a.ndim)
    lhs_specs, lhs_args = [], []
    for a_p, a_s in lhs:
        lhs_specs += [pl.BlockSpec((tm, a_p.shape[1]), lambda i: (jnp.minimum(i, n_pt - 1), 0)),
                      pl.BlockSpec((tm, a_s.shape[1]), lambda i: (jnp.maximum(i - n_pt, 0), 0))]
        lhs_args += [a_p, a_s]
    return pl.pallas_call(
        functools.partial(_outproj_kernel, n_lhs=len(lhs), n_prompt_tiles=n_pt),
        grid=(t // tm,),
        in_specs=lhs_specs + [full(w), row(x), full(g), full(b), full(w_router_t)],
        out_specs=[pl.BlockSpec((tm, D_MODEL), lambda i: (i, 0)), pl.BlockSpec((tm, D_MODEL // 2), lambda i: (i, 0)),
                   pl.BlockSpec((N_EXPERTS, tm), lambda i: (0, i))],
        out_shape=[jax.ShapeDtypeStruct((t, D_MODEL), F32), jax.ShapeDtypeStruct((t, D_MODEL // 2), jnp.uint32),
                   jax.ShapeDtypeStruct((N_EXPERTS, t), F32)],
        compiler_params=_params("arbitrary"),
        name=name,
    )(*lhs_args, w, x, g, b, w_router_t)


MOE_TM = 256
EXPERT_BM = 256


def _route_select(lg, bias):
    e, tm = lg.shape
    ninf = np.float32(-np.inf)
    shift = int(np.log2(E_PER_GROUP))
    scores = jax.nn.sigmoid(lg)
    sel = scores + bias
    e_id = lax.broadcasted_iota(jnp.int32, (e, tm), 0)
    g_id = lax.shift_right_logical(e_id, shift)
    g3 = sel.reshape(N_GROUPS, E_PER_GROUP, tm)
    i3 = lax.broadcasted_iota(jnp.int32, g3.shape, 1)
    m1 = g3.max(axis=1, keepdims=True)
    first = jnp.where(g3 == m1, i3, E_PER_GROUP).min(axis=1, keepdims=True)
    m2 = jnp.where(i3 == first, ninf, g3).max(axis=1, keepdims=True)
    grp = jnp.broadcast_to(m1 + m2, g3.shape).reshape(e, tm)
    gsel = jnp.zeros((e, tm), jnp.int32)
    for _ in range(TOPK_GROUPS):
        m = grp.max(axis=0, keepdims=True)
        first = jnp.where(grp == m, e_id, e).min(axis=0, keepdims=True)
        chosen = g_id == lax.shift_right_logical(first, shift)
        gsel = jnp.where(chosen, 1, gsel)
        grp = jnp.where(chosen, ninf, grp)
    sel = jnp.where(gsel > 0, sel, NEG)
    picks = []
    for _ in range(TOP_K):
        m = sel.max(axis=0, keepdims=True)
        first = jnp.where(sel == m, e_id, e).min(axis=0, keepdims=True)
        oh = e_id == first
        picks.append(oh)
        sel = jnp.where(oh, ninf, sel)
    return scores, picks


def _route_kernel(lg_ref, bias_ref, dest_ref, gate_ref, cnt_ref, pad_ref, counts_ref, pads_ref, run_ref):
    phase = pl.program_id(0)
    i = pl.program_id(1)
    e, tm = lg_ref.shape
    scores, picks = _route_select(lg_ref[...], bias_ref[...])
    mask = jnp.zeros((e, tm), F32)
    for oh in picks:
        mask = jnp.where(oh, 1.0, mask)

    @pl.when((phase == 0) & (i == 0))
    def _():
        counts_ref[...] = jnp.zeros_like(counts_ref)

    @pl.when(phase == 0)
    def _():
        counts_ref[...] += mask.sum(axis=1, keepdims=True)

    @pl.when((phase == 1) & (i == 0))
    def _():
        counts = counts_ref[...]
        padded = jnp.ceil(counts * np.float32(1.0 / EXPERT_BM)) * np.float32(EXPERT_BM)
        r_i = lax.broadcasted_iota(jnp.int32, (e, e), 0)
        c_i = lax.broadcasted_iota(jnp.int32, (e, e), 1)
        before = (c_i < r_i).astype(F32)
        starts = jnp.dot(before, jnp.broadcast_to(padded, (e, LANES)), preferred_element_type=F32,
                         precision=lax.Precision.HIGHEST)
        pads_ref[...] = starts[:, 0:1]
        run_ref[...] = jnp.zeros_like(run_ref)
        cnt_ref[...] = jnp.broadcast_to(counts, cnt_ref.shape)
        pad_ref[...] = starts

    @pl.when(phase == 1)
    def _():
        r_i = lax.broadcasted_iota(jnp.int32, (tm, tm), 0)
        c_i = lax.broadcasted_iota(jnp.int32, (tm, tm), 1)
        earlier = (r_i < c_i).astype(BF16)
        slot = pads_ref[...] + run_ref[...] + _dot(mask.astype(BF16), earlier)
        run_ref[...] += mask.sum(axis=1, keepdims=True)
        k_i = lax.broadcasted_iota(jnp.int32, (TOP_K, tm), 0)
        dest = jnp.zeros((TOP_K, tm), F32)
        gate = jnp.zeros((TOP_K, tm), F32)
        total = jnp.zeros((1, tm), F32)
        for k, oh in enumerate(picks):
            d_k = jnp.where(oh, slot, 0.0).sum(axis=0, keepdims=True)
            g_k = jnp.where(oh, scores, 0.0).sum(axis=0, keepdims=True)
            total = total + g_k
            dest = jnp.where(k_i == k, d_k, dest)
            gate = jnp.where(k_i == k, g_k, gate)
        dest_ref[...] = dest.astype(jnp.int32)
        gate_ref[...] = gate / total * np.float32(ROUTED_SCALE)


def _route(lg_t, bias):
    e, t = lg_t.shape
    tm = MOE_TM
    n_t = t // tm
    tile = pl.BlockSpec((TOP_K, tm), lambda p, i: (0, i * p))
    meta = pl.BlockSpec((e, LANES), lambda p, i: (0, 0))
    return pl.pallas_call(
        _route_kernel,
        grid=(2, n_t),
        in_specs=[pl.BlockSpec((e, tm), lambda p, i: (0, i)), pl.BlockSpec((e, 1), lambda p, i: (0, 0))],
        out_specs=[tile, tile, meta, meta],
        out_shape=[jax.ShapeDtypeStruct((TOP_K, t), jnp.int32), jax.ShapeDtypeStruct((TOP_K, t), F32),
                   jax.ShapeDtypeStruct((e, LANES), F32), jax.ShapeDtypeStruct((e, LANES), F32)],
        scratch_shapes=[pltpu.VMEM((e, 1), F32), pltpu.VMEM((e, 1), F32), pltpu.VMEM((e, 1), F32)],
        compiler_params=_params("arbitrary", "arbitrary"),
        name="route",
    )(lg_t, bias)


def _row_copy(src, src_row, dst, dst_row, sem):
    return pltpu.make_async_copy(src.at[pl.ds(src_row, 1)], dst.at[pl.ds(dst_row, 1)], sem)


def _scatter_kernel(last_ref, nu_ref, dest_ref, x_ref, xs_out, zeros_ref, sem, zsem):
    i = pl.program_id(0)
    tm = x_ref.shape[0]
    n_blocks = xs_out.shape[0] // EXPERT_BM

    @pl.when(i == 0)
    def _():
        zeros_ref[...] = jnp.zeros_like(zeros_ref)

        def zero_copy(row):
            return pltpu.make_async_copy(zeros_ref, xs_out.at[pl.ds(pl.multiple_of(row, EXPERT_BM), EXPERT_BM)], zsem)

        def tail_row(b):
            return (nu_ref[0] + b) * EXPERT_BM

        def each(fn, n, act):
            def body(b, c):
                row = fn(b)

                @pl.when(row >= 0)
                def _():
                    act(zero_copy(row))
                return c
            lax.fori_loop(0, n, body, 0)

        start = lambda fn, n: each(fn, n, lambda c: c.start())
        wait = lambda fn, n: each(fn, n, lambda c: c.wait())

        start(lambda e: last_ref[e], N_EXPERTS)
        wait(lambda e: last_ref[e], N_EXPERTS)
        start(tail_row, n_blocks - nu_ref[0])
        wait(tail_row, n_blocks - nu_ref[0])

    def issue(t, c):
        for k in range(TOP_K):
            _row_copy(x_ref, t, xs_out, dest_ref[k * tm + t], sem).start(priority=k % 2)
        return c

    def drain(t, c):
        for k in range(TOP_K):
            _row_copy(x_ref, t, xs_out, dest_ref[k * tm + t], sem).wait()
        return c

    lax.fori_loop(0, tm, issue, 0)
    lax.fori_loop(0, tm, drain, 0)


def _scatter(last_row, n_used, dest_flat, x1p, n_slots):
    t, w = x1p.shape
    tm = MOE_TM
    grid_spec = pltpu.PrefetchScalarGridSpec(
        num_scalar_prefetch=2,
        grid=(t // tm,),
        in_specs=[pl.BlockSpec((TOP_K * tm,), lambda i, lr, nu: (i,), memory_space=pltpu.SMEM),
                  pl.BlockSpec((tm, w), lambda i, lr, nu: (i, 0))],
        out_specs=pl.BlockSpec(memory_space=pl.ANY),
        scratch_shapes=[pltpu.VMEM((EXPERT_BM, w), x1p.dtype), pltpu.SemaphoreType.DMA(()),
                        pltpu.SemaphoreType.DMA(())],
    )
    return pl.pallas_call(
        _scatter_kernel,
        grid_spec=grid_spec,
        out_shape=jax.ShapeDtypeStruct((n_slots, w), x1p.dtype),
        compiler_params=_params("arbitrary"),
        name="scatter",
    )(last_row, n_used, dest_flat, x1p)


def _expert_kernel(be_ref, nu_ref, slot_ref, nxt_ref, x_ref, wg_hbm, wu_hbm, wd_hbm, y_ref,
                   wgf, wuf, wdf, wgb, wub, wdb, sem, *, layer):
    i = pl.program_id(0)
    active = i < nu_ref[0]
    first = (i == 0) | (be_ref[i] != be_ref[jnp.maximum(i - 1, 0)])

    def weight_copies(e, s):
        return [pltpu.make_async_copy(src.at[layer, e], dst.at[s], sem.at[s])
                for src, dst in ((wg_hbm, wgf), (wu_hbm, wuf), (wd_hbm, wdf))]

    @pl.when(active & (i == 0))
    def _():
        for c in weight_copies(be_ref[0], slot_ref[0]):
            c.start()

    @pl.when(active & first)
    def _():
        s = slot_ref[i]
        for c in weight_copies(be_ref[i], s):
            c.wait()
        wgb[...] = wgf[s].astype(BF16)
        wub[...] = wuf[s].astype(BF16)
        wdb[...] = wdf[s].astype(BF16)

        @pl.when(nxt_ref[i] >= 0)
        def _():
            for c in weight_copies(nxt_ref[i], 1 - s):
                c.start()

    @pl.when(active)
    def _():
        x = _unpack_bf16_pairs(x_ref[...])
        g = _dot(x, wgb[...])
        h = (g * jax.nn.sigmoid(g)) * _dot(x, wub[...])
        y_ref[...] = _pack_bf16_pairs(_dot(h.astype(BF16), wdb[...]))

    @pl.when(jnp.logical_not(active))
    def _():
        y_ref[...] = jnp.zeros_like(y_ref)


def _experts(block_e, n_used, slot, nxt, xs, layer, wg, wu, wd):
    n_slots = xs.shape[0]
    n_blocks = n_slots // EXPERT_BM
    ff = wg.shape[3]
    any_spec = pl.BlockSpec(memory_space=pl.ANY)
    grid_spec = pltpu.PrefetchScalarGridSpec(
        num_scalar_prefetch=4,
        grid=(n_blocks,),
        in_specs=[pl.BlockSpec((EXPERT_BM, D_MODEL // 2), lambda i, be, nu, sl, nx: (jnp.minimum(i, nu[0] - 1), 0)),
                  any_spec, any_spec, any_spec],
        out_specs=pl.BlockSpec((EXPERT_BM, D_MODEL // 2), lambda i, be, nu, sl, nx: (i, 0)),
        scratch_shapes=[pltpu.VMEM((2, D_MODEL, ff), F32), pltpu.VMEM((2, D_MODEL, ff), F32),
                        pltpu.VMEM((2, ff, D_MODEL), F32),
                        pltpu.VMEM((D_MODEL, ff), BF16), pltpu.VMEM((D_MODEL, ff), BF16),
                        pltpu.VMEM((ff, D_MODEL), BF16), pltpu.SemaphoreType.DMA((2,))],
    )
    return pl.pallas_call(
        functools.partial(_expert_kernel, layer=layer),
        grid_spec=grid_spec,
        out_shape=jax.ShapeDtypeStruct((n_slots, D_MODEL // 2), jnp.uint32),
        compiler_params=_params("arbitrary"),
        name="experts",
    )(block_e, n_used, slot, nxt, xs, wg, wu, wd)


def _ffn_ln_kernel(dest_ref, next_ref, xp_ref, x_ref, gate_ref, y_hbm, wg_ref, wu_ref, wd_ref, g_ref, b_ref, o_ref,
                   buf, sem):
    i = pl.program_id(0)
    n = pl.num_programs(0)
    tm = x_ref.shape[0]

    def row_copy(table, t, k, slot):
        return _row_copy(y_hbm, table[k * tm + t], buf.at[slot, k], t, sem.at[slot])

    def issue(table, slot):
        def body(t, c):
            for k in range(TOP_K):
                row_copy(table, t, k, slot).start(priority=k % 2)
            return c
        lax.fori_loop(0, tm, body, 0)

    def drain(table, slot):
        def body(t, c):
            for k in range(TOP_K):
                row_copy(table, t, k, slot).wait()
            return c
        lax.fori_loop(0, tm, body, 0)

    cur = lax.rem(i, 2)

    @pl.when(i == 0)
    def _():
        issue(dest_ref, 0)

    @pl.when(i + 1 < n)
    def _():
        issue(next_ref, 1 - cur)

    xb = _unpack_bf16_pairs(xp_ref[...])
    gte = _dot(xb, wg_ref[...])
    h = (gte * jax.nn.sigmoid(gte)) * _dot(xb, wu_ref[...])
    shared = _dot(h.astype(BF16), wd_ref[...])
    drain(dest_ref, cur)
    lo_half = hi_half = None
    for k in range(TOP_K):
        w = buf[cur, k]
        g = gate_ref[:, k:k + 1]
        hi = lax.bitcast_convert_type(w & jnp.uint32(0xFFFF0000), F32) * g
        lo = lax.bitcast_convert_type(w << 16, F32) * g
        hi_half = hi if hi_half is None else hi_half + hi
        lo_half = lo if lo_half is None else lo_half + lo
    routed = jnp.concatenate([hi_half, lo_half], axis=1)
    z = np.float32(ALPHA) * x_ref[...] + (routed + shared)
    o_ref[...] = _layer_norm(z, g_ref[...], b_ref[...])


def _ffn_ln(dest_flat, x1p, x1, gate_t, y, wg, wu, wd, g, b):
    t = x1.shape[0]
    tm = MOE_TM
    n_t = t // tm
    row = lambda a: pl.BlockSpec((tm, a.shape[1]), lambda i: (i, 0))
    full = lambda a: pl.BlockSpec(a.shape, lambda i: (0,) * a.ndim)
    return pl.pallas_call(
        _ffn_ln_kernel,
        grid=(n_t,),
        in_specs=[pl.BlockSpec((TOP_K * tm,), lambda i: (i,), memory_space=pltpu.SMEM),
                  pl.BlockSpec((TOP_K * tm,), lambda i: (jnp.minimum(i + 1, n_t - 1),), memory_space=pltpu.SMEM),
                  row(x1p), row(x1), row(gate_t), pl.BlockSpec(memory_space=pl.ANY),
                  full(wg), full(wu), full(wd), full(g), full(b)],
        out_specs=row(x1),
        out_shape=jax.ShapeDtypeStruct((t, D_MODEL), F32),
        scratch_shapes=[pltpu.VMEM((2, TOP_K, tm, D_MODEL // 2), jnp.uint32), pltpu.SemaphoreType.DMA((2,))],
        compiler_params=_params("arbitrary"),
        name="ffn_ln",
    )(dest_flat, dest_flat, x1p, x1, gate_t, y, wg, wu, wd, g, b)


def _ple_kernel(x_ref, p_ref, wp_ref, wg_ref, *o_refs, n_prompt_tiles):
    x = x_ref[...]
    gate = jax.nn.sigmoid(_dot(x.astype(BF16), wg_ref[...]))
    emb = _dot(p_ref[...].astype(BF16), wp_ref[...])
    out = x + emb * gate
    if len(o_refs) == 1:
        o_refs[0][...] = out
    else:
        i = pl.program_id(0)

        @pl.when(i < n_prompt_tiles)
        def _():
            o_refs[0][...] = out

        @pl.when(i >= n_prompt_tiles)
        def _():
            o_refs[1][...] = out


def _ple(x2, p, w_ple, w_gate, n_prompt=None):
    t = x2.shape[0]
    tm = _tile(t if n_prompt is None else np.gcd(n_prompt, t - n_prompt), 256)
    n_pt = None if n_prompt is None else n_prompt // tm
    row = lambda a: pl.BlockSpec((tm, a.shape[1]), lambda i: (i, 0))
    full = lambda a: pl.BlockSpec(a.shape, lambda i: (0,) * a.ndim)
    if n_prompt is None:
        out_specs = [row(x2)]
        out_shape = [jax.ShapeDtypeStruct((t, D_MODEL), F32)]
    else:
        out_specs = [pl.BlockSpec((tm, D_MODEL), lambda i: (jnp.minimum(i, n_pt - 1), 0)),
                     pl.BlockSpec((tm, D_MODEL), lambda i: (jnp.maximum(i - n_pt, 0), 0))]
        out_shape = [jax.ShapeDtypeStruct((n_prompt, D_MODEL), F32),
                     jax.ShapeDtypeStruct((t - n_prompt, D_MODEL), F32)]
    return pl.pallas_call(
        functools.partial(_ple_kernel, n_prompt_tiles=n_pt),
        grid=(t // tm,),
        in_specs=[row(x2), row(p), full(w_ple), full(w_gate)],
        out_specs=out_specs, out_shape=out_shape,
        compiler_params=_params("arbitrary"),
        name="ple",
    )(x2, p, w_ple, w_gate)


def _rel_bias(table, qpos, kpos):
    idx = np.clip(qpos[:, None] - kpos[None, :], -REL_CLIP, REL_CLIP) + REL_CLIP
    return table[:, idx]


def _band_prompt_bias(table):
    tq, width = BAND_TQ, BAND_NWB * BAND_TQ
    span = width + tq - 1
    rel = (BAND_NWB - 1) * tq + tq - 1 - np.arange(span)
    strip = table[:, np.clip(rel, -REL_CLIP, REL_CLIP) + REL_CLIP]
    u = jnp.concatenate([strip[:, tq - 1:], strip[:, :1], strip[:, :tq - 1]], axis=1)
    flat = jnp.tile(u, (1, tq))[:, :tq * span]
    return flat.reshape(table.shape[0], tq, span)[:, :, :width]


def _band_mask(qpos, kpos):
    dc = qpos // CHUNK - kpos // CHUNK
    return (kpos >= 0) & (dc >= 0) & (dc <= B_PREV_CHUNKS)


def _post_block(x, lhs, w_out, p, ln1_g, ln1_b, ln2_g, ln2_b, w_router, b_router,
                layer, w_gate, w_up, w_down, ws_gate, ws_up, ws_down, w_ple, w_ple_gate, name, n_prompt):
    row = lambda a: a.reshape(1, -1)
    t = x.shape[0]
    x1, x1p, lg_t = _outproj_ln(lhs, w_out, x, row(ln1_g), row(ln1_b), w_router.T, name)
    dest, gate, cnt, pad = _route(lg_t, b_router.astype(F32).reshape(N_EXPERTS, 1))

    n_blocks = -(-(t * TOP_K + N_EXPERTS * (EXPERT_BM - 1)) // EXPERT_BM)
    counts = cnt[:, 0].astype(jnp.int32)
    pad_end = pad[:, 0].astype(jnp.int32) + (counts + EXPERT_BM - 1) // EXPERT_BM * EXPERT_BM
    block_e = jnp.minimum((pad_end[:, None] <= jnp.arange(n_blocks)[None, :] * EXPERT_BM).sum(axis=0),
                          N_EXPERTS - 1).astype(jnp.int32)
    n_used = (pad_end[-1:] // EXPERT_BM).astype(jnp.int32)
    nonempty = counts > 0
    e_ids = jnp.arange(N_EXPERTS)
    later_ids = jnp.where(nonempty[None, :] & (e_ids[None, :] > e_ids[:, None]), e_ids[None, :], N_EXPERTS).min(axis=1)
    nxt_of = jnp.where(later_ids < N_EXPERTS, later_ids, -1)
    slot_of = (jnp.cumsum(nonempty) - 1) % 2
    in_block = (block_e[:, None] == e_ids[None, :]).astype(jnp.int32)
    nxt = (in_block * nxt_of[None, :]).sum(axis=1).astype(jnp.int32)
    slot = (in_block * slot_of[None, :]).sum(axis=1).astype(jnp.int32)
    dest_flat = dest.reshape(TOP_K, t // MOE_TM, MOE_TM).transpose(1, 0, 2).reshape(-1)

    last_row = jnp.where(counts > 0, pad_end - EXPERT_BM, -1).astype(jnp.int32)
    xs = _scatter(last_row, n_used, dest_flat, x1p, n_blocks * EXPERT_BM)
    y = _experts(block_e, n_used, slot, nxt, xs, layer, w_gate, w_up, w_down)
    x2 = _ffn_ln(dest_flat, x1p, x1, gate.T, y, ws_gate, ws_up, ws_down, row(ln2_g), row(ln2_b))
    return _ple(x2, p, w_ple, w_ple_gate, n_prompt)


def kernel(x_prompt, x_sample, cache_b_k, cache_b_v, cache_c_k, cache_c_v, p_prompt, p_sample, w_in_ab, w_out_ab, sgu_w, sgu_b, sgu_ln_g, sgu_ln_b, rel_bias_tab, w_in_c, w_out_c, ln_mix_g, ln_mix_b, ln_ffn_g, ln_ffn_b, w_router, b_router, w_gate, w_up, w_down, ws_gate, ws_up, ws_down, w_ple, w_ple_gate):
    n_p = BATCH * SEQ
    n_s = DEC_BATCH * DEC_SEQ
    x = jnp.concatenate([x_prompt.reshape(n_p, D_MODEL), x_sample.reshape(n_s, D_MODEL)], axis=0)
    p_all = jnp.concatenate([p_prompt.reshape(DEPTH, n_p, PLE_DIM), p_sample.reshape(DEPTH, n_s, PLE_DIM)], axis=1)
    bf = lambda a: a.astype(BF16)
    outs = {k: [] for k in ("bk_p", "bv_p", "bk_s", "bv_s", "av_s", "ck_p", "cv_p", "ck_s", "cv_s")}
    kv_outs = ((F32, True), (BF16, False))

    for i in range(DEPTH):
        j = i // 2
        if i % 2 == 0:
            plain = lambda d: ((d, False),)
            hu, hv, q, k_p, k_s, kb16, v_p, v_s, vb16 = _proj(
                x, bf(w_in_ab[j]), 0,
                [(0, A_WIDTH, plain(F32)), (A_WIDTH, A_WIDTH, plain(F32)), (2 * A_WIDTH, B_WIDTH, plain(BF16)),
                 (2 * A_WIDTH + B_WIDTH, B_WIDTH, kv_outs), (2 * A_WIDTH + 2 * B_WIDTH, B_WIDTH, kv_outs)],
                "proj_ab", n_p)

            bs_t = sgu_b[j].T
            a_p, = _sgu(hu, hv, sgu_w[j], bs_t, sgu_ln_g[j], sgu_ln_b[j], A_CHUNK, 0, n_p // A_CHUNK, False)
            a_s, va_s = _sgu(hu, hv, sgu_w[j][:, :DEC_SEQ, :DEC_SEQ], bs_t[:DEC_SEQ], sgu_ln_g[j], sgu_ln_b[j],
                             DEC_SEQ, n_p, DEC_BATCH, True)

            b_p = _band_prompt(q, kb16, vb16, _band_prompt_bias(rel_bias_tab[j]))
            n_cache = cache_b_k.shape[2]
            qs = PAST_LEN + np.arange(DEC_SEQ)
            kc = PAST_LEN - n_cache + np.arange(n_cache)
            b_s = _band_sample(
                q, kb16, vb16, cache_b_k.reshape(N_EVEN, DEC_BATCH, n_cache * B_HEADS, B_DIM),
                cache_b_v.reshape(N_EVEN, DEC_BATCH, n_cache * B_HEADS, B_DIM), j,
                _rel_bias(rel_bias_tab[j], qs, kc), _rel_bias(rel_bias_tab[j], qs, qs),
                jnp.asarray(_band_mask(qs[:, None], kc[None, :]), F32),
                jnp.asarray(_band_mask(qs[:, None], qs[None, :]), F32))

            keep = min(B_WINDOW, SEQ)
            outs["bk_p"].append(k_p.reshape(BATCH, SEQ, B_HEADS, B_DIM)[:, SEQ - keep:])
            outs["bv_p"].append(v_p.reshape(BATCH, SEQ, B_HEADS, B_DIM)[:, SEQ - keep:])
            outs["bk_s"].append(k_s.reshape(DEC_BATCH, DEC_SEQ, B_HEADS, B_DIM))
            outs["bv_s"].append(v_s.reshape(DEC_BATCH, DEC_SEQ, B_HEADS, B_DIM))
            outs["av_s"].append(va_s.reshape(DEC_BATCH, DEC_SEQ, A_HEADS, A_DIM))
            lhs, w_out, name = [(a_p, a_s), (b_p, b_s)], bf(w_out_ab[j]), "outproj_ab"
        else:
            w_in = bf(w_in_c[j])
            q, k_p, k_s, kb16 = _proj(x, w_in, 0, [(0, C_WIDTH, ((BF16, False),)), (C_WIDTH, C_WIDTH, kv_outs)],
                                      "proj_cqk", n_p)
            v_p, v_s, vb16 = _proj(x, w_in, 2, [(0, C_WIDTH, kv_outs)], "proj_cv", n_p)
            o_p = _sb_prompt(q, kb16, vb16)
            rows = lambda c: c.reshape(N_ODD, DEC_BATCH, PAST_LEN * C_HEADS, C_DIM)
            o_s = _sb_sample(q, kb16, vb16, rows(cache_c_k), rows(cache_c_v), j)
            outs["ck_p"].append(k_p.reshape(BATCH, SEQ, C_HEADS, C_DIM))
            outs["cv_p"].append(v_p.reshape(BATCH, SEQ, C_HEADS, C_DIM))
            outs["ck_s"].append(k_s.reshape(DEC_BATCH, DEC_SEQ, C_HEADS, C_DIM))
            outs["cv_s"].append(v_s.reshape(DEC_BATCH, DEC_SEQ, C_HEADS, C_DIM))
            lhs, w_out, name = [(o_p, o_s)], bf(w_out_c[j]), "outproj_c"

        x = _post_block(x, lhs, w_out, p_all[i], ln_mix_g[i], ln_mix_b[i], ln_ffn_g[i], ln_ffn_b[i],
                        w_router[i], b_router[i], i, w_gate, w_up, w_down,
                        bf(ws_gate[i]), bf(ws_up[i]), bf(ws_down[i]), bf(w_ple[i]), bf(w_ple_gate[i]), name,
                        n_p if i == DEPTH - 1 else None)
        if i < DEPTH - 1:
            x, = x

    y_p, y_s = x
    st = lambda key: jnp.stack(outs[key])
    return (y_p.reshape(BATCH, SEQ, D_MODEL), y_s.reshape(DEC_BATCH, DEC_SEQ, D_MODEL),
            st("bk_p"), st("bv_p"), st("bk_s"), st("bv_s"), st("av_s"),
            st("ck_p"), st("cv_p"), st("ck_s"), st("cv_s"))
```

```python
import functools

import numpy as np
import jax
import jax.numpy as jnp
from jax import lax
from jax.experimental import pallas as pl
from jax.experimental.pallas import tpu as pltpu

D_MODEL = 2048
BATCH = 2
SEQ = 4096
DEPTH = 2
DEC_BATCH = 32
DEC_SEQ = 16
PAST_LEN = 2048

CHUNK = 64
N_EVEN = (DEPTH + 1) // 2
N_ODD = DEPTH // 2
A_CHUNK = 128
A_HEADS = 8
A_DIM = 128
A_WIDTH = A_HEADS * A_DIM
B_HEADS = 8
B_DIM = 128
B_WIDTH = B_HEADS * B_DIM
B_PREV_CHUNKS = 8
B_WINDOW = B_PREV_CHUNKS * CHUNK
REL_CLIP = 128
C_HEADS = 16
C_DIM = 128
C_WIDTH = C_HEADS * C_DIM
N_EXPERTS = 64
N_GROUPS = 8
E_PER_GROUP = N_EXPERTS // N_GROUPS
TOPK_GROUPS = 4
TOP_K = 8
EXPERT_FF = 512
SHARED_FF = 512
ROUTED_SCALE = 2.5
PLE_DIM = 256
LN_EPS = 1e-5
ALPHA = (2 * DEPTH) ** 0.25
NEG = -1e9

LANES = 128
VMEM_LIMIT_BYTES = 56 * 1024 * 1024

BF16 = jnp.bfloat16
F32 = jnp.float32


def _params(*sem):
    return pltpu.CompilerParams(dimension_semantics=sem, vmem_limit_bytes=VMEM_LIMIT_BYTES)


def _tile(n, pref):
    if n <= pref:
        return n
    for t in range(pref, 7, -1):
        if n % t == 0 and t % 8 == 0:
            return t
    return n


def _dot(a, b):
    return jnp.dot(a, b, preferred_element_type=F32)


def _dot_nt(a, b):
    return lax.dot_general(a, b, (((1,), (1,)), ((), ())), preferred_element_type=F32)


def _layer_norm(z, g, b):
    mu = jnp.mean(z, axis=-1, keepdims=True)
    zc = z - mu
    var = jnp.mean(zc * zc, axis=-1, keepdims=True)
    return zc * lax.rsqrt(var + LN_EPS) * g + b


def _gelu(x):
    return x * (lax.erf(x * np.float32(1.0 / np.sqrt(2.0))) + 1.0) * 0.5


def _softplus(z):
    return jnp.maximum(z, 0.0) + jnp.log(1.0 + jnp.exp(-jnp.abs(z)))


def _row_specs(a, tm, n_prompt_tiles):
    if not isinstance(a, tuple):
        return [pl.BlockSpec((tm, a.shape[1]), lambda i: (i, 0))], [a]
    a_p, a_s = a
    assert a_p.shape[0] == n_prompt_tiles * tm and a_s.shape[0] % tm == 0
    return ([pl.BlockSpec((tm, a_p.shape[1]), lambda i: (jnp.minimum(i, n_prompt_tiles - 1), 0)),
             pl.BlockSpec((tm, a_s.shape[1]), lambda i: (jnp.maximum(i - n_prompt_tiles, 0), 0))], [a_p, a_s])


def _row_value(refs, is_prompt, rows=slice(None)):
    if len(refs) == 1:
        return refs[0][rows, :]
    return jnp.where(is_prompt, refs[0][rows, :], refs[1][rows, :])


def _proj_kernel(*refs, segs, n_prompt_tiles, n_x):
    x_refs, w_ref, o_refs = refs[:n_x], refs[n_x], refs[n_x + 1:]
    i = pl.program_id(0)
    xb = _row_value(x_refs, i < n_prompt_tiles).astype(BF16)
    refs = iter(o_refs)
    for col0, ncols, outs in segs:
        acc = _dot(xb, w_ref[:, col0:col0 + ncols])
        for _, is_split in outs:
            if is_split:
                o_p, o_s = next(refs), next(refs)

                @pl.when(i < n_prompt_tiles)
                def _(o_p=o_p, acc=acc):
                    o_p[...] = acc.astype(o_p.dtype)

                @pl.when(i >= n_prompt_tiles)
                def _(o_s=o_s, acc=acc):
                    o_s[...] = acc.astype(o_s.dtype)
            else:
                o = next(refs)
                o[...] = acc.astype(o.dtype)


def _proj(x, w, col_block, segs, name, n_prompt):
    m = sum(a.shape[0] for a in x) if isinstance(x, tuple) else x.shape[0]
    k = (x[0] if isinstance(x, tuple) else x).shape[1]
    width = sum(n for _, n, _ in segs)
    tm = _tile(np.gcd(n_prompt, m - n_prompt), 256)
    n_pt = n_prompt // tm
    x_specs, x_args = _row_specs(x, tm, n_pt)
    out_specs, out_shape = [], []
    for _, ncols, outs in segs:
        for dtype, is_split in outs:
            if is_split:
                out_specs += [pl.BlockSpec((tm, ncols), lambda i: (jnp.minimum(i, n_pt - 1), 0)),
                              pl.BlockSpec((tm, ncols), lambda i: (jnp.maximum(i - n_pt, 0), 0))]
                out_shape += [jax.ShapeDtypeStruct((n_prompt, ncols), dtype),
                              jax.ShapeDtypeStruct((m - n_prompt, ncols), dtype)]
            else:
                out_specs.append(pl.BlockSpec((tm, ncols), lambda i: (i, 0)))
                out_shape.append(jax.ShapeDtypeStruct((m, ncols), dtype))
    return pl.pallas_call(
        functools.partial(_proj_kernel, segs=tuple(segs), n_prompt_tiles=n_pt, n_x=len(x_args)),
        grid=(m // tm,),
        in_specs=x_specs + [pl.BlockSpec((k, width), lambda i: (0, col_block), pipeline_mode=pl.Buffered(1))],
        out_specs=out_specs, out_shape=out_shape,
        compiler_params=_params("arbitrary"),
        name=name,
    )(*x_args, w)


def _sgu_kernel(hu_ref, hv_ref, w_ref, bs_ref, g_ref, b_ref, a_ref, *v_out, emit_v):
    rows = hu_ref.shape[0]
    r_i = lax.broadcasted_iota(jnp.int32, (rows, rows), 0)
    c_i = lax.broadcasted_iota(jnp.int32, (rows, rows), 1)
    causal = c_i <= r_i
    for h in range(A_HEADS):
        sl = slice(h * A_DIM, (h + 1) * A_DIM)
        u = _gelu(hu_ref[:, sl])
        v = _layer_norm(_gelu(hv_ref[:, sl]), g_ref[h:h + 1, :], b_ref[h:h + 1, :])
        w = jnp.where(causal, w_ref[h], 0.0).astype(BF16)
        mix = _dot(w, v.astype(BF16)) + bs_ref[:, h:h + 1]
        a_ref[:, sl] = (u * mix).astype(a_ref.dtype)
        if emit_v:
            v_out[0][:, sl] = v


def _sgu(hu, hv, w_s, bs_t, g, b, rows, row0, n_chunks, emit_v):
    off = row0 // rows
    in_spec = pl.BlockSpec((rows, A_WIDTH), lambda c: (c + off, 0))
    out_spec = pl.BlockSpec((rows, A_WIDTH), lambda c: (c, 0))
    full = lambda a: pl.BlockSpec(a.shape, lambda c: (0,) * a.ndim)
    out_specs = [out_spec]
    out_shape = [jax.ShapeDtypeStruct((n_chunks * rows, A_WIDTH), BF16)]
    if emit_v:
        out_specs.append(out_spec)
        out_shape.append(jax.ShapeDtypeStruct((n_chunks * rows, A_WIDTH), F32))
    return pl.pallas_call(
        functools.partial(_sgu_kernel, emit_v=emit_v),
        grid=(n_chunks,),
        in_specs=[in_spec, in_spec, full(w_s), full(bs_t), full(g), full(b)],
        out_specs=out_specs, out_shape=out_shape,
        compiler_params=_params("arbitrary"),
        name="sgu_sample" if emit_v else "sgu_prompt",
    )(hu, hv, w_s, bs_t, g, b)


BAND_TQ = 2 * CHUNK
BAND_NWB = B_WINDOW // BAND_TQ + 1


BAND_HP = 8


def _band_prompt_kernel(q_ref, k_ref, v_ref, bias_ref, o_ref):
    t = pl.program_id(2)
    tq = BAND_TQ
    shift = int(np.log2(CHUNK))
    q_chunk = lax.shift_right_arithmetic(t * tq + lax.broadcasted_iota(jnp.int32, (tq, 1), 0), shift)
    lane = lax.broadcasted_iota(jnp.int32, (1, tq), 1)
    starts, valids = [], []
    for j in range(BAND_NWB):
        kb = t - (BAND_NWB - 1) + j
        starts.append(pl.multiple_of(jnp.maximum(kb, 0) * tq, tq))
        kpos = kb * tq + lane
        dc = q_chunk - lax.shift_right_arithmetic(kpos, shift)
        valids.append((kpos >= 0) & (dc >= 0) & (dc <= B_PREV_CHUNKS))
    for h in range(BAND_HP):
        sl = slice(h * B_DIM, (h + 1) * B_DIM)
        q = q_ref[:, sl]
        s_blocks = []
        for j in range(BAND_NWB):
            s = (_dot_nt(q, k_ref[pl.ds(starts[j], tq), sl]) * np.float32(B_DIM ** -0.5)
                 + bias_ref[h, :, j * tq:(j + 1) * tq])
            s_blocks.append(jnp.where(valids[j], s, NEG))
        m = s_blocks[0].max(axis=-1, keepdims=True)
        for s in s_blocks[1:]:
            m = jnp.maximum(m, s.max(axis=-1, keepdims=True))
        acc = jnp.zeros((tq, B_DIM), F32)
        den = jnp.zeros((tq, 1), F32)
        for j, s in enumerate(s_blocks):
            p = jnp.exp(s - m)
            den = den + p.sum(axis=-1, keepdims=True)
            acc = acc + _dot(p.astype(BF16), v_ref[pl.ds(starts[j], tq), sl])
        o_ref[:, sl] = (acc / den).astype(o_ref.dtype)


def _band_prompt(q, k, v, bias):
    assert B_HEADS % BAND_HP == 0
    t_all = BATCH * SEQ
    n_t = SEQ // BAND_TQ
    q_spec = pl.BlockSpec((BAND_TQ, BAND_HP * B_DIM), lambda b, h, t: (b * n_t + t, h))
    kv_spec = pl.BlockSpec((SEQ, BAND_HP * B_DIM), lambda b, h, t: (b, h))
    return pl.pallas_call(
        _band_prompt_kernel,
        grid=(BATCH, B_HEADS // BAND_HP, n_t),
        in_specs=[q_spec, kv_spec, kv_spec,
                  pl.BlockSpec((BAND_HP, BAND_TQ, BAND_NWB * BAND_TQ), lambda b, h, t: (h, 0, 0))],
        out_specs=q_spec,
        out_shape=jax.ShapeDtypeStruct((t_all, B_WIDTH), BF16),
        compiler_params=_params("arbitrary", "arbitrary", "arbitrary"),
        name="band_prompt",
    )(q, k, v, bias)


def _band_sample_kernel(q_ref, kn_ref, vn_ref, kc_ref, vc_ref, bc_ref, bn_ref, mc_ref, mn_ref, o_ref):
    scale = np.float32(B_DIM ** -0.5)
    n_cache = kc_ref.shape[0] // B_HEADS
    for h in range(B_HEADS):
        sl = slice(h * B_DIM, (h + 1) * B_DIM)
        q = q_ref[:, sl]
        s_c = _dot_nt(q, kc_ref[pl.ds(h, n_cache, stride=B_HEADS), :].astype(BF16)) * scale + bc_ref[h]
        s_n = _dot_nt(q, kn_ref[:, sl]) * scale + bn_ref[h]
        s_c = jnp.where(mc_ref[...] > 0, s_c, NEG)
        s_n = jnp.where(mn_ref[...] > 0, s_n, NEG)
        m = jnp.maximum(s_c.max(axis=-1, keepdims=True), s_n.max(axis=-1, keepdims=True))
        p_c = jnp.exp(s_c - m)
        p_n = jnp.exp(s_n - m)
        den = p_c.sum(axis=-1, keepdims=True) + p_n.sum(axis=-1, keepdims=True)
        acc = (_dot(p_c.astype(BF16), vc_ref[pl.ds(h, n_cache, stride=B_HEADS), :].astype(BF16))
               + _dot(p_n.astype(BF16), vn_ref[:, sl]))
        o_ref[:, sl] = (acc / den).astype(o_ref.dtype)


def _band_sample(q, k, v, cache_k, cache_v, layer, bias_c, bias_n, mask_c, mask_n):
    n = DEC_SEQ
    off = BATCH * SEQ // n
    n_cache = cache_k.shape[2] // B_HEADS
    row_spec = pl.BlockSpec((n, B_WIDTH), lambda b: (b + off, 0))
    cache_spec = pl.BlockSpec((None, None, n_cache * B_HEADS, B_DIM), lambda b: (layer, b, 0, 0))
    full = lambda a: pl.BlockSpec(a.shape, lambda b: (0,) * a.ndim)
    return pl.pallas_call(
        _band_sample_kernel,
        grid=(DEC_BATCH,),
        in_specs=[row_spec, row_spec, row_spec, cache_spec, cache_spec,
                  full(bias_c), full(bias_n), full(mask_c), full(mask_n)],
        out_specs=pl.BlockSpec((n, B_WIDTH), lambda b: (b, 0)),
        out_shape=jax.ShapeDtypeStruct((DEC_BATCH * n, B_WIDTH), BF16),
        compiler_params=_params("arbitrary"),
        name="band_sample",
    )(q, k, v, cache_k, cache_v, bias_c, bias_n, mask_c, mask_n)


SB_TQ = 256
SB_TK = 256
SB_SAMPLE_TK = 512
SB_HP = 4


def _sb_weights(z, carry, upper):
    sp = _softplus(z)
    hi = sp.astype(BF16)
    lo = (sp - hi.astype(F32)).astype(BF16)
    later = _dot(hi, upper) + _dot(lo, upper)
    w = jnp.exp((z - sp) - (carry + later))
    return w, carry + later[:, 0:1] + sp[:, 0:1]


def _upper(n):
    r_i = lax.broadcasted_iota(jnp.int32, (n, n), 0)
    c_i = lax.broadcasted_iota(jnp.int32, (n, n), 1)
    return (r_i > c_i).astype(BF16)


SB_ROWS = 32


def _sb_prompt_kernel(q_ref, k_ref, v_ref, o_ref, z_ref, ls_ref, hl_ref, lat_ref, w_ref, carry_ref, acc_ref):
    i = pl.program_id(2)
    tq, tk = SB_TQ, SB_TK
    upper2 = jnp.concatenate([_upper(tk)] * 2, axis=0)
    scale = np.float32(C_DIM ** -0.5)
    chunks = [slice(r, r + SB_ROWS) for r in range(0, tq, SB_ROWS)]
    r_i = lax.broadcasted_iota(jnp.int32, (SB_ROWS, tk), 0)
    c_i = lax.broadcasted_iota(jnp.int32, (SB_ROWS, tk), 1)

    def block(j, diagonal):
        start = pl.multiple_of(j * tk, tk)
        for h in range(SB_HP):
            sl = slice(h * C_DIM, (h + 1) * C_DIM)
            z_ref[h] = _dot_nt(q_ref[:, sl], k_ref[pl.ds(start, tk), sl])
        for h in range(SB_HP):
            for ci, rows in enumerate(chunks):
                z = z_ref[h, rows, :] * scale
                if diagonal:
                    z = jnp.where(c_i < r_i + ci * SB_ROWS, z, NEG)
                sp = _softplus(z)
                hi = sp.astype(BF16)
                ls_ref[h, rows, :] = z - sp
                hl_ref[h, rows, 0:tk] = hi
                hl_ref[h, rows, tk:2 * tk] = (sp - hi.astype(F32)).astype(BF16)
        for h in range(SB_HP):
            lat_ref[h] = _dot(hl_ref[h], upper2)
        for h in range(SB_HP):
            for rows in chunks:
                later = lat_ref[h, rows, :]
                carry = carry_ref[h, rows, :]
                w_ref[h, rows, :] = jnp.exp(ls_ref[h, rows, :] - (carry + later)).astype(BF16)
                sp0 = hl_ref[h, rows, 0:1].astype(F32) + hl_ref[h, rows, tk:tk + 1].astype(F32)
                carry_ref[h, rows, :] = carry + later[:, 0:1] + sp0
        for h in range(SB_HP):
            sl = slice(h * C_DIM, (h + 1) * C_DIM)
            acc_ref[h] += _dot(w_ref[h], v_ref[pl.ds(start, tk), sl])

    carry_ref[...] = jnp.zeros_like(carry_ref)
    acc_ref[...] = jnp.zeros_like(acc_ref)
    block(i, True)

    def body(jj, c):
        block(i - 1 - jj, False)
        return c

    lax.fori_loop(0, i, body, 0)
    for h in range(SB_HP):
        o_ref[:, h * C_DIM:(h + 1) * C_DIM] = acc_ref[h].astype(o_ref.dtype)


def _sb_prompt(q, k, v):
    assert SB_TQ == SB_TK and C_HEADS % SB_HP == 0
    t_all = BATCH * SEQ
    n_q = SEQ // SB_TQ
    q_spec = pl.BlockSpec((SB_TQ, SB_HP * C_DIM), lambda b, h, i: (b * n_q + i, h))
    kv_spec = pl.BlockSpec((SEQ, SB_HP * C_DIM), lambda b, h, i: (b, h))
    return pl.pallas_call(
        _sb_prompt_kernel,
        grid=(BATCH, C_HEADS // SB_HP, n_q),
        in_specs=[q_spec, kv_spec, kv_spec],
        out_specs=q_spec,
        out_shape=jax.ShapeDtypeStruct((t_all, C_WIDTH), BF16),
        scratch_shapes=[pltpu.VMEM((SB_HP, SB_TQ, SB_TK), F32), pltpu.VMEM((SB_HP, SB_TQ, SB_TK), F32),
                        pltpu.VMEM((SB_HP, SB_TQ, 2 * SB_TK), BF16), pltpu.VMEM((SB_HP, SB_TQ, SB_TK), F32),
                        pltpu.VMEM((SB_HP, SB_TQ, SB_TK), BF16), pltpu.VMEM((SB_HP, SB_TQ, 1), F32),
                        pltpu.VMEM((SB_HP, SB_TQ, C_DIM), F32)],
        compiler_params=_params("arbitrary", "arbitrary", "arbitrary"),
        name="sb_prompt",
    )(q, k, v)


def _sb_sample_kernel(q_ref, kn_ref, vn_ref, kc_ref, vc_ref, o_ref, z_ref, w_ref, carry_ref, acc_ref):
    jj = pl.program_id(1)
    n = DEC_SEQ
    tk = kc_ref.shape[0] // C_HEADS
    scale = np.float32(C_DIM ** -0.5)
    col = lambda h: slice(h * C_DIM, (h + 1) * C_DIM)

    def all_heads(k_of, v_of, width, mask):
        for h in range(C_HEADS):
            z_ref[h * n:(h + 1) * n, 0:width] = _dot_nt(q_ref[:, col(h)], k_of(h)) * scale
        z = z_ref[:, 0:width]
        if mask is not None:
            z = jnp.where(mask, z, NEG)
        w, carry = _sb_weights(z, carry_ref[...], _upper(width))
        carry_ref[...] = carry
        w_ref[:, 0:width] = w.astype(BF16)
        for h in range(C_HEADS):
            acc_ref[:, col(h)] += _dot(w_ref[h * n:(h + 1) * n, 0:width], v_of(h))

    @pl.when(jj == 0)
    def _():
        carry_ref[...] = jnp.zeros_like(carry_ref)
        acc_ref[...] = jnp.zeros_like(acc_ref)
        r_i = lax.broadcasted_iota(jnp.int32, (C_HEADS * n, n), 0) % n
        c_i = lax.broadcasted_iota(jnp.int32, (C_HEADS * n, n), 1)
        all_heads(lambda h: kn_ref[:, col(h)], lambda h: vn_ref[:, col(h)], n, c_i < r_i)

    all_heads(lambda h: kc_ref[pl.ds(h, tk, stride=C_HEADS), :].astype(BF16),
              lambda h: vc_ref[pl.ds(h, tk, stride=C_HEADS), :].astype(BF16), tk, None)

    @pl.when(jj == pl.num_programs(1) - 1)
    def _():
        o_ref[...] = acc_ref[...].astype(o_ref.dtype)


def _sb_sample(q, k, v, cache_k, cache_v, layer):
    n = DEC_SEQ
    off = BATCH * SEQ // n
    past = cache_k.shape[2] // C_HEADS
    tk = _tile(past, SB_SAMPLE_TK)
    n_kb = past // tk
    row_spec = pl.BlockSpec((n, C_WIDTH), lambda b, j: (b + off, 0))
    cache_spec = pl.BlockSpec((None, None, tk * C_HEADS, C_DIM), lambda b, j: (layer, b, n_kb - 1 - j, 0))
    return pl.pallas_call(
        _sb_sample_kernel,
        grid=(DEC_BATCH, n_kb),
        in_specs=[row_spec, row_spec, row_spec, cache_spec, cache_spec],
        out_specs=pl.BlockSpec((n, C_WIDTH), lambda b, j: (b, 0)),
        out_shape=jax.ShapeDtypeStruct((DEC_BATCH * n, C_WIDTH), BF16),
        scratch_shapes=[pltpu.VMEM((C_HEADS * n, tk), F32), pltpu.VMEM((C_HEADS * n, tk), BF16),
                        pltpu.VMEM((C_HEADS * n, 1), F32), pltpu.VMEM((n, C_WIDTH), F32)],
        compiler_params=_params("arbitrary", "arbitrary"),
        name="sb_sample",
    )(q, k, v, cache_k, cache_v)


def _pack_bf16_pairs(x):
    half = x.shape[1] // 2
    bits = lax.bitcast_convert_type(x.astype(BF16).astype(F32), jnp.uint32)
    return (bits[:, :half] & jnp.uint32(0xFFFF0000)) | (bits[:, half:] >> 16)


def _unpack_bf16_pairs(w):
    hi = lax.bitcast_convert_type(w & jnp.uint32(0xFFFF0000), F32)
    lo = lax.bitcast_convert_type(w << 16, F32)
    return jnp.concatenate([hi, lo], axis=1).astype(BF16)


def _outproj_kernel(*refs, n_lhs, n_prompt_tiles, n_x):
    lhs = refs[:2 * n_lhs]
    x_refs = refs[2 * n_lhs:2 * n_lhs + n_x]
    w_ref, g_ref, b_ref, wr_ref, x1_ref, x1p_ref, lg_ref = refs[2 * n_lhs + n_x:]
    is_prompt = pl.program_id(0) < n_prompt_tiles
    tm = x1_ref.shape[0]
    half = tm // 2
    for rows in (slice(0, half), slice(half, tm)):
        k0 = 0
        y = None
        for a_p, a_s in zip(lhs[0::2], lhs[1::2]):
            kk = a_p.shape[1]
            part = _dot(jnp.where(is_prompt, a_p[rows, :], a_s[rows, :]), w_ref[k0:k0 + kk, :])
            y = part if y is None else y + part
            k0 += kk
        x1 = _layer_norm(np.float32(ALPHA) * _row_value(x_refs, is_prompt, rows) + y, g_ref[...], b_ref[...])
        x1_ref[rows, :] = x1
        x1p_ref[rows, :] = _pack_bf16_pairs(x1)
        lg_ref[:, rows] = lax.dot_general(wr_ref[...], x1, (((1,), (1,)), ((), ())), preferred_element_type=F32,
                                          precision=lax.Precision.HIGHEST)


def _outproj_ln(lhs, w, x, g, b, w_router_t, name):
    t = sum(a.shape[0] for a in x) if isinstance(x, tuple) else x.shape[0]
    tm = MOE_TM
    n_pt = lhs[0][0].shape[0] // tm
    assert all(a_p.shape[0] == n_pt * tm and a_s.shape[0] == t - n_pt * tm for a_p, a_s in lhs)
    full = lambda a: pl.BlockSpec(a.shape, lambda i: (0,) * a.ndim)
    lhs_specs, lhs_args = [], []
    for pair in lhs:
        specs, args = _row_specs(pair, tm, n_pt)
        lhs_specs += specs
        lhs_args += args
    x_specs, x_args = _row_specs(x, tm, n_pt)
    return pl.pallas_call(
        functools.partial(_outproj_kernel, n_lhs=len(lhs), n_prompt_tiles=n_pt, n_x=len(x_args)),
        grid=(t // tm,),
        in_specs=lhs_specs + x_specs + [full(w), full(g), full(b), full(w_router_t)],
        out_specs=[pl.BlockSpec((tm, D_MODEL), lambda i: (i, 0)), pl.BlockSpec((tm, D_MODEL // 2), lambda i: (i, 0)),
                   pl.BlockSpec((N_EXPERTS, tm), lambda i: (0, i))],
        out_shape=[jax.ShapeDtypeStruct((t, D_MODEL), F32), jax.ShapeDtypeStruct((t, D_MODEL // 2), jnp.uint32),
                   jax.ShapeDtypeStruct((N_EXPERTS, t), F32)],
        compiler_params=_params("arbitrary"),
        name=name,
    )(*lhs_args, *x_args, w, g, b, w_router_t)


MOE_TM = 256
EXPERT_BM = 256


def _route_select(lg, bias):
    e, tm = lg.shape
    ninf = np.float32(-np.inf)
    shift = int(np.log2(E_PER_GROUP))
    scores = jax.nn.sigmoid(lg)
    sel = scores + bias
    e_id = lax.broadcasted_iota(jnp.int32, (e, tm), 0)
    g_id = lax.shift_right_logical(e_id, shift)
    g3 = sel.reshape(N_GROUPS, E_PER_GROUP, tm)
    i3 = lax.broadcasted_iota(jnp.int32, g3.shape, 1)
    m1 = g3.max(axis=1, keepdims=True)
    first = jnp.where(g3 == m1, i3, E_PER_GROUP).min(axis=1, keepdims=True)
    m2 = jnp.where(i3 == first, ninf, g3).max(axis=1, keepdims=True)
    grp = jnp.broadcast_to(m1 + m2, g3.shape).reshape(e, tm)
    gsel = jnp.zeros((e, tm), jnp.int32)
    for _ in range(TOPK_GROUPS):
        m = grp.max(axis=0, keepdims=True)
        first = jnp.where(grp == m, e_id, e).min(axis=0, keepdims=True)
        chosen = g_id == lax.shift_right_logical(first, shift)
        gsel = jnp.where(chosen, 1, gsel)
        grp = jnp.where(chosen, ninf, grp)
    sel = jnp.where(gsel > 0, sel, NEG)
    picks = []
    for _ in range(TOP_K):
        m = sel.max(axis=0, keepdims=True)
        first = jnp.where(sel == m, e_id, e).min(axis=0, keepdims=True)
        oh = e_id == first
        picks.append(oh)
        sel = jnp.where(oh, ninf, sel)
    return scores, picks


def _route_kernel(lg_ref, bias_ref, dest_ref, gate_ref, cnt_ref, pad_ref, counts_ref, pads_ref, run_ref):
    phase = pl.program_id(0)
    i = pl.program_id(1)
    e, tm = lg_ref.shape
    scores, picks = _route_select(lg_ref[...], bias_ref[...])
    mask = jnp.zeros((e, tm), F32)
    for oh in picks:
        mask = jnp.where(oh, 1.0, mask)

    @pl.when((phase == 0) & (i == 0))
    def _():
        counts_ref[...] = jnp.zeros_like(counts_ref)

    @pl.when(phase == 0)
    def _():
        counts_ref[...] += mask.sum(axis=1, keepdims=True)

    @pl.when((phase == 1) & (i == 0))
    def _():
        counts = counts_ref[...]
        padded = jnp.ceil(counts * np.float32(1.0 / EXPERT_BM)) * np.float32(EXPERT_BM)
        r_i = lax.broadcasted_iota(jnp.int32, (e, e), 0)
        c_i = lax.broadcasted_iota(jnp.int32, (e, e), 1)
        before = (c_i < r_i).astype(F32)
        starts = jnp.dot(before, jnp.broadcast_to(padded, (e, LANES)), preferred_element_type=F32,
                         precision=lax.Precision.HIGHEST)
        pads_ref[...] = starts[:, 0:1]
        run_ref[...] = jnp.zeros_like(run_ref)
        cnt_ref[...] = jnp.broadcast_to(counts, cnt_ref.shape)
        pad_ref[...] = starts

    @pl.when(phase == 1)
    def _():
        r_i = lax.broadcasted_iota(jnp.int32, (tm, tm), 0)
        c_i = lax.broadcasted_iota(jnp.int32, (tm, tm), 1)
        earlier = (r_i < c_i).astype(BF16)
        slot = pads_ref[...] + run_ref[...] + _dot(mask.astype(BF16), earlier)
        run_ref[...] += mask.sum(axis=1, keepdims=True)
        k_i = lax.broadcasted_iota(jnp.int32, (TOP_K, tm), 0)
        dest = jnp.zeros((TOP_K, tm), F32)
        gate = jnp.zeros((TOP_K, tm), F32)
        total = jnp.zeros((1, tm), F32)
        for k, oh in enumerate(picks):
            d_k = jnp.where(oh, slot, 0.0).sum(axis=0, keepdims=True)
            g_k = jnp.where(oh, scores, 0.0).sum(axis=0, keepdims=True)
            total = total + g_k
            dest = jnp.where(k_i == k, d_k, dest)
            gate = jnp.where(k_i == k, g_k, gate)
        dest_ref[...] = dest.astype(jnp.int32)
        gate_ref[...] = gate / total * np.float32(ROUTED_SCALE)


def _route(lg_t, bias):
    e, t = lg_t.shape
    tm = MOE_TM
    n_t = t // tm
    tile = pl.BlockSpec((TOP_K, tm), lambda p, i: (0, i * p))
    meta = pl.BlockSpec((e, LANES), lambda p, i: (0, 0))
    return pl.pallas_call(
        _route_kernel,
        grid=(2, n_t),
        in_specs=[pl.BlockSpec((e, tm), lambda p, i: (0, i)), pl.BlockSpec((e, 1), lambda p, i: (0, 0))],
        out_specs=[tile, tile, meta, meta],
        out_shape=[jax.ShapeDtypeStruct((TOP_K, t), jnp.int32), jax.ShapeDtypeStruct((TOP_K, t), F32),
                   jax.ShapeDtypeStruct((e, LANES), F32), jax.ShapeDtypeStruct((e, LANES), F32)],
        scratch_shapes=[pltpu.VMEM((e, 1), F32), pltpu.VMEM((e, 1), F32), pltpu.VMEM((e, 1), F32)],
        compiler_params=_params("arbitrary", "arbitrary"),
        name="route",
    )(lg_t, bias)


def _row_copy(src, src_row, dst, dst_row, sem):
    return pltpu.make_async_copy(src.at[pl.ds(src_row, 1)], dst.at[pl.ds(dst_row, 1)], sem)


def _scatter_kernel(last_ref, nu_ref, dest_ref, x_ref, xs_out, zeros_ref, sem, zsem):
    i = pl.program_id(0)
    tm = x_ref.shape[0]
    n_blocks = xs_out.shape[0] // EXPERT_BM

    @pl.when(i == 0)
    def _():
        zeros_ref[...] = jnp.zeros_like(zeros_ref)

        def zero_copy(row):
            return pltpu.make_async_copy(zeros_ref, xs_out.at[pl.ds(pl.multiple_of(row, EXPERT_BM), EXPERT_BM)], zsem)

        def tail_row(b):
            return (nu_ref[0] + b) * EXPERT_BM

        def each(fn, n, act):
            def body(b, c):
                row = fn(b)

                @pl.when(row >= 0)
                def _():
                    act(zero_copy(row))
                return c
            lax.fori_loop(0, n, body, 0)

        start = lambda fn, n: each(fn, n, lambda c: c.start())
        wait = lambda fn, n: each(fn, n, lambda c: c.wait())

        start(lambda e: last_ref[e], N_EXPERTS)
        wait(lambda e: last_ref[e], N_EXPERTS)
        start(tail_row, n_blocks - nu_ref[0])
        wait(tail_row, n_blocks - nu_ref[0])

    def issue(t, c):
        for k in range(TOP_K):
            _row_copy(x_ref, t, xs_out, dest_ref[k * tm + t], sem).start(priority=k % 2)
        return c

    def drain(t, c):
        for k in range(TOP_K):
            _row_copy(x_ref, t, xs_out, dest_ref[k * tm + t], sem).wait()
        return c

    lax.fori_loop(0, tm, issue, 0)
    lax.fori_loop(0, tm, drain, 0)


def _scatter(last_row, n_used, dest_flat, x1p, n_slots):
    t, w = x1p.shape
    tm = MOE_TM
    grid_spec = pltpu.PrefetchScalarGridSpec(
        num_scalar_prefetch=2,
        grid=(t // tm,),
        in_specs=[pl.BlockSpec((TOP_K * tm,), lambda i, lr, nu: (i,), memory_space=pltpu.SMEM),
                  pl.BlockSpec((tm, w), lambda i, lr, nu: (i, 0))],
        out_specs=pl.BlockSpec(memory_space=pl.ANY),
        scratch_shapes=[pltpu.VMEM((EXPERT_BM, w), x1p.dtype), pltpu.SemaphoreType.DMA(()),
                        pltpu.SemaphoreType.DMA(())],
    )
    return pl.pallas_call(
        _scatter_kernel,
        grid_spec=grid_spec,
        out_shape=jax.ShapeDtypeStruct((n_slots, w), x1p.dtype),
        compiler_params=_params("arbitrary"),
        name="scatter",
    )(last_row, n_used, dest_flat, x1p)


def _expert_kernel(be_ref, nu_ref, slot_ref, nxt_ref, x_ref, wg_hbm, wu_hbm, wd_hbm, y_ref,
                   wgf, wuf, wdf, wgb, wub, wdb, sem, *, layer):
    i = pl.program_id(0)
    active = i < nu_ref[0]
    first = (i == 0) | (be_ref[i] != be_ref[jnp.maximum(i - 1, 0)])

    def weight_copies(e, s):
        return [pltpu.make_async_copy(src.at[layer, e], dst.at[s], sem.at[s])
                for src, dst in ((wg_hbm, wgf), (wu_hbm, wuf), (wd_hbm, wdf))]

    @pl.when(active & (i == 0))
    def _():
        for c in weight_copies(be_ref[0], slot_ref[0]):
            c.start()

    @pl.when(active & first)
    def _():
        s = slot_ref[i]
        for c in weight_copies(be_ref[i], s):
            c.wait()
        wgb[...] = wgf[s].astype(BF16)
        wub[...] = wuf[s].astype(BF16)
        wdb[...] = wdf[s].astype(BF16)

        @pl.when(nxt_ref[i] >= 0)
        def _():
            for c in weight_copies(nxt_ref[i], 1 - s):
                c.start()

    @pl.when(active)
    def _():
        x = _unpack_bf16_pairs(x_ref[...])
        g = _dot(x, wgb[...])
        h = (g * jax.nn.sigmoid(g)) * _dot(x, wub[...])
        y_ref[...] = _pack_bf16_pairs(_dot(h.astype(BF16), wdb[...]))

    @pl.when(jnp.logical_not(active))
    def _():
        y_ref[...] = jnp.zeros_like(y_ref)


def _experts(block_e, n_used, slot, nxt, xs, layer, wg, wu, wd):
    n_slots = xs.shape[0]
    n_blocks = n_slots // EXPERT_BM
    ff = wg.shape[3]
    any_spec = pl.BlockSpec(memory_space=pl.ANY)
    grid_spec = pltpu.PrefetchScalarGridSpec(
        num_scalar_prefetch=4,
        grid=(n_blocks,),
        in_specs=[pl.BlockSpec((EXPERT_BM, D_MODEL // 2), lambda i, be, nu, sl, nx: (jnp.minimum(i, nu[0] - 1), 0)),
                  any_spec, any_spec, any_spec],
        out_specs=pl.BlockSpec((EXPERT_BM, D_MODEL // 2), lambda i, be, nu, sl, nx: (i, 0)),
        scratch_shapes=[pltpu.VMEM((2, D_MODEL, ff), F32), pltpu.VMEM((2, D_MODEL, ff), F32),
                        pltpu.VMEM((2, ff, D_MODEL), F32),
                        pltpu.VMEM((D_MODEL, ff), BF16), pltpu.VMEM((D_MODEL, ff), BF16),
                        pltpu.VMEM((ff, D_MODEL), BF16), pltpu.SemaphoreType.DMA((2,))],
    )
    return pl.pallas_call(
        functools.partial(_expert_kernel, layer=layer),
        grid_spec=grid_spec,
        out_shape=jax.ShapeDtypeStruct((n_slots, D_MODEL // 2), jnp.uint32),
        compiler_params=_params("arbitrary"),
        name="experts",
    )(block_e, n_used, slot, nxt, xs, wg, wu, wd)


FFN_TM = 128


def _ffn_ple_kernel(dest_ref, next_ref, xp_ref, x_ref, gate_ref, y_hbm, wg_ref, wu_ref, wd_ref, g_ref, b_ref,
                    wp_ref, wpg_ref, *rest, n_prompt_tiles, n_p):
    p_refs, o_refs, buf, sem = rest[:n_p], rest[n_p:-2], rest[-2], rest[-1]
    i = pl.program_id(0)
    n = pl.num_programs(0)
    tm = x_ref.shape[0]
    per = MOE_TM // tm

    def row_copy(table, step, t, k, slot):
        base = lax.rem(step, per) * tm
        return _row_copy(y_hbm, table[k * MOE_TM + base + t], buf.at[slot, k], t, sem.at[slot])

    def issue(table, step, slot):
        def body(t, c):
            for k in range(TOP_K):
                row_copy(table, step, t, k, slot).start(priority=k % 2)
            return c
        lax.fori_loop(0, tm, body, 0)

    def drain(table, step, slot):
        def body(t, c):
            for k in range(TOP_K):
                row_copy(table, step, t, k, slot).wait()
            return c
        lax.fori_loop(0, tm, body, 0)

    cur = lax.rem(i, 2)

    @pl.when(i == 0)
    def _():
        issue(dest_ref, i, 0)

    @pl.when(i + 1 < n)
    def _():
        issue(next_ref, i + 1, 1 - cur)

    xb = _unpack_bf16_pairs(xp_ref[...])
    gte = _dot(xb, wg_ref[...])
    h = (gte * jax.nn.sigmoid(gte)) * _dot(xb, wu_ref[...])
    shared = _dot(h.astype(BF16), wd_ref[...])
    drain(dest_ref, i, cur)
    lo_half = hi_half = None
    for k in range(TOP_K):
        w = buf[cur, k]
        g = gate_ref[:, k:k + 1]
        hi = lax.bitcast_convert_type(w & jnp.uint32(0xFFFF0000), F32) * g
        lo = lax.bitcast_convert_type(w << 16, F32) * g
        hi_half = hi if hi_half is None else hi_half + hi
        lo_half = lo if lo_half is None else lo_half + lo
    routed = jnp.concatenate([hi_half, lo_half], axis=1)
    x2 = _layer_norm(np.float32(ALPHA) * x_ref[...] + (routed + shared), g_ref[...], b_ref[...])
    emb_gate = jax.nn.sigmoid(_dot(x2.astype(BF16), wpg_ref[...]))
    out = x2 + _dot(_row_value(p_refs, i < n_prompt_tiles).astype(BF16), wp_ref[...]) * emb_gate
    if len(o_refs) == 1:
        o_refs[0][...] = out
    else:
        @pl.when(i < n_prompt_tiles)
        def _():
            o_refs[0][...] = out

        @pl.when(i >= n_prompt_tiles)
        def _():
            o_refs[1][...] = out


def _ffn_ple(dest_flat, x1p, x1, gate_t, y, p, wg, wu, wd, g, b, w_ple, w_ple_gate, n_prompt=None):
    t = x1.shape[0]
    tm = FFN_TM
    assert MOE_TM % tm == 0 and t % MOE_TM == 0
    per = MOE_TM // tm
    n_t = t // tm
    n_pt = None if n_prompt is None else n_prompt // tm
    row = lambda a: pl.BlockSpec((tm, a.shape[1]), lambda i: (i, 0))
    p_specs, p_args = _row_specs(p, tm, p[0].shape[0] // tm if isinstance(p, tuple) else None)
    once = lambda a: pl.BlockSpec(a.shape, lambda i: (0,) * a.ndim, pipeline_mode=pl.Buffered(1))
    table = lambda f: pl.BlockSpec((TOP_K * MOE_TM,), lambda i: (f(i) // per,), memory_space=pltpu.SMEM)
    if n_prompt is None:
        out_specs = [row(x1)]
        out_shape = [jax.ShapeDtypeStruct((t, D_MODEL), F32)]
    else:
        out_specs = [pl.BlockSpec((tm, D_MODEL), lambda i: (jnp.minimum(i, n_pt - 1), 0)),
                     pl.BlockSpec((tm, D_MODEL), lambda i: (jnp.maximum(i - n_pt, 0), 0))]
        out_shape = [jax.ShapeDtypeStruct((n_prompt, D_MODEL), F32),
                     jax.ShapeDtypeStruct((t - n_prompt, D_MODEL), F32)]
    return pl.pallas_call(
        functools.partial(_ffn_ple_kernel, n_prompt_tiles=p[0].shape[0] // tm if isinstance(p, tuple) else n_pt,
                          n_p=len(p_args)),
        grid=(n_t,),
        in_specs=[table(lambda i: i), table(lambda i: jnp.minimum(i + 1, n_t - 1)),
                  row(x1p), row(x1), row(gate_t), pl.BlockSpec(memory_space=pl.ANY),
                  once(wg), once(wu), once(wd), once(g), once(b), once(w_ple), once(w_ple_gate)] + p_specs,
        out_specs=out_specs, out_shape=out_shape,
        scratch_shapes=[pltpu.VMEM((2, TOP_K, tm, D_MODEL // 2), jnp.uint32), pltpu.SemaphoreType.DMA((2,))],
        compiler_params=_params("arbitrary"),
        name="ffn_ple",
    )(dest_flat, dest_flat, x1p, x1, gate_t, y, wg, wu, wd, g, b, w_ple, w_ple_gate, *p_args)


def _rel_bias(table, qpos, kpos):
    idx = np.clip(qpos[:, None] - kpos[None, :], -REL_CLIP, REL_CLIP) + REL_CLIP
    return table[:, idx]


def _band_prompt_bias(table):
    tq, width = BAND_TQ, BAND_NWB * BAND_TQ
    span = width + tq - 1
    rel = (BAND_NWB - 1) * tq + tq - 1 - np.arange(span)
    strip = table[:, np.clip(rel, -REL_CLIP, REL_CLIP) + REL_CLIP]
    u = jnp.concatenate([strip[:, tq - 1:], strip[:, :1], strip[:, :tq - 1]], axis=1)
    flat = jnp.tile(u, (1, tq))[:, :tq * span]
    return flat.reshape(table.shape[0], tq, span)[:, :, :width]


def _band_mask(qpos, kpos):
    dc = qpos // CHUNK - kpos // CHUNK
    return (kpos >= 0) & (dc >= 0) & (dc <= B_PREV_CHUNKS)


def _post_block(x, lhs, w_out, p, ln1_g, ln1_b, ln2_g, ln2_b, w_router, b_router,
                layer, w_gate, w_up, w_down, ws_gate, ws_up, ws_down, w_ple, w_ple_gate, name, n_prompt):
    row = lambda a: a.reshape(1, -1)
    x1, x1p, lg_t = _outproj_ln(lhs, w_out, x, row(ln1_g), row(ln1_b), w_router.T, name)
    t = x1.shape[0]
    dest, gate, cnt, pad = _route(lg_t, b_router.astype(F32).reshape(N_EXPERTS, 1))

    n_blocks = -(-(t * TOP_K + N_EXPERTS * (EXPERT_BM - 1)) // EXPERT_BM)
    counts = cnt[:, 0].astype(jnp.int32)
    pad_end = pad[:, 0].astype(jnp.int32) + (counts + EXPERT_BM - 1) // EXPERT_BM * EXPERT_BM
    block_e = jnp.minimum((pad_end[:, None] <= jnp.arange(n_blocks)[None, :] * EXPERT_BM).sum(axis=0),
                          N_EXPERTS - 1).astype(jnp.int32)
    n_used = (pad_end[-1:] // EXPERT_BM).astype(jnp.int32)
    nonempty = counts > 0
    e_ids = jnp.arange(N_EXPERTS)
    later_ids = jnp.where(nonempty[None, :] & (e_ids[None, :] > e_ids[:, None]), e_ids[None, :], N_EXPERTS).min(axis=1)
    nxt_of = jnp.where(later_ids < N_EXPERTS, later_ids, -1)
    slot_of = (jnp.cumsum(nonempty) - 1) % 2
    in_block = (block_e[:, None] == e_ids[None, :]).astype(jnp.int32)
    nxt = (in_block * nxt_of[None, :]).sum(axis=1).astype(jnp.int32)
    slot = (in_block * slot_of[None, :]).sum(axis=1).astype(jnp.int32)
    dest_flat = dest.reshape(TOP_K, t // MOE_TM, MOE_TM).transpose(1, 0, 2).reshape(-1)

    last_row = jnp.where(counts > 0, pad_end - EXPERT_BM, -1).astype(jnp.int32)
    xs = _scatter(last_row, n_used, dest_flat, x1p, n_blocks * EXPERT_BM)
    y = _experts(block_e, n_used, slot, nxt, xs, layer, w_gate, w_up, w_down)
    return _ffn_ple(dest_flat, x1p, x1, gate.T, y, p, ws_gate, ws_up, ws_down, row(ln2_g), row(ln2_b),
                    w_ple, w_ple_gate, n_prompt)


def kernel(x_prompt, x_sample, cache_b_k, cache_b_v, cache_c_k, cache_c_v, p_prompt, p_sample, w_in_ab, w_out_ab, sgu_w, sgu_b, sgu_ln_g, sgu_ln_b, rel_bias_tab, w_in_c, w_out_c, ln_mix_g, ln_mix_b, ln_ffn_g, ln_ffn_b, w_router, b_router, w_gate, w_up, w_down, ws_gate, ws_up, ws_down, w_ple, w_ple_gate):
    n_p = BATCH * SEQ
    n_s = DEC_BATCH * DEC_SEQ
    x = (x_prompt.reshape(n_p, D_MODEL), x_sample.reshape(n_s, D_MODEL))
    p_rows = lambda i: (p_prompt[i].reshape(n_p, PLE_DIM), p_sample[i].reshape(n_s, PLE_DIM))
    bf = lambda a: a.astype(BF16)
    outs = {k: [] for k in ("bk_p", "bv_p", "bk_s", "bv_s", "av_s", "ck_p", "cv_p", "ck_s", "cv_s")}
    kv_outs = ((F32, True), (BF16, False))

    for i in range(DEPTH):
        j = i // 2
        if i % 2 == 0:
            plain = lambda d: ((d, False),)
            hu, hv, q, k_p, k_s, kb16, v_p, v_s, vb16 = _proj(
                x, bf(w_in_ab[j]), 0,
                [(0, A_WIDTH, plain(F32)), (A_WIDTH, A_WIDTH, plain(F32)), (2 * A_WIDTH, B_WIDTH, plain(BF16)),
                 (2 * A_WIDTH + B_WIDTH, B_WIDTH, kv_outs), (2 * A_WIDTH + 2 * B_WIDTH, B_WIDTH, kv_outs)],
                "proj_ab", n_p)

            bs_t = sgu_b[j].T
            a_p, = _sgu(hu, hv, sgu_w[j], bs_t, sgu_ln_g[j], sgu_ln_b[j], A_CHUNK, 0, n_p // A_CHUNK, False)
            a_s, va_s = _sgu(hu, hv, sgu_w[j][:, :DEC_SEQ, :DEC_SEQ], bs_t[:DEC_SEQ], sgu_ln_g[j], sgu_ln_b[j],
                             DEC_SEQ, n_p, DEC_BATCH, True)

            b_p = _band_prompt(q, kb16, vb16, _band_prompt_bias(rel_bias_tab[j]))
            n_cache = cache_b_k.shape[2]
            qs = PAST_LEN + np.arange(DEC_SEQ)
            kc = PAST_LEN - n_cache + np.arange(n_cache)
            b_s = _band_sample(
                q, kb16, vb16, cache_b_k.reshape(N_EVEN, DEC_BATCH, n_cache * B_HEADS, B_DIM),
                cache_b_v.reshape(N_EVEN, DEC_BATCH, n_cache * B_HEADS, B_DIM), j,
                _rel_bias(rel_bias_tab[j], qs, kc), _rel_bias(rel_bias_tab[j], qs, qs),
                jnp.asarray(_band_mask(qs[:, None], kc[None, :]), F32),
                jnp.asarray(_band_mask(qs[:, None], qs[None, :]), F32))

            keep = min(B_WINDOW, SEQ)
            outs["bk_p"].append(k_p.reshape(BATCH, SEQ, B_HEADS, B_DIM)[:, SEQ - keep:])
            outs["bv_p"].append(v_p.reshape(BATCH, SEQ, B_HEADS, B_DIM)[:, SEQ - keep:])
            outs["bk_s"].append(k_s.reshape(DEC_BATCH, DEC_SEQ, B_HEADS, B_DIM))
            outs["bv_s"].append(v_s.reshape(DEC_BATCH, DEC_SEQ, B_HEADS, B_DIM))
            outs["av_s"].append(va_s.reshape(DEC_BATCH, DEC_SEQ, A_HEADS, A_DIM))
            lhs, w_out, name = [(a_p, a_s), (b_p, b_s)], bf(w_out_ab[j]), "outproj_ab"
        else:
            w_in = bf(w_in_c[j])
            q, k_p, k_s, kb16 = _proj(x, w_in, 0, [(0, C_WIDTH, ((BF16, False),)), (C_WIDTH, C_WIDTH, kv_outs)],
                                      "proj_cqk", n_p)
            v_p, v_s, vb16 = _proj(x, w_in, 2, [(0, C_WIDTH, kv_outs)], "proj_cv", n_p)
            o_p = _sb_prompt(q, kb16, vb16)
            rows = lambda c: c.reshape(N_ODD, DEC_BATCH, PAST_LEN * C_HEADS, C_DIM)
            o_s = _sb_sample(q, kb16, vb16, rows(cache_c_k), rows(cache_c_v), j)
            outs["ck_p"].append(k_p.reshape(BATCH, SEQ, C_HEADS, C_DIM))
            outs["cv_p"].append(v_p.reshape(BATCH, SEQ, C_HEADS, C_DIM))
            outs["ck_s"].append(k_s.reshape(DEC_BATCH, DEC_SEQ, C_HEADS, C_DIM))
            outs["cv_s"].append(v_s.reshape(DEC_BATCH, DEC_SEQ, C_HEADS, C_DIM))
            lhs, w_out, name = [(o_p, o_s)], bf(w_out_c[j]), "outproj_c"

        x = _post_block(x, lhs, w_out, p_rows(i), ln_mix_g[i], ln_mix_b[i], ln_ffn_g[i], ln_ffn_b[i],
                        w_router[i], b_router[i], i, w_gate, w_up, w_down,
                        bf(ws_gate[i]), bf(ws_up[i]), bf(ws_down[i]), bf(w_ple[i]), bf(w_ple_gate[i]), name,
                        n_p if i == DEPTH - 1 else None)
        if i < DEPTH - 1:
            x, = x

    y_p, y_s = x
    st = lambda key: jnp.stack(outs[key])
    return (y_p.reshape(BATCH, SEQ, D_MODEL), y_s.reshape(DEC_BATCH, DEC_SEQ, D_MODEL),
            st("bk_p"), st("bv_p"), st("bk_s"), st("bv_s"), st("av_s"),
            st("ck_p"), st("cv_p"), st("ck_s"), st("cv_s"))
```

```python
import functools

import numpy as np
import jax
import jax.numpy as jnp
from jax import lax
from jax.experimental import pallas as pl
from jax.experimental.pallas import tpu as pltpu

D_MODEL = 2048
BATCH = 2
SEQ = 4096
DEPTH = 2
DEC_BATCH = 32
DEC_SEQ = 16
PAST_LEN = 2048

CHUNK = 64
N_EVEN = (DEPTH + 1) // 2
N_ODD = DEPTH // 2
A_CHUNK = 128
A_HEADS = 8
A_DIM = 128
A_WIDTH = A_HEADS * A_DIM
B_HEADS = 8
B_DIM = 128
B_WIDTH = B_HEADS * B_DIM
B_PREV_CHUNKS = 8
B_WINDOW = B_PREV_CHUNKS * CHUNK
REL_CLIP = 128
C_HEADS = 16
C_DIM = 128
C_WIDTH = C_HEADS * C_DIM
N_EXPERTS = 64
N_GROUPS = 8
E_PER_GROUP = N_EXPERTS // N_GROUPS
TOPK_GROUPS = 4
TOP_K = 8
EXPERT_FF = 512
SHARED_FF = 512
ROUTED_SCALE = 2.5
PLE_DIM = 256
LN_EPS = 1e-5
ALPHA = (2 * DEPTH) ** 0.25
NEG = -1e9

LANES = 128
VMEM_LIMIT_BYTES = 56 * 1024 * 1024

BF16 = jnp.bfloat16
F32 = jnp.float32


def _params(*sem):
    return pltpu.CompilerParams(dimension_semantics=sem, vmem_limit_bytes=VMEM_LIMIT_BYTES)


def _tile(n, pref):
    if n <= pref:
        return n
    for t in range(pref, 7, -1):
        if n % t == 0 and t % 8 == 0:
            return t
    return n


def _dot(a, b):
    return jnp.dot(a, b, preferred_element_type=F32)


def _dot_nt(a, b):
    return lax.dot_general(a, b, (((1,), (1,)), ((), ())), preferred_element_type=F32)


def _layer_norm(z, g, b):
    mu = jnp.mean(z, axis=-1, keepdims=True)
    zc = z - mu
    var = jnp.mean(zc * zc, axis=-1, keepdims=True)
    return zc * lax.rsqrt(var + LN_EPS) * g + b


def _gelu(x):
    return x * (lax.erf(x * np.float32(1.0 / np.sqrt(2.0))) + 1.0) * 0.5


def _softplus(z):
    return jnp.maximum(z, 0.0) + jnp.log(1.0 + jnp.exp(-jnp.abs(z)))


def _row_specs(a, tm, n_prompt_tiles):
    if not isinstance(a, tuple):
        return [pl.BlockSpec((tm, a.shape[1]), lambda i: (i, 0))], [a]
    a_p, a_s = a
    assert a_p.shape[0] == n_prompt_tiles * tm and a_s.shape[0] % tm == 0
    return ([pl.BlockSpec((tm, a_p.shape[1]), lambda i: (jnp.minimum(i, n_prompt_tiles - 1), 0)),
             pl.BlockSpec((tm, a_s.shape[1]), lambda i: (jnp.maximum(i - n_prompt_tiles, 0), 0))], [a_p, a_s])


def _row_value(refs, is_prompt, rows=slice(None)):
    if len(refs) == 1:
        return refs[0][rows, :]
    return jnp.where(is_prompt, refs[0][rows, :], refs[1][rows, :])


def _proj_kernel(*refs, segs, n_prompt_tiles, n_x):
    x_refs, w_ref, o_refs = refs[:n_x], refs[n_x], refs[n_x + 1:]
    i = pl.program_id(0)
    xb = _row_value(x_refs, i < n_prompt_tiles).astype(BF16)
    refs = iter(o_refs)
    for col0, ncols, outs in segs:
        acc = _dot(xb, w_ref[:, col0:col0 + ncols])
        for _, is_split in outs:
            if is_split:
                o_p, o_s = next(refs), next(refs)

                @pl.when(i < n_prompt_tiles)
                def _(o_p=o_p, acc=acc):
                    o_p[...] = acc.astype(o_p.dtype)

                @pl.when(i >= n_prompt_tiles)
                def _(o_s=o_s, acc=acc):
                    o_s[...] = acc.astype(o_s.dtype)
            else:
                o = next(refs)
                o[...] = acc.astype(o.dtype)


def _proj(x, w, col_block, segs, name, n_prompt):
    m = sum(a.shape[0] for a in x) if isinstance(x, tuple) else x.shape[0]
    k = (x[0] if isinstance(x, tuple) else x).shape[1]
    width = sum(n for _, n, _ in segs)
    tm = _tile(np.gcd(n_prompt, m - n_prompt), 256)
    n_pt = n_prompt // tm
    x_specs, x_args = _row_specs(x, tm, n_pt)
    out_specs, out_shape = [], []
    for _, ncols, outs in segs:
        for dtype, is_split in outs:
            if is_split:
                out_specs += [pl.BlockSpec((tm, ncols), lambda i: (jnp.minimum(i, n_pt - 1), 0)),
                              pl.BlockSpec((tm, ncols), lambda i: (jnp.maximum(i - n_pt, 0), 0))]
                out_shape += [jax.ShapeDtypeStruct((n_prompt, ncols), dtype),
                              jax.ShapeDtypeStruct((m - n_prompt, ncols), dtype)]
            else:
                out_specs.append(pl.BlockSpec((tm, ncols), lambda i: (i, 0)))
                out_shape.append(jax.ShapeDtypeStruct((m, ncols), dtype))
    return pl.pallas_call(
        functools.partial(_proj_kernel, segs=tuple(segs), n_prompt_tiles=n_pt, n_x=len(x_args)),
        grid=(m // tm,),
        in_specs=x_specs + [pl.BlockSpec((k, width), lambda i: (0, col_block), pipeline_mode=pl.Buffered(1))],
        out_specs=out_specs, out_shape=out_shape,
        compiler_params=_params("arbitrary"),
        name=name,
    )(*x_args, w)


def _sgu_kernel(hu_ref, hv_ref, w_ref, bs_ref, g_ref, b_ref, a_ref, *v_out, emit_v):
    rows = hu_ref.shape[0]
    r_i = lax.broadcasted_iota(jnp.int32, (rows, rows), 0)
    c_i = lax.broadcasted_iota(jnp.int32, (rows, rows), 1)
    causal = c_i <= r_i
    for h in range(A_HEADS):
        sl = slice(h * A_DIM, (h + 1) * A_DIM)
        u = _gelu(hu_ref[:, sl])
        v = _layer_norm(_gelu(hv_ref[:, sl]), g_ref[h:h + 1, :], b_ref[h:h + 1, :])
        w = jnp.where(causal, w_ref[h], 0.0).astype(BF16)
        mix = _dot(w, v.astype(BF16)) + bs_ref[:, h:h + 1]
        a_ref[:, sl] = (u * mix).astype(a_ref.dtype)
        if emit_v:
            v_out[0][:, sl] = v


def _sgu(hu, hv, w_s, bs_t, g, b, rows, row0, n_chunks, emit_v):
    off = row0 // rows
    in_spec = pl.BlockSpec((rows, A_WIDTH), lambda c: (c + off, 0))
    out_spec = pl.BlockSpec((rows, A_WIDTH), lambda c: (c, 0))
    full = lambda a: pl.BlockSpec(a.shape, lambda c: (0,) * a.ndim)
    out_specs = [out_spec]
    out_shape = [jax.ShapeDtypeStruct((n_chunks * rows, A_WIDTH), BF16)]
    if emit_v:
        out_specs.append(out_spec)
        out_shape.append(jax.ShapeDtypeStruct((n_chunks * rows, A_WIDTH), F32))
    return pl.pallas_call(
        functools.partial(_sgu_kernel, emit_v=emit_v),
        grid=(n_chunks,),
        in_specs=[in_spec, in_spec, full(w_s), full(bs_t), full(g), full(b)],
        out_specs=out_specs, out_shape=out_shape,
        compiler_params=_params("arbitrary"),
        name="sgu_sample" if emit_v else "sgu_prompt",
    )(hu, hv, w_s, bs_t, g, b)


BAND_TQ = 2 * CHUNK
BAND_NWB = B_WINDOW // BAND_TQ + 1


BAND_HP = 8


def _band_prompt_kernel(q_ref, k_ref, v_ref, bias_ref, o_ref):
    t = pl.program_id(2)
    tq = BAND_TQ
    shift = int(np.log2(CHUNK))
    q_chunk = lax.shift_right_arithmetic(t * tq + lax.broadcasted_iota(jnp.int32, (tq, 1), 0), shift)
    lane = lax.broadcasted_iota(jnp.int32, (1, tq), 1)
    starts, valids = [], []
    for j in range(BAND_NWB):
        kb = t - (BAND_NWB - 1) + j
        starts.append(pl.multiple_of(jnp.maximum(kb, 0) * tq, tq))
        kpos = kb * tq + lane
        dc = q_chunk - lax.shift_right_arithmetic(kpos, shift)
        valids.append((kpos >= 0) & (dc >= 0) & (dc <= B_PREV_CHUNKS))
    for h in range(BAND_HP):
        sl = slice(h * B_DIM, (h + 1) * B_DIM)
        q = q_ref[:, sl]
        s_blocks = []
        for j in range(BAND_NWB):
            s = (_dot_nt(q, k_ref[pl.ds(starts[j], tq), sl]) * np.float32(B_DIM ** -0.5)
                 + bias_ref[h, :, j * tq:(j + 1) * tq])
            s_blocks.append(jnp.where(valids[j], s, NEG))
        m = s_blocks[0].max(axis=-1, keepdims=True)
        for s in s_blocks[1:]:
            m = jnp.maximum(m, s.max(axis=-1, keepdims=True))
        acc = jnp.zeros((tq, B_DIM), F32)
        den = jnp.zeros((tq, 1), F32)
        for j, s in enumerate(s_blocks):
            p = jnp.exp(s - m)
            den = den + p.sum(axis=-1, keepdims=True)
            acc = acc + _dot(p.astype(BF16), v_ref[pl.ds(starts[j], tq), sl])
        o_ref[:, sl] = (acc / den).astype(o_ref.dtype)


def _band_prompt(q, k, v, bias):
    assert B_HEADS % BAND_HP == 0
    t_all = BATCH * SEQ
    n_t = SEQ // BAND_TQ
    q_spec = pl.BlockSpec((BAND_TQ, BAND_HP * B_DIM), lambda b, h, t: (b * n_t + t, h))
    kv_spec = pl.BlockSpec((SEQ, BAND_HP * B_DIM), lambda b, h, t: (b, h))
    return pl.pallas_call(
        _band_prompt_kernel,
        grid=(BATCH, B_HEADS // BAND_HP, n_t),
        in_specs=[q_spec, kv_spec, kv_spec,
                  pl.BlockSpec((BAND_HP, BAND_TQ, BAND_NWB * BAND_TQ), lambda b, h, t: (h, 0, 0))],
        out_specs=q_spec,
        out_shape=jax.ShapeDtypeStruct((t_all, B_WIDTH), BF16),
        compiler_params=_params("arbitrary", "arbitrary", "arbitrary"),
        name="band_prompt",
    )(q, k, v, bias)


def _band_sample_kernel(q_ref, kn_ref, vn_ref, kc_ref, vc_ref, bc_ref, bn_ref, mc_ref, mn_ref, o_ref):
    scale = np.float32(B_DIM ** -0.5)
    n_cache = kc_ref.shape[0] // B_HEADS
    for h in range(B_HEADS):
        sl = slice(h * B_DIM, (h + 1) * B_DIM)
        q = q_ref[:, sl]
        s_c = _dot_nt(q, kc_ref[pl.ds(h, n_cache, stride=B_HEADS), :].astype(BF16)) * scale + bc_ref[h]
        s_n = _dot_nt(q, kn_ref[:, sl]) * scale + bn_ref[h]
        s_c = jnp.where(mc_ref[...] > 0, s_c, NEG)
        s_n = jnp.where(mn_ref[...] > 0, s_n, NEG)
        m = jnp.maximum(s_c.max(axis=-1, keepdims=True), s_n.max(axis=-1, keepdims=True))
        p_c = jnp.exp(s_c - m)
        p_n = jnp.exp(s_n - m)
        den = p_c.sum(axis=-1, keepdims=True) + p_n.sum(axis=-1, keepdims=True)
        acc = (_dot(p_c.astype(BF16), vc_ref[pl.ds(h, n_cache, stride=B_HEADS), :].astype(BF16))
               + _dot(p_n.astype(BF16), vn_ref[:, sl]))
        o_ref[:, sl] = (acc / den).astype(o_ref.dtype)


def _band_sample(q, k, v, cache_k, cache_v, layer, bias_c, bias_n, mask_c, mask_n):
    n = DEC_SEQ
    off = BATCH * SEQ // n
    n_cache = cache_k.shape[2] // B_HEADS
    row_spec = pl.BlockSpec((n, B_WIDTH), lambda b: (b + off, 0))
    cache_spec = pl.BlockSpec((None, None, n_cache * B_HEADS, B_DIM), lambda b: (layer, b, 0, 0))
    full = lambda a: pl.BlockSpec(a.shape, lambda b: (0,) * a.ndim)
    return pl.pallas_call(
        _band_sample_kernel,
        grid=(DEC_BATCH,),
        in_specs=[row_spec, row_spec, row_spec, cache_spec, cache_spec,
                  full(bias_c), full(bias_n), full(mask_c), full(mask_n)],
        out_specs=pl.BlockSpec((n, B_WIDTH), lambda b: (b, 0)),
        out_shape=jax.ShapeDtypeStruct((DEC_BATCH * n, B_WIDTH), BF16),
        compiler_params=_params("arbitrary"),
        name="band_sample",
    )(q, k, v, cache_k, cache_v, bias_c, bias_n, mask_c, mask_n)


SB_TQ = 256
SB_TK = 256
SB_SAMPLE_TK = 512
SB_HP = 4


def _sb_weights(z, carry, upper):
    sp = _softplus(z)
    hi = sp.astype(BF16)
    lo = (sp - hi.astype(F32)).astype(BF16)
    later = _dot(hi, upper) + _dot(lo, upper)
    w = jnp.exp((z - sp) - (carry + later))
    return w, carry + later[:, 0:1] + sp[:, 0:1]


def _upper(n):
    r_i = lax.broadcasted_iota(jnp.int32, (n, n), 0)
    c_i = lax.broadcasted_iota(jnp.int32, (n, n), 1)
    return (r_i > c_i).astype(BF16)


SB_ROWS = 32


def _sb_prompt_kernel(q_ref, k_ref, v_ref, o_ref, z_ref, ls_ref, hl_ref, lat_ref, w_ref, carry_ref, acc_ref):
    i = pl.program_id(2)
    tq, tk = SB_TQ, SB_TK
    upper2 = jnp.concatenate([_upper(tk)] * 2, axis=0)
    scale = np.float32(C_DIM ** -0.5)
    chunks = [slice(r, r + SB_ROWS) for r in range(0, tq, SB_ROWS)]
    r_i = lax.broadcasted_iota(jnp.int32, (SB_ROWS, tk), 0)
    c_i = lax.broadcasted_iota(jnp.int32, (SB_ROWS, tk), 1)

    def block(j, diagonal):
        start = pl.multiple_of(j * tk, tk)
        for h in range(SB_HP):
            sl = slice(h * C_DIM, (h + 1) * C_DIM)
            z_ref[h] = _dot_nt(q_ref[:, sl], k_ref[pl.ds(start, tk), sl])
        for h in range(SB_HP):
            for ci, rows in enumerate(chunks):
                z = z_ref[h, rows, :] * scale
                if diagonal:
                    z = jnp.where(c_i < r_i + ci * SB_ROWS, z, NEG)
                sp = _softplus(z)
                hi = sp.astype(BF16)
                ls_ref[h, rows, :] = z - sp
                hl_ref[h, rows, 0:tk] = hi
                hl_ref[h, rows, tk:2 * tk] = (sp - hi.astype(F32)).astype(BF16)
        for h in range(SB_HP):
            lat_ref[h] = _dot(hl_ref[h], upper2)
        for h in range(SB_HP):
            for rows in chunks:
                later = lat_ref[h, rows, :]
                carry = carry_ref[h, rows, :]
                w_ref[h, rows, :] = jnp.exp(ls_ref[h, rows, :] - (carry + later)).astype(BF16)
                sp0 = hl_ref[h, rows, 0:1].astype(F32) + hl_ref[h, rows, tk:tk + 1].astype(F32)
                carry_ref[h, rows, :] = carry + later[:, 0:1] + sp0
        for h in range(SB_HP):
            sl = slice(h * C_DIM, (h + 1) * C_DIM)
            acc_ref[h] += _dot(w_ref[h], v_ref[pl.ds(start, tk), sl])

    carry_ref[...] = jnp.zeros_like(carry_ref)
    acc_ref[...] = jnp.zeros_like(acc_ref)
    block(i, True)

    def body(jj, c):
        block(i - 1 - jj, False)
        return c

    lax.fori_loop(0, i, body, 0)
    for h in range(SB_HP):
        o_ref[:, h * C_DIM:(h + 1) * C_DIM] = acc_ref[h].astype(o_ref.dtype)


def _sb_prompt(q, k, v):
    assert SB_TQ == SB_TK and C_HEADS % SB_HP == 0
    t_all = BATCH * SEQ
    n_q = SEQ // SB_TQ
    q_spec = pl.BlockSpec((SB_TQ, SB_HP * C_DIM), lambda b, h, i: (b * n_q + i, h))
    kv_spec = pl.BlockSpec((SEQ, SB_HP * C_DIM), lambda b, h, i: (b, h))
    return pl.pallas_call(
        _sb_prompt_kernel,
        grid=(BATCH, C_HEADS // SB_HP, n_q),
        in_specs=[q_spec, kv_spec, kv_spec],
        out_specs=q_spec,
        out_shape=jax.ShapeDtypeStruct((t_all, C_WIDTH), BF16),
        scratch_shapes=[pltpu.VMEM((SB_HP, SB_TQ, SB_TK), F32), pltpu.VMEM((SB_HP, SB_TQ, SB_TK), F32),
                        pltpu.VMEM((SB_HP, SB_TQ, 2 * SB_TK), BF16), pltpu.VMEM((SB_HP, SB_TQ, SB_TK), F32),
                        pltpu.VMEM((SB_HP, SB_TQ, SB_TK), BF16), pltpu.VMEM((SB_HP, SB_TQ, 1), F32),
                        pltpu.VMEM((SB_HP, SB_TQ, C_DIM), F32)],
        compiler_params=_params("arbitrary", "arbitrary", "arbitrary"),
        name="sb_prompt",
    )(q, k, v)


def _sb_sample_kernel(q_ref, kn_ref, vn_ref, kc_ref, vc_ref, o_ref, z_ref, w_ref, carry_ref, acc_ref):
    jj = pl.program_id(1)
    n = DEC_SEQ
    tk = kc_ref.shape[0] // C_HEADS
    scale = np.float32(C_DIM ** -0.5)
    col = lambda h: slice(h * C_DIM, (h + 1) * C_DIM)

    def all_heads(k_of, v_of, width, mask):
        for h in range(C_HEADS):
            z_ref[h * n:(h + 1) * n, 0:width] = _dot_nt(q_ref[:, col(h)], k_of(h)) * scale
        z = z_ref[:, 0:width]
        if mask is not None:
            z = jnp.where(mask, z, NEG)
        w, carry = _sb_weights(z, carry_ref[...], _upper(width))
        carry_ref[...] = carry
        w_ref[:, 0:width] = w.astype(BF16)
        for h in range(C_HEADS):
            acc_ref[:, col(h)] += _dot(w_ref[h * n:(h + 1) * n, 0:width], v_of(h))

    @pl.when(jj == 0)
    def _():
        carry_ref[...] = jnp.zeros_like(carry_ref)
        acc_ref[...] = jnp.zeros_like(acc_ref)
        r_i = lax.broadcasted_iota(jnp.int32, (C_HEADS * n, n), 0) % n
        c_i = lax.broadcasted_iota(jnp.int32, (C_HEADS * n, n), 1)
        all_heads(lambda h: kn_ref[:, col(h)], lambda h: vn_ref[:, col(h)], n, c_i < r_i)

    all_heads(lambda h: kc_ref[pl.ds(h, tk, stride=C_HEADS), :].astype(BF16),
              lambda h: vc_ref[pl.ds(h, tk, stride=C_HEADS), :].astype(BF16), tk, None)

    @pl.when(jj == pl.num_programs(1) - 1)
    def _():
        o_ref[...] = acc_ref[...].astype(o_ref.dtype)


def _sb_sample(q, k, v, cache_k, cache_v, layer):
    n = DEC_SEQ
    off = BATCH * SEQ // n
    past = cache_k.shape[2] // C_HEADS
    tk = _tile(past, SB_SAMPLE_TK)
    n_kb = past // tk
    row_spec = pl.BlockSpec((n, C_WIDTH), lambda b, j: (b + off, 0))
    cache_spec = pl.BlockSpec((None, None, tk * C_HEADS, C_DIM), lambda b, j: (layer, b, n_kb - 1 - j, 0))
    return pl.pallas_call(
        _sb_sample_kernel,
        grid=(DEC_BATCH, n_kb),
        in_specs=[row_spec, row_spec, row_spec, cache_spec, cache_spec],
        out_specs=pl.BlockSpec((n, C_WIDTH), lambda b, j: (b, 0)),
        out_shape=jax.ShapeDtypeStruct((DEC_BATCH * n, C_WIDTH), BF16),
        scratch_shapes=[pltpu.VMEM((C_HEADS * n, tk), F32), pltpu.VMEM((C_HEADS * n, tk), BF16),
                        pltpu.VMEM((C_HEADS * n, 1), F32), pltpu.VMEM((n, C_WIDTH), F32)],
        compiler_params=_params("arbitrary", "arbitrary"),
        name="sb_sample",
    )(q, k, v, cache_k, cache_v)


def _pack_bf16_pairs(x):
    half = x.shape[1] // 2
    bits = lax.bitcast_convert_type(x.astype(BF16).astype(F32), jnp.uint32)
    return (bits[:, :half] & jnp.uint32(0xFFFF0000)) | (bits[:, half:] >> 16)


def _unpack_bf16_pairs(w):
    hi = lax.bitcast_convert_type(w & jnp.uint32(0xFFFF0000), F32)
    lo = lax.bitcast_convert_type(w << 16, F32)
    return jnp.concatenate([hi, lo], axis=1).astype(BF16)


def _outproj_kernel(*refs, n_lhs, n_prompt_tiles, n_x):
    lhs = refs[:2 * n_lhs]
    x_refs = refs[2 * n_lhs:2 * n_lhs + n_x]
    w_ref, g_ref, b_ref, wr_ref, x1_ref, x1p_ref, lg_ref = refs[2 * n_lhs + n_x:]
    is_prompt = pl.program_id(0) < n_prompt_tiles
    tm = x1_ref.shape[0]
    half = tm // 2
    for rows in (slice(0, half), slice(half, tm)):
        k0 = 0
        y = None
        for a_p, a_s in zip(lhs[0::2], lhs[1::2]):
            kk = a_p.shape[1]
            part = _dot(jnp.where(is_prompt, a_p[rows, :], a_s[rows, :]), w_ref[k0:k0 + kk, :])
            y = part if y is None else y + part
            k0 += kk
        x1 = _layer_norm(np.float32(ALPHA) * _row_value(x_refs, is_prompt, rows) + y, g_ref[...], b_ref[...])
        x1_ref[rows, :] = x1
        x1p_ref[rows, :] = _pack_bf16_pairs(x1)
        lg_ref[:, rows] = lax.dot_general(wr_ref[...], x1, (((1,), (1,)), ((), ())), preferred_element_type=F32,
                                          precision=lax.Precision.HIGHEST)


def _outproj_ln(lhs, w, x, g, b, w_router_t, name):
    t = sum(a.shape[0] for a in x) if isinstance(x, tuple) else x.shape[0]
    tm = MOE_TM
    n_pt = lhs[0][0].shape[0] // tm
    assert all(a_p.shape[0] == n_pt * tm and a_s.shape[0] == t - n_pt * tm for a_p, a_s in lhs)
    full = lambda a: pl.BlockSpec(a.shape, lambda i: (0,) * a.ndim)
    lhs_specs, lhs_args = [], []
    for pair in lhs:
        specs, args = _row_specs(pair, tm, n_pt)
        lhs_specs += specs
        lhs_args += args
    x_specs, x_args = _row_specs(x, tm, n_pt)
    return pl.pallas_call(
        functools.partial(_outproj_kernel, n_lhs=len(lhs), n_prompt_tiles=n_pt, n_x=len(x_args)),
        grid=(t // tm,),
        in_specs=lhs_specs + x_specs + [full(w), full(g), full(b), full(w_router_t)],
        out_specs=[pl.BlockSpec((tm, D_MODEL), lambda i: (i, 0)), pl.BlockSpec((tm, D_MODEL // 2), lambda i: (i, 0)),
                   pl.BlockSpec((N_EXPERTS, tm), lambda i: (0, i))],
        out_shape=[jax.ShapeDtypeStruct((t, D_MODEL), F32), jax.ShapeDtypeStruct((t, D_MODEL // 2), jnp.uint32),
                   jax.ShapeDtypeStruct((N_EXPERTS, t), F32)],
        compiler_params=_params("arbitrary"),
        name=name,
    )(*lhs_args, *x_args, w, g, b, w_router_t)


MOE_TM = 256
EXPERT_BM = 256


def _route_select(lg, bias):
    e, tm = lg.shape
    ninf = np.float32(-np.inf)
    shift = int(np.log2(E_PER_GROUP))
    scores = jax.nn.sigmoid(lg)
    sel = scores + bias
    e_id = lax.broadcasted_iota(jnp.int32, (e, tm), 0)
    g_id = lax.shift_right_logical(e_id, shift)
    g3 = sel.reshape(N_GROUPS, E_PER_GROUP, tm)
    i3 = lax.broadcasted_iota(jnp.int32, g3.shape, 1)
    m1 = g3.max(axis=1, keepdims=True)
    first = jnp.where(g3 == m1, i3, E_PER_GROUP).min(axis=1, keepdims=True)
    m2 = jnp.where(i3 == first, ninf, g3).max(axis=1, keepdims=True)
    grp = jnp.broadcast_to(m1 + m2, g3.shape).reshape(e, tm)
    gsel = jnp.zeros((e, tm), jnp.int32)
    for _ in range(TOPK_GROUPS):
        m = grp.max(axis=0, keepdims=True)
        first = jnp.where(grp == m, e_id, e).min(axis=0, keepdims=True)
        chosen = g_id == lax.shift_right_logical(first, shift)
        gsel = jnp.where(chosen, 1, gsel)
        grp = jnp.where(chosen, ninf, grp)
    sel = jnp.where(gsel > 0, sel, NEG)
    picks = []
    for _ in range(TOP_K):
        m = sel.max(axis=0, keepdims=True)
        first = jnp.where(sel == m, e_id, e).min(axis=0, keepdims=True)
        oh = e_id == first
        picks.append(oh)
        sel = jnp.where(oh, ninf, sel)
    return scores, picks


def _route_kernel(lg_ref, bias_ref, dest_ref, gate_ref, cnt_ref, pad_ref, counts_ref, pads_ref, run_ref):
    phase = pl.program_id(0)
    i = pl.program_id(1)
    e, tm = lg_ref.shape
    scores, picks = _route_select(lg_ref[...], bias_ref[...])
    mask = jnp.zeros((e, tm), F32)
    for oh in picks:
        mask = jnp.where(oh, 1.0, mask)

    @pl.when((phase == 0) & (i == 0))
    def _():
        counts_ref[...] = jnp.zeros_like(counts_ref)

    @pl.when(phase == 0)
    def _():
        counts_ref[...] += mask.sum(axis=1, keepdims=True)

    @pl.when((phase == 1) & (i == 0))
    def _():
        counts = counts_ref[...]
        padded = jnp.ceil(counts * np.float32(1.0 / EXPERT_BM)) * np.float32(EXPERT_BM)
        r_i = lax.broadcasted_iota(jnp.int32, (e, e), 0)
        c_i = lax.broadcasted_iota(jnp.int32, (e, e), 1)
        before = (c_i < r_i).astype(F32)
        starts = jnp.dot(before, jnp.broadcast_to(padded, (e, LANES)), preferred_element_type=F32,
                         precision=lax.Precision.HIGHEST)
        pads_ref[...] = starts[:, 0:1]
        run_ref[...] = jnp.zeros_like(run_ref)
        cnt_ref[...] = jnp.broadcast_to(counts, cnt_ref.shape)
        pad_ref[...] = starts

    @pl.when(phase == 1)
    def _():
        r_i = lax.broadcasted_iota(jnp.int32, (tm, tm), 0)
        c_i = lax.broadcasted_iota(jnp.int32, (tm, tm), 1)
        earlier = (r_i < c_i).astype(BF16)
        slot = pads_ref[...] + run_ref[...] + _dot(mask.astype(BF16), earlier)
        run_ref[...] += mask.sum(axis=1, keepdims=True)
        k_i = lax.broadcasted_iota(jnp.int32, (TOP_K, tm), 0)
        dest = jnp.zeros((TOP_K, tm), F32)
        gate = jnp.zeros((TOP_K, tm), F32)
        total = jnp.zeros((1, tm), F32)
        for k, oh in enumerate(picks):
            d_k = jnp.where(oh, slot, 0.0).sum(axis=0, keepdims=True)
            g_k = jnp.where(oh, scores, 0.0).sum(axis=0, keepdims=True)
            total = total + g_k
            dest = jnp.where(k_i == k, d_k, dest)
            gate = jnp.where(k_i == k, g_k, gate)
        dest_ref[...] = dest.astype(jnp.int32)
        gate_ref[...] = gate / total * np.float32(ROUTED_SCALE)


def _route(lg_t, bias):
    e, t = lg_t.shape
    tm = MOE_TM
    n_t = t // tm
    tile = pl.BlockSpec((TOP_K, tm), lambda p, i: (0, i * p))
    meta = pl.BlockSpec((e, LANES), lambda p, i: (0, 0))
    return pl.pallas_call(
        _route_kernel,
        grid=(2, n_t),
        in_specs=[pl.BlockSpec((e, tm), lambda p, i: (0, i)), pl.BlockSpec((e, 1), lambda p, i: (0, 0))],
        out_specs=[tile, tile, meta, meta],
        out_shape=[jax.ShapeDtypeStruct((TOP_K, t), jnp.int32), jax.ShapeDtypeStruct((TOP_K, t), F32),
                   jax.ShapeDtypeStruct((e, LANES), F32), jax.ShapeDtypeStruct((e, LANES), F32)],
        scratch_shapes=[pltpu.VMEM((e, 1), F32), pltpu.VMEM((e, 1), F32), pltpu.VMEM((e, 1), F32)],
        compiler_params=_params("arbitrary", "arbitrary"),
        name="route",
    )(lg_t, bias)


def _row_copy(src, src_row, dst, dst_row, sem):
    return pltpu.make_async_copy(src.at[pl.ds(src_row, 1)], dst.at[pl.ds(dst_row, 1)], sem)


def _scatter_kernel(last_ref, nu_ref, dest_ref, x_ref, xs_out, zeros_ref, sem, zsem):
    i = pl.program_id(0)
    tm = x_ref.shape[0]
    n_blocks = xs_out.shape[0] // EXPERT_BM

    @pl.when(i == 0)
    def _():
        zeros_ref[...] = jnp.zeros_like(zeros_ref)

        def zero_copy(row):
            return pltpu.make_async_copy(zeros_ref, xs_out.at[pl.ds(pl.multiple_of(row, EXPERT_BM), EXPERT_BM)], zsem)

        def tail_row(b):
            return (nu_ref[0] + b) * EXPERT_BM

        def each(fn, n, act):
            def body(b, c):
                row = fn(b)

                @pl.when(row >= 0)
                def _():
                    act(zero_copy(row))
                return c
            lax.fori_loop(0, n, body, 0)

        start = lambda fn, n: each(fn, n, lambda c: c.start())
        wait = lambda fn, n: each(fn, n, lambda c: c.wait())

        start(lambda e: last_ref[e], N_EXPERTS)
        wait(lambda e: last_ref[e], N_EXPERTS)
        start(tail_row, n_blocks - nu_ref[0])
        wait(tail_row, n_blocks - nu_ref[0])

    def issue(t, c):
        for k in range(TOP_K):
            _row_copy(x_ref, t, xs_out, dest_ref[k * tm + t], sem).start(priority=k % 2)
        return c

    def drain(t, c):
        for k in range(TOP_K):
            _row_copy(x_ref, t, xs_out, dest_ref[k * tm + t], sem).wait()
        return c

    lax.fori_loop(0, tm, issue, 0)
    lax.fori_loop(0, tm, drain, 0)


def _scatter(last_row, n_used, dest_flat, x1p, n_slots):
    t, w = x1p.shape
    tm = MOE_TM
    grid_spec = pltpu.PrefetchScalarGridSpec(
        num_scalar_prefetch=2,
        grid=(t // tm,),
        in_specs=[pl.BlockSpec((TOP_K * tm,), lambda i, lr, nu: (i,), memory_space=pltpu.SMEM),
                  pl.BlockSpec((tm, w), lambda i, lr, nu: (i, 0))],
        out_specs=pl.BlockSpec(memory_space=pl.ANY),
        scratch_shapes=[pltpu.VMEM((EXPERT_BM, w), x1p.dtype), pltpu.SemaphoreType.DMA(()),
                        pltpu.SemaphoreType.DMA(())],
    )
    return pl.pallas_call(
        _scatter_kernel,
        grid_spec=grid_spec,
        out_shape=jax.ShapeDtypeStruct((n_slots, w), x1p.dtype),
        compiler_params=_params("arbitrary"),
        name="scatter",
    )(last_row, n_used, dest_flat, x1p)


def _expert_kernel(be_ref, nu_ref, slot_ref, nxt_ref, x_ref, wg_hbm, wu_hbm, wd_hbm, y_ref,
                   wgf, wuf, wdf, wgb, wub, wdb, sem, *, layer):
    i = pl.program_id(0)
    active = i < nu_ref[0]
    first = (i == 0) | (be_ref[i] != be_ref[jnp.maximum(i - 1, 0)])

    def weight_copies(e, s):
        return [pltpu.make_async_copy(src.at[layer, e], dst.at[s], sem.at[s])
                for src, dst in ((wg_hbm, wgf), (wu_hbm, wuf), (wd_hbm, wdf))]

    @pl.when(active & (i == 0))
    def _():
        for c in weight_copies(be_ref[0], slot_ref[0]):
            c.start()

    @pl.when(active & first)
    def _():
        s = slot_ref[i]
        for c in weight_copies(be_ref[i], s):
            c.wait()
        wgb[...] = wgf[s].astype(BF16)
        wub[...] = wuf[s].astype(BF16)
        wdb[...] = wdf[s].astype(BF16)

        @pl.when(nxt_ref[i] >= 0)
        def _():
            for c in weight_copies(nxt_ref[i], 1 - s):
                c.start()

    @pl.when(active)
    def _():
        x = _unpack_bf16_pairs(x_ref[...])
        g = _dot(x, wgb[...])
        h = (g * jax.nn.sigmoid(g)) * _dot(x, wub[...])
        y_ref[...] = _pack_bf16_pairs(_dot(h.astype(BF16), wdb[...]))

    @pl.when(jnp.logical_not(active))
    def _():
        y_ref[...] = jnp.zeros_like(y_ref)


def _experts(block_e, n_used, slot, nxt, xs, layer, wg, wu, wd):
    n_slots = xs.shape[0]
    n_blocks = n_slots // EXPERT_BM
    ff = wg.shape[3]
    any_spec = pl.BlockSpec(memory_space=pl.ANY)
    grid_spec = pltpu.PrefetchScalarGridSpec(
        num_scalar_prefetch=4,
        grid=(n_blocks,),
        in_specs=[pl.BlockSpec((EXPERT_BM, D_MODEL // 2), lambda i, be, nu, sl, nx: (jnp.minimum(i, nu[0] - 1), 0)),
                  any_spec, any_spec, any_spec],
        out_specs=pl.BlockSpec((EXPERT_BM, D_MODEL // 2), lambda i, be, nu, sl, nx: (i, 0)),
        scratch_shapes=[pltpu.VMEM((2, D_MODEL, ff), F32), pltpu.VMEM((2, D_MODEL, ff), F32),
                        pltpu.VMEM((2, ff, D_MODEL), F32),
                        pltpu.VMEM((D_MODEL, ff), BF16), pltpu.VMEM((D_MODEL, ff), BF16),
                        pltpu.VMEM((ff, D_MODEL), BF16), pltpu.SemaphoreType.DMA((2,))],
    )
    return pl.pallas_call(
        functools.partial(_expert_kernel, layer=layer),
        grid_spec=grid_spec,
        out_shape=jax.ShapeDtypeStruct((n_slots, D_MODEL // 2), jnp.uint32),
        compiler_params=_params("arbitrary"),
        name="experts",
    )(block_e, n_used, slot, nxt, xs, wg, wu, wd)


def _ffn_ln_kernel(dest_ref, xp_ref, x_ref, gate_ref, y_hbm, wg_ref, wu_ref, wd_ref, g_ref, b_ref, o_ref, buf, sem):
    tm = x_ref.shape[0]

    def issue(t, c):
        for k in range(TOP_K):
            _row_copy(y_hbm, dest_ref[k * tm + t], buf.at[k], t, sem).start(priority=k % 2)
        return c

    def drain(t, c):
        for k in range(TOP_K):
            _row_copy(y_hbm, dest_ref[k * tm + t], buf.at[k], t, sem).wait()
        return c

    lax.fori_loop(0, tm, issue, 0)
    xb = _unpack_bf16_pairs(xp_ref[...])
    gte = _dot(xb, wg_ref[...])
    h = (gte * jax.nn.sigmoid(gte)) * _dot(xb, wu_ref[...])
    shared = _dot(h.astype(BF16), wd_ref[...])
    lax.fori_loop(0, tm, drain, 0)
    lo_half = hi_half = None
    for k in range(TOP_K):
        w = buf[k]
        g = gate_ref[:, k:k + 1]
        hi = lax.bitcast_convert_type(w & jnp.uint32(0xFFFF0000), F32) * g
        lo = lax.bitcast_convert_type(w << 16, F32) * g
        hi_half = hi if hi_half is None else hi_half + hi
        lo_half = lo if lo_half is None else lo_half + lo
    routed = jnp.concatenate([hi_half, lo_half], axis=1)
    z = np.float32(ALPHA) * x_ref[...] + (routed + shared)
    o_ref[...] = _layer_norm(z, g_ref[...], b_ref[...])


def _ffn_ln(dest_flat, x1p, x1, gate_t, y, wg, wu, wd, g, b):
    t = x1.shape[0]
    tm = MOE_TM
    row = lambda a: pl.BlockSpec((tm, a.shape[1]), lambda i: (i, 0))
    full = lambda a: pl.BlockSpec(a.shape, lambda i: (0,) * a.ndim)
    return pl.pallas_call(
        _ffn_ln_kernel,
        grid=(t // tm,),
        in_specs=[pl.BlockSpec((TOP_K * tm,), lambda i: (i,), memory_space=pltpu.SMEM),
                  row(x1p), row(x1), row(gate_t), pl.BlockSpec(memory_space=pl.ANY),
                  full(wg), full(wu), full(wd), full(g), full(b)],
        out_specs=row(x1),
        out_shape=jax.ShapeDtypeStruct((t, D_MODEL), F32),
        scratch_shapes=[pltpu.VMEM((TOP_K, tm, D_MODEL // 2), jnp.uint32), pltpu.SemaphoreType.DMA(())],
        compiler_params=_params("arbitrary"),
        name="ffn_ln",
    )(dest_flat, x1p, x1, gate_t, y, wg, wu, wd, g, b)


def _ple_kernel(x_ref, wp_ref, wg_ref, *rest, n_prompt_tiles, split, n_p):
    p_refs, o_refs = rest[:n_p], rest[n_p:]
    i = pl.program_id(0)
    x = x_ref[...]
    gate = jax.nn.sigmoid(_dot(x.astype(BF16), wg_ref[...]))
    emb = _dot(_row_value(p_refs, i < n_prompt_tiles).astype(BF16), wp_ref[...])
    out = x + emb * gate
    if not split:
        o_refs[0][...] = out
    else:

        @pl.when(i < n_prompt_tiles)
        def _():
            o_refs[0][...] = out

        @pl.when(i >= n_prompt_tiles)
        def _():
            o_refs[1][...] = out


def _ple(x2, p, w_ple, w_gate, n_prompt, split):
    t = x2.shape[0]
    tm = _tile(np.gcd(n_prompt, t - n_prompt), 256)
    n_pt = n_prompt // tm
    row = lambda a: pl.BlockSpec((tm, a.shape[1]), lambda i: (i, 0))
    full = lambda a: pl.BlockSpec(a.shape, lambda i: (0,) * a.ndim)
    p_specs, p_args = _row_specs(p, tm, n_pt)
    if not split:
        out_specs = [row(x2)]
        out_shape = [jax.ShapeDtypeStruct((t, D_MODEL), F32)]
    else:
        out_specs = [pl.BlockSpec((tm, D_MODEL), lambda i: (jnp.minimum(i, n_pt - 1), 0)),
                     pl.BlockSpec((tm, D_MODEL), lambda i: (jnp.maximum(i - n_pt, 0), 0))]
        out_shape = [jax.ShapeDtypeStruct((n_prompt, D_MODEL), F32),
                     jax.ShapeDtypeStruct((t - n_prompt, D_MODEL), F32)]
    return pl.pallas_call(
        functools.partial(_ple_kernel, n_prompt_tiles=n_pt, split=split, n_p=len(p_args)),
        grid=(t // tm,),
        in_specs=[row(x2), full(w_ple), full(w_gate)] + p_specs,
        out_specs=out_specs, out_shape=out_shape,
        compiler_params=_params("arbitrary"),
        name="ple",
    )(x2, w_ple, w_gate, *p_args)


def _rel_bias(table, qpos, kpos):
    idx = np.clip(qpos[:, None] - kpos[None, :], -REL_CLIP, REL_CLIP) + REL_CLIP
    return table[:, idx]


def _band_prompt_bias(table):
    tq, width = BAND_TQ, BAND_NWB * BAND_TQ
    span = width + tq - 1
    rel = (BAND_NWB - 1) * tq + tq - 1 - np.arange(span)
    strip = table[:, np.clip(rel, -REL_CLIP, REL_CLIP) + REL_CLIP]
    u = jnp.concatenate([strip[:, tq - 1:], strip[:, :1], strip[:, :tq - 1]], axis=1)
    flat = jnp.tile(u, (1, tq))[:, :tq * span]
    return flat.reshape(table.shape[0], tq, span)[:, :, :width]


def _band_mask(qpos, kpos):
    dc = qpos // CHUNK - kpos // CHUNK
    return (kpos >= 0) & (dc >= 0) & (dc <= B_PREV_CHUNKS)


def _post_block(x, lhs, w_out, p, ln1_g, ln1_b, ln2_g, ln2_b, w_router, b_router,
                layer, w_gate, w_up, w_down, ws_gate, ws_up, ws_down, w_ple, w_ple_gate, name, n_prompt, split):
    row = lambda a: a.reshape(1, -1)
    x1, x1p, lg_t = _outproj_ln(lhs, w_out, x, row(ln1_g), row(ln1_b), w_router.T, name)
    t = x1.shape[0]
    dest, gate, cnt, pad = _route(lg_t, b_router.astype(F32).reshape(N_EXPERTS, 1))

    n_blocks = -(-(t * TOP_K + N_EXPERTS * (EXPERT_BM - 1)) // EXPERT_BM)
    counts = cnt[:, 0].astype(jnp.int32)
    pad_end = pad[:, 0].astype(jnp.int32) + (counts + EXPERT_BM - 1) // EXPERT_BM * EXPERT_BM
    block_e = jnp.minimum((pad_end[:, None] <= jnp.arange(n_blocks)[None, :] * EXPERT_BM).sum(axis=0),
                          N_EXPERTS - 1).astype(jnp.int32)
    n_used = (pad_end[-1:] // EXPERT_BM).astype(jnp.int32)
    nonempty = counts > 0
    e_ids = jnp.arange(N_EXPERTS)
    later_ids = jnp.where(nonempty[None, :] & (e_ids[None, :] > e_ids[:, None]), e_ids[None, :], N_EXPERTS).min(axis=1)
    nxt_of = jnp.where(later_ids < N_EXPERTS, later_ids, -1)
    slot_of = (jnp.cumsum(nonempty) - 1) % 2
    in_block = (block_e[:, None] == e_ids[None, :]).astype(jnp.int32)
    nxt = (in_block * nxt_of[None, :]).sum(axis=1).astype(jnp.int32)
    slot = (in_block * slot_of[None, :]).sum(axis=1).astype(jnp.int32)
    dest_flat = dest.reshape(TOP_K, t // MOE_TM, MOE_TM).transpose(1, 0, 2).reshape(-1)

    last_row = jnp.where(counts > 0, pad_end - EXPERT_BM, -1).astype(jnp.int32)
    xs = _scatter(last_row, n_used, dest_flat, x1p, n_blocks * EXPERT_BM)
    y = _experts(block_e, n_used, slot, nxt, xs, layer, w_gate, w_up, w_down)
    x2 = _ffn_ln(dest_flat, x1p, x1, gate.T, y, ws_gate, ws_up, ws_down, row(ln2_g), row(ln2_b))
    return _ple(x2, p, w_ple, w_ple_gate, n_prompt, split)


def kernel(x_prompt, x_sample, cache_b_k, cache_b_v, cache_c_k, cache_c_v, p_prompt, p_sample, w_in_ab, w_out_ab, sgu_w, sgu_b, sgu_ln_g, sgu_ln_b, rel_bias_tab, w_in_c, w_out_c, ln_mix_g, ln_mix_b, ln_ffn_g, ln_ffn_b, w_router, b_router, w_gate, w_up, w_down, ws_gate, ws_up, ws_down, w_ple, w_ple_gate):
    n_p = BATCH * SEQ
    n_s = DEC_BATCH * DEC_SEQ
    x = (x_prompt.reshape(n_p, D_MODEL), x_sample.reshape(n_s, D_MODEL))
    p_rows = lambda i: (p_prompt[i].reshape(n_p, PLE_DIM), p_sample[i].reshape(n_s, PLE_DIM))
    bf = lambda a: a.astype(BF16)
    outs = {k: [] for k in ("bk_p", "bv_p", "bk_s", "bv_s", "av_s", "ck_p", "cv_p", "ck_s", "cv_s")}
    kv_outs = ((F32, True), (BF16, False))

    for i in range(DEPTH):
        j = i // 2
        if i % 2 == 0:
            plain = lambda d: ((d, False),)
            hu, hv, q, k_p, k_s, kb16, v_p, v_s, vb16 = _proj(
                x, bf(w_in_ab[j]), 0,
                [(0, A_WIDTH, plain(F32)), (A_WIDTH, A_WIDTH, plain(F32)), (2 * A_WIDTH, B_WIDTH, plain(BF16)),
                 (2 * A_WIDTH + B_WIDTH, B_WIDTH, kv_outs), (2 * A_WIDTH + 2 * B_WIDTH, B_WIDTH, kv_outs)],
                "proj_ab", n_p)

            bs_t = sgu_b[j].T
            a_p, = _sgu(hu, hv, sgu_w[j], bs_t, sgu_ln_g[j], sgu_ln_b[j], A_CHUNK, 0, n_p // A_CHUNK, False)
            a_s, va_s = _sgu(hu, hv, sgu_w[j][:, :DEC_SEQ, :DEC_SEQ], bs_t[:DEC_SEQ], sgu_ln_g[j], sgu_ln_b[j],
                             DEC_SEQ, n_p, DEC_BATCH, True)

            b_p = _band_prompt(q, kb16, vb16, _band_prompt_bias(rel_bias_tab[j]))
            n_cache = cache_b_k.shape[2]
            qs = PAST_LEN + np.arange(DEC_SEQ)
            kc = PAST_LEN - n_cache + np.arange(n_cache)
            b_s = _band_sample(
                q, kb16, vb16, cache_b_k.reshape(N_EVEN, DEC_BATCH, n_cache * B_HEADS, B_DIM),
                cache_b_v.reshape(N_EVEN, DEC_BATCH, n_cache * B_HEADS, B_DIM), j,
                _rel_bias(rel_bias_tab[j], qs, kc), _rel_bias(rel_bias_tab[j], qs, qs),
                jnp.asarray(_band_mask(qs[:, None], kc[None, :]), F32),
                jnp.asarray(_band_mask(qs[:, None], qs[None, :]), F32))

            keep = min(B_WINDOW, SEQ)
            outs["bk_p"].append(k_p.reshape(BATCH, SEQ, B_HEADS, B_DIM)[:, SEQ - keep:])
            outs["bv_p"].append(v_p.reshape(BATCH, SEQ, B_HEADS, B_DIM)[:, SEQ - keep:])
            outs["bk_s"].append(k_s.reshape(DEC_BATCH, DEC_SEQ, B_HEADS, B_DIM))
            outs["bv_s"].append(v_s.reshape(DEC_BATCH, DEC_SEQ, B_HEADS, B_DIM))
            outs["av_s"].append(va_s.reshape(DEC_BATCH, DEC_SEQ, A_HEADS, A_DIM))
            lhs, w_out, name = [(a_p, a_s), (b_p, b_s)], bf(w_out_ab[j]), "outproj_ab"
        else:
            w_in = bf(w_in_c[j])
            q, k_p, k_s, kb16 = _proj(x, w_in, 0, [(0, C_WIDTH, ((BF16, False),)), (C_WIDTH, C_WIDTH, kv_outs)],
                                      "proj_cqk", n_p)
            v_p, v_s, vb16 = _proj(x, w_in, 2, [(0, C_WIDTH, kv_outs)], "proj_cv", n_p)
            o_p = _sb_prompt(q, kb16, vb16)
            rows = lambda c: c.reshape(N_ODD, DEC_BATCH, PAST_LEN * C_HEADS, C_DIM)
            o_s = _sb_sample(q, kb16, vb16, rows(cache_c_k), rows(cache_c_v), j)
            outs["ck_p"].append(k_p.reshape(BATCH, SEQ, C_HEADS, C_DIM))
            outs["cv_p"].append(v_p.reshape(BATCH, SEQ, C_HEADS, C_DIM))
            outs["ck_s"].append(k_s.reshape(DEC_BATCH, DEC_SEQ, C_HEADS, C_DIM))
            outs["cv_s"].append(v_s.reshape(DEC_BATCH, DEC_SEQ, C_HEADS, C_DIM))
            lhs, w_out, name = [(o_p, o_s)], bf(w_out_c[j]), "outproj_c"

        x = _post_block(x, lhs, w_out, p_rows(i), ln_mix_g[i], ln_mix_b[i], ln_ffn_g[i], ln_ffn_b[i],
                        w_router[i], b_router[i], i, w_gate, w_up, w_down,
                        bf(ws_gate[i]), bf(ws_up[i]), bf(ws_down[i]), bf(w_ple[i]), bf(w_ple_gate[i]), name,
                        n_p, i == DEPTH - 1)
        if i < DEPTH - 1:
            x, = x

    y_p, y_s = x
    st = lambda key: jnp.stack(outs[key])
    return (y_p.reshape(BATCH, SEQ, D_MODEL), y_s.reshape(DEC_BATCH, DEC_SEQ, D_MODEL),
            st("bk_p"), st("bv_p"), st("bk_s"), st("bv_s"), st("av_s"),
            st("ck_p"), st("cv_p"), st("ck_s"), st("cv_s"))
```

```python
import functools

import numpy as np
import jax
import jax.numpy as jnp
from jax import lax
from jax.experimental import pallas as pl
from jax.experimental.pallas import tpu as pltpu

D_MODEL = 2048
BATCH = 2
SEQ = 4096
DEPTH = 2
DEC_BATCH = 32
DEC_SEQ = 16
PAST_LEN = 2048

CHUNK = 64
N_EVEN = (DEPTH + 1) // 2
N_ODD = DEPTH // 2
A_CHUNK = 128
A_HEADS = 8
A_DIM = 128
A_WIDTH = A_HEADS * A_DIM
B_HEADS = 8
B_DIM = 128
B_WIDTH = B_HEADS * B_DIM
B_PREV_CHUNKS = 8
B_WINDOW = B_PREV_CHUNKS * CHUNK
REL_CLIP = 128
C_HEADS = 16
C_DIM = 128
C_WIDTH = C_HEADS * C_DIM
N_EXPERTS = 64
N_GROUPS = 8
E_PER_GROUP = N_EXPERTS // N_GROUPS
TOPK_GROUPS = 4
TOP_K = 8
EXPERT_FF = 512
SHARED_FF = 512
ROUTED_SCALE = 2.5
PLE_DIM = 256
LN_EPS = 1e-5
ALPHA = (2 * DEPTH) ** 0.25
NEG = -1e9

LANES = 128
VMEM_LIMIT_BYTES = 56 * 1024 * 1024

BF16 = jnp.bfloat16
F32 = jnp.float32


def _params(*sem):
    return pltpu.CompilerParams(dimension_semantics=sem, vmem_limit_bytes=VMEM_LIMIT_BYTES)


def _tile(n, pref):
    if n <= pref:
        return n
    for t in range(pref, 7, -1):
        if n % t == 0 and t % 8 == 0:
            return t
    return n


def _dot(a, b):
    return jnp.dot(a, b, preferred_element_type=F32)


def _dot_nt(a, b):
    return lax.dot_general(a, b, (((1,), (1,)), ((), ())), preferred_element_type=F32)


def _layer_norm(z, g, b):
    mu = jnp.mean(z, axis=-1, keepdims=True)
    zc = z - mu
    var = jnp.mean(zc * zc, axis=-1, keepdims=True)
    return zc * lax.rsqrt(var + LN_EPS) * g + b


def _gelu(x):
    return x * (lax.erf(x * np.float32(1.0 / np.sqrt(2.0))) + 1.0) * 0.5


def _softplus(z):
    return jnp.maximum(z, 0.0) + jnp.log(1.0 + jnp.exp(-jnp.abs(z)))


def _row_specs(a, tm, n_prompt_tiles):
    if not isinstance(a, tuple):
        return [pl.BlockSpec((tm, a.shape[1]), lambda i: (i, 0))], [a]
    a_p, a_s = a
    assert a_p.shape[0] == n_prompt_tiles * tm and a_s.shape[0] % tm == 0
    return ([pl.BlockSpec((tm, a_p.shape[1]), lambda i: (jnp.minimum(i, n_prompt_tiles - 1), 0)),
             pl.BlockSpec((tm, a_s.shape[1]), lambda i: (jnp.maximum(i - n_prompt_tiles, 0), 0))], [a_p, a_s])


def _row_value(refs, is_prompt, rows=slice(None)):
    if len(refs) == 1:
        return refs[0][rows, :]
    return jnp.where(is_prompt, refs[0][rows, :], refs[1][rows, :])


def _proj_kernel(*refs, segs, n_prompt_tiles, n_x):
    x_refs, w_ref, o_refs = refs[:n_x], refs[n_x], refs[n_x + 1:]
    i = pl.program_id(0)
    xb = _row_value(x_refs, i < n_prompt_tiles).astype(BF16)
    refs = iter(o_refs)
    for col0, ncols, outs in segs:
        acc = _dot(xb, w_ref[:, col0:col0 + ncols])
        for _, is_split in outs:
            if is_split:
                o_p, o_s = next(refs), next(refs)

                @pl.when(i < n_prompt_tiles)
                def _(o_p=o_p, acc=acc):
                    o_p[...] = acc.astype(o_p.dtype)

                @pl.when(i >= n_prompt_tiles)
                def _(o_s=o_s, acc=acc):
                    o_s[...] = acc.astype(o_s.dtype)
            else:
                o = next(refs)
                o[...] = acc.astype(o.dtype)


def _proj(x, w, col_block, segs, name, n_prompt):
    m = sum(a.shape[0] for a in x) if isinstance(x, tuple) else x.shape[0]
    k = (x[0] if isinstance(x, tuple) else x).shape[1]
    width = sum(n for _, n, _ in segs)
    tm = _tile(np.gcd(n_prompt, m - n_prompt), 256)
    n_pt = n_prompt // tm
    x_specs, x_args = _row_specs(x, tm, n_pt)
    out_specs, out_shape = [], []
    for _, ncols, outs in segs:
        for dtype, is_split in outs:
            if is_split:
                out_specs += [pl.BlockSpec((tm, ncols), lambda i: (jnp.minimum(i, n_pt - 1), 0)),
                              pl.BlockSpec((tm, ncols), lambda i: (jnp.maximum(i - n_pt, 0), 0))]
                out_shape += [jax.ShapeDtypeStruct((n_prompt, ncols), dtype),
                              jax.ShapeDtypeStruct((m - n_prompt, ncols), dtype)]
            else:
                out_specs.append(pl.BlockSpec((tm, ncols), lambda i: (i, 0)))
                out_shape.append(jax.ShapeDtypeStruct((m, ncols), dtype))
    return pl.pallas_call(
        functools.partial(_proj_kernel, segs=tuple(segs), n_prompt_tiles=n_pt, n_x=len(x_args)),
        grid=(m // tm,),
        in_specs=x_specs + [pl.BlockSpec((k, width), lambda i: (0, col_block), pipeline_mode=pl.Buffered(1))],
        out_specs=out_specs, out_shape=out_shape,
        compiler_params=_params("arbitrary"),
        name=name,
    )(*x_args, w)


def _sgu_kernel(hu_ref, hv_ref, w_ref, bs_ref, g_ref, b_ref, a_ref, *v_out, emit_v):
    rows = hu_ref.shape[0]
    r_i = lax.broadcasted_iota(jnp.int32, (rows, rows), 0)
    c_i = lax.broadcasted_iota(jnp.int32, (rows, rows), 1)
    causal = c_i <= r_i
    for h in range(A_HEADS):
        sl = slice(h * A_DIM, (h + 1) * A_DIM)
        u = _gelu(hu_ref[:, sl])
        v = _layer_norm(_gelu(hv_ref[:, sl]), g_ref[h:h + 1, :], b_ref[h:h + 1, :])
        w = jnp.where(causal, w_ref[h], 0.0).astype(BF16)
        mix = _dot(w, v.astype(BF16)) + bs_ref[:, h:h + 1]
        a_ref[:, sl] = (u * mix).astype(a_ref.dtype)
        if emit_v:
            v_out[0][:, sl] = v


def _sgu(hu, hv, w_s, bs_t, g, b, rows, row0, n_chunks, emit_v):
    off = row0 // rows
    in_spec = pl.BlockSpec((rows, A_WIDTH), lambda c: (c + off, 0))
    out_spec = pl.BlockSpec((rows, A_WIDTH), lambda c: (c, 0))
    full = lambda a: pl.BlockSpec(a.shape, lambda c: (0,) * a.ndim)
    out_specs = [out_spec]
    out_shape = [jax.ShapeDtypeStruct((n_chunks * rows, A_WIDTH), BF16)]
    if emit_v:
        out_specs.append(out_spec)
        out_shape.append(jax.ShapeDtypeStruct((n_chunks * rows, A_WIDTH), F32))
    return pl.pallas_call(
        functools.partial(_sgu_kernel, emit_v=emit_v),
        grid=(n_chunks,),
        in_specs=[in_spec, in_spec, full(w_s), full(bs_t), full(g), full(b)],
        out_specs=out_specs, out_shape=out_shape,
        compiler_params=_params("arbitrary"),
        name="sgu_sample" if emit_v else "sgu_prompt",
    )(hu, hv, w_s, bs_t, g, b)


BAND_TQ = 2 * CHUNK
BAND_NWB = B_WINDOW // BAND_TQ + 1


BAND_HP = 8


def _band_prompt_kernel(q_ref, k_ref, v_ref, bias_ref, o_ref):
    t = pl.program_id(2)
    tq = BAND_TQ
    shift = int(np.log2(CHUNK))
    q_chunk = lax.shift_right_arithmetic(t * tq + lax.broadcasted_iota(jnp.int32, (tq, 1), 0), shift)
    lane = lax.broadcasted_iota(jnp.int32, (1, tq), 1)
    starts, valids = [], []
    for j in range(BAND_NWB):
        kb = t - (BAND_NWB - 1) + j
        starts.append(pl.multiple_of(jnp.maximum(kb, 0) * tq, tq))
        kpos = kb * tq + lane
        dc = q_chunk - lax.shift_right_arithmetic(kpos, shift)
        valids.append((kpos >= 0) & (dc >= 0) & (dc <= B_PREV_CHUNKS))
    for h in range(BAND_HP):
        sl = slice(h * B_DIM, (h + 1) * B_DIM)
        q = q_ref[:, sl]
        s_blocks = []
        for j in range(BAND_NWB):
            s = (_dot_nt(q, k_ref[pl.ds(starts[j], tq), sl]) * np.float32(B_DIM ** -0.5)
                 + bias_ref[h, :, j * tq:(j + 1) * tq])
            s_blocks.append(jnp.where(valids[j], s, NEG))
        m = s_blocks[0].max(axis=-1, keepdims=True)
        for s in s_blocks[1:]:
            m = jnp.maximum(m, s.max(axis=-1, keepdims=True))
        acc = jnp.zeros((tq, B_DIM), F32)
        den = jnp.zeros((tq, 1), F32)
        for j, s in enumerate(s_blocks):
            p = jnp.exp(s - m)
            den = den + p.sum(axis=-1, keepdims=True)
            acc = acc + _dot(p.astype(BF16), v_ref[pl.ds(starts[j], tq), sl])
        o_ref[:, sl] = (acc / den).astype(o_ref.dtype)


def _band_prompt(q, k, v, bias):
    assert B_HEADS % BAND_HP == 0
    t_all = BATCH * SEQ
    n_t = SEQ // BAND_TQ
    q_spec = pl.BlockSpec((BAND_TQ, BAND_HP * B_DIM), lambda b, h, t: (b * n_t + t, h))
    kv_spec = pl.BlockSpec((SEQ, BAND_HP * B_DIM), lambda b, h, t: (b, h))
    return pl.pallas_call(
        _band_prompt_kernel,
        grid=(BATCH, B_HEADS // BAND_HP, n_t),
        in_specs=[q_spec, kv_spec, kv_spec,
                  pl.BlockSpec((BAND_HP, BAND_TQ, BAND_NWB * BAND_TQ), lambda b, h, t: (h, 0, 0))],
        out_specs=q_spec,
        out_shape=jax.ShapeDtypeStruct((t_all, B_WIDTH), BF16),
        compiler_params=_params("arbitrary", "arbitrary", "arbitrary"),
        name="band_prompt",
    )(q, k, v, bias)


def _band_sample_kernel(q_ref, kn_ref, vn_ref, kc_ref, vc_ref, bc_ref, bn_ref, mc_ref, mn_ref, o_ref):
    scale = np.float32(B_DIM ** -0.5)
    n_cache = kc_ref.shape[0] // B_HEADS
    for h in range(B_HEADS):
        sl = slice(h * B_DIM, (h + 1) * B_DIM)
        q = q_ref[:, sl]
        s_c = _dot_nt(q, kc_ref[pl.ds(h, n_cache, stride=B_HEADS), :].astype(BF16)) * scale + bc_ref[h]
        s_n = _dot_nt(q, kn_ref[:, sl]) * scale + bn_ref[h]
        s_c = jnp.where(mc_ref[...] > 0, s_c, NEG)
        s_n = jnp.where(mn_ref[...] > 0, s_n, NEG)
        m = jnp.maximum(s_c.max(axis=-1, keepdims=True), s_n.max(axis=-1, keepdims=True))
        p_c = jnp.exp(s_c - m)
        p_n = jnp.exp(s_n - m)
        den = p_c.sum(axis=-1, keepdims=True) + p_n.sum(axis=-1, keepdims=True)
        acc = (_dot(p_c.astype(BF16), vc_ref[pl.ds(h, n_cache, stride=B_HEADS), :].astype(BF16))
               + _dot(p_n.astype(BF16), vn_ref[:, sl]))
        o_ref[:, sl] = (acc / den).astype(o_ref.dtype)


def _band_sample(q, k, v, cache_k, cache_v, layer, bias_c, bias_n, mask_c, mask_n):
    n = DEC_SEQ
    off = BATCH * SEQ // n
    n_cache = cache_k.shape[2] // B_HEADS
    row_spec = pl.BlockSpec((n, B_WIDTH), lambda b: (b + off, 0))
    cache_spec = pl.BlockSpec((None, None, n_cache * B_HEADS, B_DIM), lambda b: (layer, b, 0, 0))
    full = lambda a: pl.BlockSpec(a.shape, lambda b: (0,) * a.ndim)
    return pl.pallas_call(
        _band_sample_kernel,
        grid=(DEC_BATCH,),
        in_specs=[row_spec, row_spec, row_spec, cache_spec, cache_spec,
                  full(bias_c), full(bias_n), full(mask_c), full(mask_n)],
        out_specs=pl.BlockSpec((n, B_WIDTH), lambda b: (b, 0)),
        out_shape=jax.ShapeDtypeStruct((DEC_BATCH * n, B_WIDTH), BF16),
        compiler_params=_params("arbitrary"),
        name="band_sample",
    )(q, k, v, cache_k, cache_v, bias_c, bias_n, mask_c, mask_n)


SB_TQ = 256
SB_TK = 256
SB_SAMPLE_TK = 512
SB_HP = 4


def _sb_weights(z, carry, upper):
    sp = _softplus(z)
    hi = sp.astype(BF16)
    lo = (sp - hi.astype(F32)).astype(BF16)
    later = _dot(hi, upper) + _dot(lo, upper)
    w = jnp.exp((z - sp) - (carry + later))
    return w, carry + later[:, 0:1] + sp[:, 0:1]


def _upper(n):
    r_i = lax.broadcasted_iota(jnp.int32, (n, n), 0)
    c_i = lax.broadcasted_iota(jnp.int32, (n, n), 1)
    return (r_i > c_i).astype(BF16)


SB_ROWS = 32


def _sb_prompt_kernel(q_ref, k_ref, v_ref, o_ref, z_ref, ls_ref, hl_ref, lat_ref, w_ref, carry_ref, acc_ref):
    i = pl.program_id(2)
    tq, tk = SB_TQ, SB_TK
    upper2 = jnp.concatenate([_upper(tk)] * 2, axis=0)
    scale = np.float32(C_DIM ** -0.5)
    chunks = [slice(r, r + SB_ROWS) for r in range(0, tq, SB_ROWS)]
    r_i = lax.broadcasted_iota(jnp.int32, (SB_ROWS, tk), 0)
    c_i = lax.broadcasted_iota(jnp.int32, (SB_ROWS, tk), 1)

    def block(j, diagonal):
        start = pl.multiple_of(j * tk, tk)
        for h in range(SB_HP):
            sl = slice(h * C_DIM, (h + 1) * C_DIM)
            z_ref[h] = _dot_nt(q_ref[:, sl], k_ref[pl.ds(start, tk), sl])
        for h in range(SB_HP):
            for ci, rows in enumerate(chunks):
                z = z_ref[h, rows, :] * scale
                if diagonal:
                    z = jnp.where(c_i < r_i + ci * SB_ROWS, z, NEG)
                sp = _softplus(z)
                hi = sp.astype(BF16)
                ls_ref[h, rows, :] = z - sp
                hl_ref[h, rows, 0:tk] = hi
                hl_ref[h, rows, tk:2 * tk] = (sp - hi.astype(F32)).astype(BF16)
        for h in range(SB_HP):
            lat_ref[h] = _dot(hl_ref[h], upper2)
        for h in range(SB_HP):
            for rows in chunks:
                later = lat_ref[h, rows, :]
                carry = carry_ref[h, rows, :]
                w_ref[h, rows, :] = jnp.exp(ls_ref[h, rows, :] - (carry + later)).astype(BF16)
                sp0 = hl_ref[h, rows, 0:1].astype(F32) + hl_ref[h, rows, tk:tk + 1].astype(F32)
                carry_ref[h, rows, :] = carry + later[:, 0:1] + sp0
        for h in range(SB_HP):
            sl = slice(h * C_DIM, (h + 1) * C_DIM)
            acc_ref[h] += _dot(w_ref[h], v_ref[pl.ds(start, tk), sl])

    carry_ref[...] = jnp.zeros_like(carry_ref)
    acc_ref[...] = jnp.zeros_like(acc_ref)
    block(i, True)

    def body(jj, c):
        block(i - 1 - jj, False)
        return c

    lax.fori_loop(0, i, body, 0)
    for h in range(SB_HP):
        o_ref[:, h * C_DIM:(h + 1) * C_DIM] = acc_ref[h].astype(o_ref.dtype)


def _sb_prompt(q, k, v):
    assert SB_TQ == SB_TK and C_HEADS % SB_HP == 0
    t_all = BATCH * SEQ
    n_q = SEQ // SB_TQ
    q_spec = pl.BlockSpec((SB_TQ, SB_HP * C_DIM), lambda b, h, i: (b * n_q + i, h))
    kv_spec = pl.BlockSpec((SEQ, SB_HP * C_DIM), lambda b, h, i: (b, h))
    return pl.pallas_call(
        _sb_prompt_kernel,
        grid=(BATCH, C_HEADS // SB_HP, n_q),
        in_specs=[q_spec, kv_spec, kv_spec],
        out_specs=q_spec,
        out_shape=jax.ShapeDtypeStruct((t_all, C_WIDTH), BF16),
        scratch_shapes=[pltpu.VMEM((SB_HP, SB_TQ, SB_TK), F32), pltpu.VMEM((SB_HP, SB_TQ, SB_TK), F32),
                        pltpu.VMEM((SB_HP, SB_TQ, 2 * SB_TK), BF16), pltpu.VMEM((SB_HP, SB_TQ, SB_TK), F32),
                        pltpu.VMEM((SB_HP, SB_TQ, SB_TK), BF16), pltpu.VMEM((SB_HP, SB_TQ, 1), F32),
                        pltpu.VMEM((SB_HP, SB_TQ, C_DIM), F32)],
        compiler_params=_params("arbitrary", "arbitrary", "arbitrary"),
        name="sb_prompt",
    )(q, k, v)


def _sb_sample_kernel(q_ref, kn_ref, vn_ref, kc_ref, vc_ref, o_ref, z_ref, w_ref, carry_ref, acc_ref):
    jj = pl.program_id(1)
    n = DEC_SEQ
    tk = kc_ref.shape[0] // C_HEADS
    scale = np.float32(C_DIM ** -0.5)
    col = lambda h: slice(h * C_DIM, (h + 1) * C_DIM)

    def all_heads(k_of, v_of, width, mask):
        for h in range(C_HEADS):
            z_ref[h * n:(h + 1) * n, 0:width] = _dot_nt(q_ref[:, col(h)], k_of(h)) * scale
        z = z_ref[:, 0:width]
        if mask is not None:
            z = jnp.where(mask, z, NEG)
        w, carry = _sb_weights(z, carry_ref[...], _upper(width))
        carry_ref[...] = carry
        w_ref[:, 0:width] = w.astype(BF16)
        for h in range(C_HEADS):
            acc_ref[:, col(h)] += _dot(w_ref[h * n:(h + 1) * n, 0:width], v_of(h))

    @pl.when(jj == 0)
    def _():
        carry_ref[...] = jnp.zeros_like(carry_ref)
        acc_ref[...] = jnp.zeros_like(acc_ref)
        r_i = lax.broadcasted_iota(jnp.int32, (C_HEADS * n, n), 0) % n
        c_i = lax.broadcasted_iota(jnp.int32, (C_HEADS * n, n), 1)
        all_heads(lambda h: kn_ref[:, col(h)], lambda h: vn_ref[:, col(h)], n, c_i < r_i)

    all_heads(lambda h: kc_ref[pl.ds(h, tk, stride=C_HEADS), :].astype(BF16),
              lambda h: vc_ref[pl.ds(h, tk, stride=C_HEADS), :].astype(BF16), tk, None)

    @pl.when(jj == pl.num_programs(1) - 1)
    def _():
        o_ref[...] = acc_ref[...].astype(o_ref.dtype)


def _sb_sample(q, k, v, cache_k, cache_v, layer):
    n = DEC_SEQ
    off = BATCH * SEQ // n
    past = cache_k.shape[2] // C_HEADS
    tk = _tile(past, SB_SAMPLE_TK)
    n_kb = past // tk
    row_spec = pl.BlockSpec((n, C_WIDTH), lambda b, j: (b + off, 0))
    cache_spec = pl.BlockSpec((None, None, tk * C_HEADS, C_DIM), lambda b, j: (layer, b, n_kb - 1 - j, 0))
    return pl.pallas_call(
        _sb_sample_kernel,
        grid=(DEC_BATCH, n_kb),
        in_specs=[row_spec, row_spec, row_spec, cache_spec, cache_spec],
        out_specs=pl.BlockSpec((n, C_WIDTH), lambda b, j: (b, 0)),
        out_shape=jax.ShapeDtypeStruct((DEC_BATCH * n, C_WIDTH), BF16),
        scratch_shapes=[pltpu.VMEM((C_HEADS * n, tk), F32), pltpu.VMEM((C_HEADS * n, tk), BF16),
                        pltpu.VMEM((C_HEADS * n, 1), F32), pltpu.VMEM((n, C_WIDTH), F32)],
        compiler_params=_params("arbitrary", "arbitrary"),
        name="sb_sample",
    )(q, k, v, cache_k, cache_v)


def _pack_bf16_pairs(x):
    half = x.shape[1] // 2
    bits = lax.bitcast_convert_type(x.astype(BF16).astype(F32), jnp.uint32)
    return (bits[:, :half] & jnp.uint32(0xFFFF0000)) | (bits[:, half:] >> 16)


def _unpack_bf16_pairs(w):
    hi = lax.bitcast_convert_type(w & jnp.uint32(0xFFFF0000), F32)
    lo = lax.bitcast_convert_type(w << 16, F32)
    return jnp.concatenate([hi, lo], axis=1).astype(BF16)


def _outproj_kernel(*refs, n_lhs, n_prompt_tiles, n_x):
    lhs = refs[:2 * n_lhs]
    x_refs = refs[2 * n_lhs:2 * n_lhs + n_x]
    w_ref, g_ref, b_ref, wr_ref, x1_ref, x1p_ref, lg_ref = refs[2 * n_lhs + n_x:]
    is_prompt = pl.program_id(0) < n_prompt_tiles
    tm = x1_ref.shape[0]
    half = tm // 2
    for rows in (slice(0, half), slice(half, tm)):
        k0 = 0
        y = None
        for a_p, a_s in zip(lhs[0::2], lhs[1::2]):
            kk = a_p.shape[1]
            part = _dot(jnp.where(is_prompt, a_p[rows, :], a_s[rows, :]), w_ref[k0:k0 + kk, :])
            y = part if y is None else y + part
            k0 += kk
        x1 = _layer_norm(np.float32(ALPHA) * _row_value(x_refs, is_prompt, rows) + y, g_ref[...], b_ref[...])
        x1_ref[rows, :] = x1
        x1p_ref[rows, :] = _pack_bf16_pairs(x1)
        lg_ref[:, rows] = lax.dot_general(wr_ref[...], x1, (((1,), (1,)), ((), ())), preferred_element_type=F32,
                                          precision=lax.Precision.HIGHEST)


def _outproj_ln(lhs, w, x, g, b, w_router_t, name):
    t = sum(a.shape[0] for a in x) if isinstance(x, tuple) else x.shape[0]
    tm = MOE_TM
    n_pt = lhs[0][0].shape[0] // tm
    assert all(a_p.shape[0] == n_pt * tm and a_s.shape[0] == t - n_pt * tm for a_p, a_s in lhs)
    full = lambda a: pl.BlockSpec(a.shape, lambda i: (0,) * a.ndim)
    lhs_specs, lhs_args = [], []
    for pair in lhs:
        specs, args = _row_specs(pair, tm, n_pt)
        lhs_specs += specs
        lhs_args += args
    x_specs, x_args = _row_specs(x, tm, n_pt)
    return pl.pallas_call(
        functools.partial(_outproj_kernel, n_lhs=len(lhs), n_prompt_tiles=n_pt, n_x=len(x_args)),
        grid=(t // tm,),
        in_specs=lhs_specs + x_specs + [full(w), full(g), full(b), full(w_router_t)],
        out_specs=[pl.BlockSpec((tm, D_MODEL), lambda i: (i, 0)), pl.BlockSpec((tm, D_MODEL // 2), lambda i: (i, 0)),
                   pl.BlockSpec((N_EXPERTS, tm), lambda i: (0, i))],
        out_shape=[jax.ShapeDtypeStruct((t, D_MODEL), F32), jax.ShapeDtypeStruct((t, D_MODEL // 2), jnp.uint32),
                   jax.ShapeDtypeStruct((N_EXPERTS, t), F32)],
        compiler_params=_params("arbitrary"),
        name=name,
    )(*lhs_args, *x_args, w, g, b, w_router_t)


MOE_TM = 256
EXPERT_BM = 512


def _route_select(lg, bias):
    e, tm = lg.shape
    ninf = np.float32(-np.inf)
    shift = int(np.log2(E_PER_GROUP))
    scores = jax.nn.sigmoid(lg)
    sel = scores + bias
    e_id = lax.broadcasted_iota(jnp.int32, (e, tm), 0)
    g_id = lax.shift_right_logical(e_id, shift)
    g3 = sel.reshape(N_GROUPS, E_PER_GROUP, tm)
    i3 = lax.broadcasted_iota(jnp.int32, g3.shape, 1)
    m1 = g3.max(axis=1, keepdims=True)
    first = jnp.where(g3 == m1, i3, E_PER_GROUP).min(axis=1, keepdims=True)
    m2 = jnp.where(i3 == first, ninf, g3).max(axis=1, keepdims=True)
    grp = jnp.broadcast_to(m1 + m2, g3.shape).reshape(e, tm)
    gsel = jnp.zeros((e, tm), jnp.int32)
    for _ in range(TOPK_GROUPS):
        m = grp.max(axis=0, keepdims=True)
        first = jnp.where(grp == m, e_id, e).min(axis=0, keepdims=True)
        chosen = g_id == lax.shift_right_logical(first, shift)
        gsel = jnp.where(chosen, 1, gsel)
        grp = jnp.where(chosen, ninf, grp)
    sel = jnp.where(gsel > 0, sel, NEG)
    picks = []
    for _ in range(TOP_K):
        m = sel.max(axis=0, keepdims=True)
        first = jnp.where(sel == m, e_id, e).min(axis=0, keepdims=True)
        oh = e_id == first
        picks.append(oh)
        sel = jnp.where(oh, ninf, sel)
    return scores, picks


def _route_kernel(lg_ref, bias_ref, dest_ref, gate_ref, cnt_ref, pad_ref, counts_ref, pads_ref, run_ref):
    phase = pl.program_id(0)
    i = pl.program_id(1)
    e, tm = lg_ref.shape
    scores, picks = _route_select(lg_ref[...], bias_ref[...])
    mask = jnp.zeros((e, tm), F32)
    for oh in picks:
        mask = jnp.where(oh, 1.0, mask)

    @pl.when((phase == 0) & (i == 0))
    def _():
        counts_ref[...] = jnp.zeros_like(counts_ref)

    @pl.when(phase == 0)
    def _():
        counts_ref[...] += mask.sum(axis=1, keepdims=True)

    @pl.when((phase == 1) & (i == 0))
    def _():
        counts = counts_ref[...]
        padded = jnp.ceil(counts * np.float32(1.0 / EXPERT_BM)) * np.float32(EXPERT_BM)
        r_i = lax.broadcasted_iota(jnp.int32, (e, e), 0)
        c_i = lax.broadcasted_iota(jnp.int32, (e, e), 1)
        before = (c_i < r_i).astype(F32)
        starts = jnp.dot(before, jnp.broadcast_to(padded, (e, LANES)), preferred_element_type=F32,
                         precision=lax.Precision.HIGHEST)
        pads_ref[...] = starts[:, 0:1]
        run_ref[...] = jnp.zeros_like(run_ref)
        cnt_ref[...] = jnp.broadcast_to(counts, cnt_ref.shape)
        pad_ref[...] = starts

    @pl.when(phase == 1)
    def _():
        r_i = lax.broadcasted_iota(jnp.int32, (tm, tm), 0)
        c_i = lax.broadcasted_iota(jnp.int32, (tm, tm), 1)
        earlier = (r_i < c_i).astype(BF16)
        slot = pads_ref[...] + run_ref[...] + _dot(mask.astype(BF16), earlier)
        run_ref[...] += mask.sum(axis=1, keepdims=True)
        k_i = lax.broadcasted_iota(jnp.int32, (TOP_K, tm), 0)
        dest = jnp.zeros((TOP_K, tm), F32)
        gate = jnp.zeros((TOP_K, tm), F32)
        total = jnp.zeros((1, tm), F32)
        for k, oh in enumerate(picks):
            d_k = jnp.where(oh, slot, 0.0).sum(axis=0, keepdims=True)
            g_k = jnp.where(oh, scores, 0.0).sum(axis=0, keepdims=True)
            total = total + g_k
            dest = jnp.where(k_i == k, d_k, dest)
            gate = jnp.where(k_i == k, g_k, gate)
        dest_ref[...] = dest.astype(jnp.int32)
        gate_ref[...] = gate / total * np.float32(ROUTED_SCALE)


def _route(lg_t, bias):
    e, t = lg_t.shape
    tm = MOE_TM
    n_t = t // tm
    tile = pl.BlockSpec((TOP_K, tm), lambda p, i: (0, i * p))
    meta = pl.BlockSpec((e, LANES), lambda p, i: (0, 0))
    return pl.pallas_call(
        _route_kernel,
        grid=(2, n_t),
        in_specs=[pl.BlockSpec((e, tm), lambda p, i: (0, i)), pl.BlockSpec((e, 1), lambda p, i: (0, 0))],
        out_specs=[tile, tile, meta, meta],
        out_shape=[jax.ShapeDtypeStruct((TOP_K, t), jnp.int32), jax.ShapeDtypeStruct((TOP_K, t), F32),
                   jax.ShapeDtypeStruct((e, LANES), F32), jax.ShapeDtypeStruct((e, LANES), F32)],
        scratch_shapes=[pltpu.VMEM((e, 1), F32), pltpu.VMEM((e, 1), F32), pltpu.VMEM((e, 1), F32)],
        compiler_params=_params("arbitrary", "arbitrary"),
        name="route",
    )(lg_t, bias)


def _row_copy(src, src_row, dst, dst_row, sem):
    return pltpu.make_async_copy(src.at[pl.ds(src_row, 1)], dst.at[pl.ds(dst_row, 1)], sem)


def _scatter_kernel(last_ref, nu_ref, dest_ref, x_ref, xs_out, zeros_ref, sem, zsem):
    i = pl.program_id(0)
    tm = x_ref.shape[0]
    n_blocks = xs_out.shape[0] // EXPERT_BM

    @pl.when(i == 0)
    def _():
        zeros_ref[...] = jnp.zeros_like(zeros_ref)

        def zero_copy(row):
            return pltpu.make_async_copy(zeros_ref, xs_out.at[pl.ds(pl.multiple_of(row, EXPERT_BM), EXPERT_BM)], zsem)

        def tail_row(b):
            return (nu_ref[0] + b) * EXPERT_BM

        def each(fn, n, act):
            def body(b, c):
                row = fn(b)

                @pl.when(row >= 0)
                def _():
                    act(zero_copy(row))
                return c
            lax.fori_loop(0, n, body, 0)

        start = lambda fn, n: each(fn, n, lambda c: c.start())
        wait = lambda fn, n: each(fn, n, lambda c: c.wait())

        start(lambda e: last_ref[e], N_EXPERTS)
        wait(lambda e: last_ref[e], N_EXPERTS)
        start(tail_row, n_blocks - nu_ref[0])
        wait(tail_row, n_blocks - nu_ref[0])

    def issue(t, c):
        for k in range(TOP_K):
            _row_copy(x_ref, t, xs_out, dest_ref[k * tm + t], sem).start(priority=k % 2)
        return c

    def drain(t, c):
        for k in range(TOP_K):
            _row_copy(x_ref, t, xs_out, dest_ref[k * tm + t], sem).wait()
        return c

    lax.fori_loop(0, tm, issue, 0)
    lax.fori_loop(0, tm, drain, 0)


def _scatter(last_row, n_used, dest_flat, x1p, n_slots):
    t, w = x1p.shape
    tm = MOE_TM
    grid_spec = pltpu.PrefetchScalarGridSpec(
        num_scalar_prefetch=2,
        grid=(t // tm,),
        in_specs=[pl.BlockSpec((TOP_K * tm,), lambda i, lr, nu: (i,), memory_space=pltpu.SMEM),
                  pl.BlockSpec((tm, w), lambda i, lr, nu: (i, 0))],
        out_specs=pl.BlockSpec(memory_space=pl.ANY),
        scratch_shapes=[pltpu.VMEM((EXPERT_BM, w), x1p.dtype), pltpu.SemaphoreType.DMA(()),
                        pltpu.SemaphoreType.DMA(())],
    )
    return pl.pallas_call(
        _scatter_kernel,
        grid_spec=grid_spec,
        out_shape=jax.ShapeDtypeStruct((n_slots, w), x1p.dtype),
        compiler_params=_params("arbitrary"),
        name="scatter",
    )(last_row, n_used, dest_flat, x1p)


def _expert_kernel(be_ref, nu_ref, slot_ref, nxt_ref, x_ref, wg_hbm, wu_hbm, wd_hbm, y_ref,
                   wgf, wuf, wdf, wgb, wub, wdb, sem, *, layer):
    i = pl.program_id(0)
    active = i < nu_ref[0]
    first = (i == 0) | (be_ref[i] != be_ref[jnp.maximum(i - 1, 0)])

    def weight_copies(e, s):
        return [pltpu.make_async_copy(src.at[layer, e], dst.at[s], sem.at[s])
                for src, dst in ((wg_hbm, wgf), (wu_hbm, wuf), (wd_hbm, wdf))]

    @pl.when(active & (i == 0))
    def _():
        for c in weight_copies(be_ref[0], slot_ref[0]):
            c.start()

    @pl.when(active & first)
    def _():
        s = slot_ref[i]
        for c in weight_copies(be_ref[i], s):
            c.wait()
        wgb[...] = wgf[s].astype(BF16)
        wub[...] = wuf[s].astype(BF16)
        wdb[...] = wdf[s].astype(BF16)

        @pl.when(nxt_ref[i] >= 0)
        def _():
            for c in weight_copies(nxt_ref[i], 1 - s):
                c.start()

    @pl.when(active)
    def _():
        x = _unpack_bf16_pairs(x_ref[...])
        g = _dot(x, wgb[...])
        h = (g * jax.nn.sigmoid(g)) * _dot(x, wub[...])
        y_ref[...] = _pack_bf16_pairs(_dot(h.astype(BF16), wdb[...]))

    @pl.when(jnp.logical_not(active))
    def _():
        y_ref[...] = jnp.zeros_like(y_ref)


def _experts(block_e, n_used, slot, nxt, xs, layer, wg, wu, wd):
    n_slots = xs.shape[0]
    n_blocks = n_slots // EXPERT_BM
    ff = wg.shape[3]
    any_spec = pl.BlockSpec(memory_space=pl.ANY)
    grid_spec = pltpu.PrefetchScalarGridSpec(
        num_scalar_prefetch=4,
        grid=(n_blocks,),
        in_specs=[pl.BlockSpec((EXPERT_BM, D_MODEL // 2), lambda i, be, nu, sl, nx: (jnp.minimum(i, nu[0] - 1), 0)),
                  any_spec, any_spec, any_spec],
        out_specs=pl.BlockSpec((EXPERT_BM, D_MODEL // 2), lambda i, be, nu, sl, nx: (i, 0)),
        scratch_shapes=[pltpu.VMEM((2, D_MODEL, ff), F32), pltpu.VMEM((2, D_MODEL, ff), F32),
                        pltpu.VMEM((2, ff, D_MODEL), F32),
                        pltpu.VMEM((D_MODEL, ff), BF16), pltpu.VMEM((D_MODEL, ff), BF16),
                        pltpu.VMEM((ff, D_MODEL), BF16), pltpu.SemaphoreType.DMA((2,))],
    )
    return pl.pallas_call(
        functools.partial(_expert_kernel, layer=layer),
        grid_spec=grid_spec,
        out_shape=jax.ShapeDtypeStruct((n_slots, D_MODEL // 2), jnp.uint32),
        compiler_params=_params("arbitrary"),
        name="experts",
    )(block_e, n_used, slot, nxt, xs, wg, wu, wd)


def _ffn_ln_kernel(dest_ref, xp_ref, x_ref, gate_ref, y_hbm, wg_ref, wu_ref, wd_ref, g_ref, b_ref, o_ref, buf, sem):
    tm = x_ref.shape[0]

    def issue(t, c):
        for k in range(TOP_K):
            _row_copy(y_hbm, dest_ref[k * tm + t], buf.at[k], t, sem).start(priority=k % 2)
        return c

    def drain(t, c):
        for k in range(TOP_K):
            _row_copy(y_hbm, dest_ref[k * tm + t], buf.at[k], t, sem).wait()
        return c

    lax.fori_loop(0, tm, issue, 0)
    xb = _unpack_bf16_pairs(xp_ref[...])
    gte = _dot(xb, wg_ref[...])
    h = (gte * jax.nn.sigmoid(gte)) * _dot(xb, wu_ref[...])
    shared = _dot(h.astype(BF16), wd_ref[...])
    lax.fori_loop(0, tm, drain, 0)
    lo_half = hi_half = None
    for k in range(TOP_K):
        w = buf[k]
        g = gate_ref[:, k:k + 1]
        hi = lax.bitcast_convert_type(w & jnp.uint32(0xFFFF0000), F32) * g
        lo = lax.bitcast_convert_type(w << 16, F32) * g
        hi_half = hi if hi_half is None else hi_half + hi
        lo_half = lo if lo_half is None else lo_half + lo
    routed = jnp.concatenate([hi_half, lo_half], axis=1)
    z = np.float32(ALPHA) * x_ref[...] + (routed + shared)
    o_ref[...] = _layer_norm(z, g_ref[...], b_ref[...])


def _ffn_ln(dest_flat, x1p, x1, gate_t, y, wg, wu, wd, g, b):
    t = x1.shape[0]
    tm = MOE_TM
    row = lambda a: pl.BlockSpec((tm, a.shape[1]), lambda i: (i, 0))
    full = lambda a: pl.BlockSpec(a.shape, lambda i: (0,) * a.ndim)
    return pl.pallas_call(
        _ffn_ln_kernel,
        grid=(t // tm,),
        in_specs=[pl.BlockSpec((TOP_K * tm,), lambda i: (i,), memory_space=pltpu.SMEM),
                  row(x1p), row(x1), row(gate_t), pl.BlockSpec(memory_space=pl.ANY),
                  full(wg), full(wu), full(wd), full(g), full(b)],
        out_specs=row(x1),
        out_shape=jax.ShapeDtypeStruct((t, D_MODEL), F32),
        scratch_shapes=[pltpu.VMEM((TOP_K, tm, D_MODEL // 2), jnp.uint32), pltpu.SemaphoreType.DMA(())],
        compiler_params=_params("arbitrary"),
        name="ffn_ln",
    )(dest_flat, x1p, x1, gate_t, y, wg, wu, wd, g, b)


def _ple_kernel(x_ref, wp_ref, wg_ref, *rest, n_prompt_tiles, split, n_p):
    p_refs, o_refs = rest[:n_p], rest[n_p:]
    i = pl.program_id(0)
    x = x_ref[...]
    gate = jax.nn.sigmoid(_dot(x.astype(BF16), wg_ref[...]))
    emb = _dot(_row_value(p_refs, i < n_prompt_tiles).astype(BF16), wp_ref[...])
    out = x + emb * gate
    if not split:
        o_refs[0][...] = out
    else:

        @pl.when(i < n_prompt_tiles)
        def _():
            o_refs[0][...] = out

        @pl.when(i >= n_prompt_tiles)
        def _():
            o_refs[1][...] = out


def _ple(x2, p, w_ple, w_gate, n_prompt, split):
    t = x2.shape[0]
    tm = _tile(np.gcd(n_prompt, t - n_prompt), 256)
    n_pt = n_prompt // tm
    row = lambda a: pl.BlockSpec((tm, a.shape[1]), lambda i: (i, 0))
    full = lambda a: pl.BlockSpec(a.shape, lambda i: (0,) * a.ndim)
    p_specs, p_args = _row_specs(p, tm, n_pt)
    if not split:
        out_specs = [row(x2)]
        out_shape = [jax.ShapeDtypeStruct((t, D_MODEL), F32)]
    else:
        out_specs = [pl.BlockSpec((tm, D_MODEL), lambda i: (jnp.minimum(i, n_pt - 1), 0)),
                     pl.BlockSpec((tm, D_MODEL), lambda i: (jnp.maximum(i - n_pt, 0), 0))]
        out_shape = [jax.ShapeDtypeStruct((n_prompt, D_MODEL), F32),
                     jax.ShapeDtypeStruct((t - n_prompt, D_MODEL), F32)]
    return pl.pallas_call(
        functools.partial(_ple_kernel, n_prompt_tiles=n_pt, split=split, n_p=len(p_args)),
        grid=(t // tm,),
        in_specs=[row(x2), full(w_ple), full(w_gate)] + p_specs,
        out_specs=out_specs, out_shape=out_shape,
        compiler_params=_params("arbitrary"),
        name="ple",
    )(x2, w_ple, w_gate, *p_args)


def _rel_bias(table, qpos, kpos):
    idx = np.clip(qpos[:, None] - kpos[None, :], -REL_CLIP, REL_CLIP) + REL_CLIP
    return table[:, idx]


def _band_prompt_bias(table):
    tq, width = BAND_TQ, BAND_NWB * BAND_TQ
    span = width + tq - 1
    rel = (BAND_NWB - 1) * tq + tq - 1 - np.arange(span)
    strip = table[:, np.clip(rel, -REL_CLIP, REL_CLIP) + REL_CLIP]
    u = jnp.concatenate([strip[:, tq - 1:], strip[:, :1], strip[:, :tq - 1]], axis=1)
    flat = jnp.tile(u, (1, tq))[:, :tq * span]
    return flat.reshape(table.shape[0], tq, span)[:, :, :width]


def _band_mask(qpos, kpos):
    dc = qpos // CHUNK - kpos // CHUNK
    return (kpos >= 0) & (dc >= 0) & (dc <= B_PREV_CHUNKS)


def _post_block(x, lhs, w_out, p, ln1_g, ln1_b, ln2_g, ln2_b, w_router, b_router,
                layer, w_gate, w_up, w_down, ws_gate, ws_up, ws_down, w_ple, w_ple_gate, name, n_prompt, split):
    row = lambda a: a.reshape(1, -1)
    x1, x1p, lg_t = _outproj_ln(lhs, w_out, x, row(ln1_g), row(ln1_b), w_router.T, name)
    t = x1.shape[0]
    dest, gate, cnt, pad = _route(lg_t, b_router.astype(F32).reshape(N_EXPERTS, 1))

    n_blocks = -(-(t * TOP_K + N_EXPERTS * (EXPERT_BM - 1)) // EXPERT_BM)
    counts = cnt[:, 0].astype(jnp.int32)
    pad_end = pad[:, 0].astype(jnp.int32) + (counts + EXPERT_BM - 1) // EXPERT_BM * EXPERT_BM
    block_e = jnp.minimum((pad_end[:, None] <= jnp.arange(n_blocks)[None, :] * EXPERT_BM).sum(axis=0),
                          N_EXPERTS - 1).astype(jnp.int32)
    n_used = (pad_end[-1:] // EXPERT_BM).astype(jnp.int32)
    nonempty = counts > 0
    e_ids = jnp.arange(N_EXPERTS)
    later_ids = jnp.where(nonempty[None, :] & (e_ids[None, :] > e_ids[:, None]), e_ids[None, :], N_EXPERTS).min(axis=1)
    nxt_of = jnp.where(later_ids < N_EXPERTS, later_ids, -1)
    slot_of = (jnp.cumsum(nonempty) - 1) % 2
    in_block = (block_e[:, None] == e_ids[None, :]).astype(jnp.int32)
    nxt = (in_block * nxt_of[None, :]).sum(axis=1).astype(jnp.int32)
    slot = (in_block * slot_of[None, :]).sum(axis=1).astype(jnp.int32)
    dest_flat = dest.reshape(TOP_K, t // MOE_TM, MOE_TM).transpose(1, 0, 2).reshape(-1)

    last_row = jnp.where(counts > 0, pad_end - EXPERT_BM, -1).astype(jnp.int32)
    xs = _scatter(last_row, n_used, dest_flat, x1p, n_blocks * EXPERT_BM)
    y = _experts(block_e, n_used, slot, nxt, xs, layer, w_gate, w_up, w_down)
    x2 = _ffn_ln(dest_flat, x1p, x1, gate.T, y, ws_gate, ws_up, ws_down, row(ln2_g), row(ln2_b))
    return _ple(x2, p, w_ple, w_ple_gate, n_prompt, split)


def kernel(x_prompt, x_sample, cache_b_k, cache_b_v, cache_c_k, cache_c_v, p_prompt, p_sample, w_in_ab, w_out_ab, sgu_w, sgu_b, sgu_ln_g, sgu_ln_b, rel_bias_tab, w_in_c, w_out_c, ln_mix_g, ln_mix_b, ln_ffn_g, ln_ffn_b, w_router, b_router, w_gate, w_up, w_down, ws_gate, ws_up, ws_down, w_ple, w_ple_gate):
    n_p = BATCH * SEQ
    n_s = DEC_BATCH * DEC_SEQ
    x = (x_prompt.reshape(n_p, D_MODEL), x_sample.reshape(n_s, D_MODEL))
    p_rows = lambda i: (p_prompt[i].reshape(n_p, PLE_DIM), p_sample[i].reshape(n_s, PLE_DIM))
    bf = lambda a: a.astype(BF16)
    outs = {k: [] for k in ("bk_p", "bv_p", "bk_s", "bv_s", "av_s", "ck_p", "cv_p", "ck_s", "cv_s")}
    kv_outs = ((F32, True), (BF16, False))

    for i in range(DEPTH):
        j = i // 2
        if i % 2 == 0:
            plain = lambda d: ((d, False),)
            hu, hv, q, k_p, k_s, kb16, v_p, v_s, vb16 = _proj(
                x, bf(w_in_ab[j]), 0,
                [(0, A_WIDTH, plain(F32)), (A_WIDTH, A_WIDTH, plain(F32)), (2 * A_WIDTH, B_WIDTH, plain(BF16)),
                 (2 * A_WIDTH + B_WIDTH, B_WIDTH, kv_outs), (2 * A_WIDTH + 2 * B_WIDTH, B_WIDTH, kv_outs)],
                "proj_ab", n_p)

            bs_t = sgu_b[j].T
            a_p, = _sgu(hu, hv, sgu_w[j], bs_t, sgu_ln_g[j], sgu_ln_b[j], A_CHUNK, 0, n_p // A_CHUNK, False)
            a_s, va_s = _sgu(hu, hv, sgu_w[j][:, :DEC_SEQ, :DEC_SEQ], bs_t[:DEC_SEQ], sgu_ln_g[j], sgu_ln_b[j],
                             DEC_SEQ, n_p, DEC_BATCH, True)

            b_p = _band_prompt(q, kb16, vb16, _band_prompt_bias(rel_bias_tab[j]))
            n_cache = cache_b_k.shape[2]
            qs = PAST_LEN + np.arange(DEC_SEQ)
            kc = PAST_LEN - n_cache + np.arange(n_cache)
            b_s = _band_sample(
                q, kb16, vb16, cache_b_k.reshape(N_EVEN, DEC_BATCH, n_cache * B_HEADS, B_DIM),
                cache_b_v.reshape(N_EVEN, DEC_BATCH, n_cache * B_HEADS, B_DIM), j,
                _rel_bias(rel_bias_tab[j], qs, kc), _rel_bias(rel_bias_tab[j], qs, qs),
                jnp.asarray(_band_mask(qs[:, None], kc[None, :]), F32),
                jnp.asarray(_band_mask(qs[:, None], qs[None, :]), F32))

            keep = min(B_WINDOW, SEQ)
            outs["bk_p"].append(k_p.reshape(BATCH, SEQ, B_HEADS, B_DIM)[:, SEQ - keep:])
            outs["bv_p"].append(v_p.reshape(BATCH, SEQ, B_HEADS, B_DIM)[:, SEQ - keep:])
            outs["bk_s"].append(k_s.reshape(DEC_BATCH, DEC_SEQ, B_HEADS, B_DIM))
            outs["bv_s"].append(v_s.reshape(DEC_BATCH, DEC_SEQ, B_HEADS, B_DIM))
            outs["av_s"].append(va_s.reshape(DEC_BATCH, DEC_SEQ, A_HEADS, A_DIM))
            lhs, w_out, name = [(a_p, a_s), (b_p, b_s)], bf(w_out_ab[j]), "outproj_ab"
        else:
            w_in = bf(w_in_c[j])
            q, k_p, k_s, kb16 = _proj(x, w_in, 0, [(0, C_WIDTH, ((BF16, False),)), (C_WIDTH, C_WIDTH, kv_outs)],
                                      "proj_cqk", n_p)
            v_p, v_s, vb16 = _proj(x, w_in, 2, [(0, C_WIDTH, kv_outs)], "proj_cv", n_p)
            o_p = _sb_prompt(q, kb16, vb16)
            rows = lambda c: c.reshape(N_ODD, DEC_BATCH, PAST_LEN * C_HEADS, C_DIM)
            o_s = _sb_sample(q, kb16, vb16, rows(cache_c_k), rows(cache_c_v), j)
            outs["ck_p"].append(k_p.reshape(BATCH, SEQ, C_HEADS, C_DIM))
            outs["cv_p"].append(v_p.reshape(BATCH, SEQ, C_HEADS, C_DIM))
            outs["ck_s"].append(k_s.reshape(DEC_BATCH, DEC_SEQ, C_HEADS, C_DIM))
            outs["cv_s"].append(v_s.reshape(DEC_BATCH, DEC_SEQ, C_HEADS, C_DIM))
            lhs, w_out, name = [(o_p, o_s)], bf(w_out_c[j]), "outproj_c"

        x = _post_block(x, lhs, w_out, p_rows(i), ln_mix_g[i], ln_mix_b[i], ln_ffn_g[i], ln_ffn_b[i],
                        w_router[i], b_router[i], i, w_gate, w_up, w_down,
                        bf(ws_gate[i]), bf(ws_up[i]), bf(ws_down[i]), bf(w_ple[i]), bf(w_ple_gate[i]), name,
                        n_p, i == DEPTH - 1)
        if i < DEPTH - 1:
            x, = x

    y_p, y_s = x
    st = lambda key: jnp.stack(outs[key])
    return (y_p.reshape(BATCH, SEQ, D_MODEL), y_s.reshape(DEC_BATCH, DEC_SEQ, D_MODEL),
            st("bk_p"), st("bv_p"), st("bk_s"), st("bv_s"), st("av_s"),
            st("ck_p"), st("cv_p"), st("ck_s"), st("cv_s"))
```
